```python
import jax, jax.numpy as jnp
from jax import lax
import numpy as np

D_MODEL = 2048
BATCH = 1
SEQ = 16384
DEPTH = 1
DEC_BATCH = 32
DEC_SEQ = 4
PAST_LEN = 16384
PAGE_SIZE = 128

HEAD_DIM = 128
FOX_HEADS = 8
FOX_KV = 2
NSA_HEADS = 8
NSA_KV = 2
FOX_GROUP = FOX_HEADS // FOX_KV
NSA_GROUP = NSA_HEADS // NSA_KV
MIX_WIDTH = (FOX_HEADS + NSA_HEADS) * HEAD_DIM
N_BRANCH = 3
PLE_DIM = 256
ROPE_THETA = 500000.0
ROT_DIM = HEAD_DIM // 4
Q_BLOCK = 128
CMP_STRIDE = 16
CMP_BLOCK = 2 * CMP_STRIDE
CMP_HIDDEN = 128
SLC_BLOCK = 64
SLC_TOPN = 16
WINDOW = 512
FORGET_BIAS_INIT = 3.0
EPS = 1e-6
NEG_BIG = -1e30
FORCE_BONUS = 1e4

SPLIT_SIZES = (
    FOX_HEADS * HEAD_DIM,
    FOX_KV * HEAD_DIM,
    FOX_KV * HEAD_DIM,
    FOX_HEADS,
    FOX_HEADS * HEAD_DIM,
    NSA_HEADS * HEAD_DIM,
    NSA_KV * HEAD_DIM,
    NSA_KV * HEAD_DIM,
    NSA_KV * HEAD_DIM,
    NSA_KV * HEAD_DIM,
    NSA_KV * HEAD_DIM,
    NSA_KV * HEAD_DIM,
    NSA_HEADS * N_BRANCH,
    NSA_HEADS * HEAD_DIM,
)
IN_WIDTH = sum(SPLIT_SIZES)

kernel_name = 'fox_nsa_parallel_heads_decode_step'


def rms_norm(x, g):
    xf = x.astype(jnp.float32)
    y = xf * lax.rsqrt(jnp.mean(xf * xf, axis=-1, keepdims=True) + EPS)
    return (y * g.astype(jnp.float32)).astype(x.dtype)


def partial_rope(x, pos):
    half = ROT_DIM // 2
    inv = ROPE_THETA ** (-(2.0 / ROT_DIM) * jnp.arange(half, dtype=jnp.float32))
    ang = pos.astype(jnp.float32)[:, None] * inv[None, :]
    cos = jnp.cos(ang)[None, :, None, :]
    sin = jnp.sin(ang)[None, :, None, :]
    xr = x[..., :ROT_DIM].astype(jnp.float32)
    x1, x2 = xr[..., :half], xr[..., half:]
    rot = jnp.concatenate([x1 * cos - x2 * sin, x2 * cos + x1 * sin], axis=-1)
    return jnp.concatenate([rot.astype(x.dtype), x[..., ROT_DIM:]], axis=-1)


def masked_softmax(s, mask):
    s = jnp.where(mask, s.astype(jnp.float32), NEG_BIG)
    m = jnp.max(s, axis=-1, keepdims=True)
    e = jnp.exp(s - m) * mask
    return e / jnp.maximum(jnp.sum(e, axis=-1, keepdims=True), 1e-30)


def project_streams(x, pos, lw):
    B, T, _ = x.shape
    h = rms_norm(x, lw['attn_norm']) @ lw['w_in']
    offs = np.cumsum(SPLIT_SIZES)[:-1].tolist()
    (fq, fk, fv, flg, fz, nq, ck, cv, sk, sv, wk, wv, ng, nz) = jnp.split(h, offs, axis=-1)
    hd = lambda a: a.reshape(B, T, -1, HEAD_DIM)
    fq = rms_norm(hd(fq), lw['fox_q_norm'])
    fk = rms_norm(hd(fk), lw['fox_k_norm'])
    fv = hd(fv)
    logf = jax.nn.log_sigmoid(flg.astype(jnp.float32) + lw['b_forget'].astype(jnp.float32))
    nk = lw['nsa_k_norm']
    nq = partial_rope(rms_norm(hd(nq), lw['nsa_q_norm']), pos)
    ck = partial_rope(rms_norm(hd(ck), nk[0]), pos)
    sk = partial_rope(rms_norm(hd(sk), nk[1]), pos)
    wk = partial_rope(rms_norm(hd(wk), nk[2]), pos)
    gates = jax.nn.sigmoid(ng.astype(jnp.float32)).reshape(B, T, NSA_HEADS, N_BRANCH)
    return fq, fk, fv, logf, fz, nq, ck, hd(cv), sk, hd(sv), wk, hd(wv), gates, nz


def fox_attend(q, cq, t0, k, v, c_all):
    B, Tq = q.shape[:2]
    L = k.shape[1]
    qg = q.reshape(B, Tq, FOX_KV, FOX_GROUP, HEAD_DIM)
    s = jnp.einsum('bqkgd,bskd->bkgqs', qg, k).astype(jnp.float32) * (HEAD_DIM ** -0.5)
    cq_ = jnp.transpose(cq.reshape(B, Tq, FOX_KV, FOX_GROUP), (0, 2, 3, 1))
    cs_ = jnp.transpose(c_all.reshape(B, L, FOX_KV, FOX_GROUP), (0, 2, 3, 1))
    s = s + cq_[..., :, None] - cs_[..., None, :]
    q_pos = t0 + jnp.arange(Tq, dtype=jnp.int32)
    mask = jnp.arange(L, dtype=jnp.int32)[None, :] <= q_pos[:, None]
    p = masked_softmax(s, mask)
    o = jnp.einsum('bkgqs,bskd->bqkgd', p.astype(v.dtype), v)
    return o.reshape(B, Tq, FOX_HEADS, HEAD_DIM)


def compress(k, w1, b1, w2):
    B, L = k.shape[:2]
    n_sub = L // CMP_STRIDE
    sub = k[:, :n_sub * CMP_STRIDE].reshape(B, n_sub, CMP_STRIDE, NSA_KV, HEAD_DIM)
    p0 = jnp.einsum('bnrkd,rde->bnke', sub, w1[:CMP_STRIDE])
    p1 = jnp.einsum('bnrkd,rde->bnke', sub, w1[CMP_STRIDE:])
    h = jax.nn.silu(p0[:, :-1] + p1[:, 1:] + b1)
    return h @ w2


def to_slc_blocks(k):
    B, L = k.shape[:2]
    n_slc = -(-L // SLC_BLOCK)
    kp = jnp.pad(k, ((0, 0), (0, n_slc * SLC_BLOCK - L), (0, 0), (0, 0)))
    return jnp.transpose(kp.reshape(B, n_slc, SLC_BLOCK, NSA_KV, HEAD_DIM), (0, 3, 1, 2, 4))


def nsa_attend(q, gates, t0, kc, vc, ks_blk, vs_blk, kw, vw, win_start):
    B, Tq = q.shape[:2]
    scale = HEAD_DIM ** -0.5
    q_pos = t0 + jnp.arange(Tq, dtype=jnp.int32)
    qg = q.reshape(B, Tq, NSA_KV, NSA_GROUP, HEAD_DIM)
    n_cmp = kc.shape[1]
    cmp_end = jnp.arange(n_cmp, dtype=jnp.int32) * CMP_STRIDE + (CMP_BLOCK - 1)
    s_c = jnp.einsum('bqkgd,bnkd->bkgqn', qg, kc).astype(jnp.float32) * scale
    p_c = masked_softmax(s_c, cmp_end[None, :] <= q_pos[:, None])
    o_c = jnp.einsum('bkgqn,bnkd->bqkgd', p_c.astype(vc.dtype), vc)
    n_slc = ks_blk.shape[2]
    r = SLC_BLOCK // CMP_STRIDE
    imp = jnp.sum(p_c, axis=2)
    imp = jnp.pad(imp, ((0, 0), (0, 0), (0, 0), (1, r * (n_slc + 1) - n_cmp - 1)))
    imp = imp.reshape(B, NSA_KV, Tq, n_slc + 1, r)
    imp_slc = jnp.sum(imp[..., :n_slc, :], axis=-1) + imp[..., 1:, 0]
    blk = jnp.arange(n_slc, dtype=jnp.int32)[None, :]
    cur = (q_pos // SLC_BLOCK)[:, None]
    forced = (blk == 0) | (blk == cur) | (blk == cur - 1)
    score = jnp.where(blk <= cur, imp_slc + FORCE_BONUS * forced, NEG_BIG)
    n_top = min(SLC_TOPN, n_slc)
    _, idx = lax.top_k(score, n_top)
    b_i = jnp.arange(B)[:, None, None, None]
    h_i = jnp.arange(NSA_KV)[:, None, None][None]
    k_sel = ks_blk[b_i, h_i, idx].reshape(B, NSA_KV, Tq, n_top * SLC_BLOCK, HEAD_DIM)
    v_sel = vs_blk[b_i, h_i, idx].reshape(B, NSA_KV, Tq, n_top * SLC_BLOCK, HEAD_DIM)
    tok = (idx[..., None] * SLC_BLOCK + jnp.arange(SLC_BLOCK, dtype=jnp.int32)).reshape(B, NSA_KV, Tq, n_top * SLC_BLOCK)
    s_s = jnp.einsum('bqkgd,bkqmd->bkgqm', qg, k_sel).astype(jnp.float32) * scale
    p_s = masked_softmax(s_s, (tok <= q_pos[:, None])[:, :, None])
    o_s = jnp.einsum('bkgqm,bkqmd->bqkgd', p_s.astype(v_sel.dtype), v_sel)
    Lw = kw.shape[1]
    k_pos = win_start + jnp.arange(Lw, dtype=jnp.int32)
    diff = q_pos[:, None] - k_pos[None, :]
    mask_w = (diff >= 0) & (diff < WINDOW) & (k_pos[None, :] >= 0)
    s_w = jnp.einsum('bqkgd,bskd->bkgqs', qg, kw).astype(jnp.float32) * scale
    p_w = masked_softmax(s_w, mask_w)
    o_w = jnp.einsum('bkgqs,bskd->bqkgd', p_w.astype(vw.dtype), vw)
    g = gates.reshape(B, Tq, NSA_KV, NSA_GROUP, N_BRANCH).astype(o_c.dtype)
    o = g[..., 0:1] * o_c + g[..., 1:2] * o_s + g[..., 2:3] * o_w
    return o.reshape(B, Tq, NSA_HEADS, HEAD_DIM)


def layer_output(x, o_f, fz, o_n, nz, p_i, lw):
    B, T, _ = x.shape
    mix = jnp.concatenate([o_f.reshape(B, T, -1) * jax.nn.silu(fz),
                           o_n.reshape(B, T, -1) * jax.nn.silu(nz)], axis=-1)
    h = x + mix @ lw['w_out']
    gate = jax.nn.sigmoid(rms_norm(h, lw['ple_norm']) @ lw['w_ple_gate'])
    return h + gate * (p_i @ lw['w_ple'])


def prompt_layer(x, p_i, lw):
    B, T, _ = x.shape
    pos = jnp.arange(T, dtype=jnp.int32)
    fq, fk, fv, logf, fz, nq, ck, cv, sk, sv, wk, wv, gates, nz = project_streams(x, pos, lw)
    c_all = jnp.cumsum(logf, axis=1)
    kc = compress(ck, lw['cmp_k_w1'], lw['cmp_k_b1'], lw['cmp_k_w2'])
    vc = compress(cv, lw['cmp_v_w1'], lw['cmp_v_b1'], lw['cmp_v_w2'])
    ks_blk, vs_blk = to_slc_blocks(sk), to_slc_blocks(sv)
    pad = ((0, 0), (WINDOW, 0), (0, 0), (0, 0))
    kw_pad, vw_pad = jnp.pad(wk, pad), jnp.pad(wv, pad)
    nb = T // Q_BLOCK

    def split_blocks(a):
        return jnp.swapaxes(a.reshape(B, nb, Q_BLOCK, *a.shape[2:]), 0, 1)

    def merge_blocks(a):
        return jnp.swapaxes(a, 0, 1).reshape(B, T, *a.shape[3:])

    def body(args):
        t0, fq_b, cq_b, nq_b, g_b = args
        o_f = fox_attend(fq_b, cq_b, t0, fk, fv, c_all)
        kw_b = lax.dynamic_slice_in_dim(kw_pad, t0, WINDOW + Q_BLOCK, axis=1)
        vw_b = lax.dynamic_slice_in_dim(vw_pad, t0, WINDOW + Q_BLOCK, axis=1)
        o_n = nsa_attend(nq_b, g_b, t0, kc, vc, ks_blk, vs_blk, kw_b, vw_b, t0 - WINDOW)
        return o_f, o_n

    t0s = jnp.arange(nb, dtype=jnp.int32) * Q_BLOCK
    o_f, o_n = lax.map(body, (t0s, split_blocks(fq), split_blocks(c_all), split_blocks(nq), split_blocks(gates)))
    y = layer_output(x, merge_blocks(o_f), fz, merge_blocks(o_n), nz, p_i, lw)
    n_win = min(WINDOW, T)
    state = (fk, fv, logf, ck, cv, sk, sv, wk[:, T - n_win:], wv[:, T - n_win:])
    return y, state


def sample_layer(x, p_i, c_fk, c_fv, c_flogf, c_ck, c_cv, c_sk, c_sv, c_wk, c_wv, page_table, lw):
    B, T, _ = x.shape
    pos = PAST_LEN + jnp.arange(T, dtype=jnp.int32)
    fq, fk, fv, logf, fz, nq, ck, cv, sk, sv, wk, wv, gates, nz = project_streams(x, pos, lw)

    def past(cache):
        g = cache[page_table]
        return g.reshape(B, g.shape[1] * PAGE_SIZE, *cache.shape[2:])

    fk_all = jnp.concatenate([past(c_fk), fk], axis=1)
    fv_all = jnp.concatenate([past(c_fv), fv], axis=1)
    c_all = jnp.cumsum(jnp.concatenate([past(c_flogf).astype(jnp.float32), logf], axis=1), axis=1)
    o_f = fox_attend(fq, c_all[:, PAST_LEN:], PAST_LEN, fk_all, fv_all, c_all)
    kc = compress(jnp.concatenate([past(c_ck), ck], axis=1), lw['cmp_k_w1'], lw['cmp_k_b1'], lw['cmp_k_w2'])
    vc = compress(jnp.concatenate([past(c_cv), cv], axis=1), lw['cmp_v_w1'], lw['cmp_v_b1'], lw['cmp_v_w2'])
    ks_blk = to_slc_blocks(jnp.concatenate([past(c_sk), sk], axis=1))
    vs_blk = to_slc_blocks(jnp.concatenate([past(c_sv), sv], axis=1))
    n_buf = c_wk.shape[1]
    kw = jnp.concatenate([c_wk, wk], axis=1)
    vw = jnp.concatenate([c_wv, wv], axis=1)
    o_n = nsa_attend(nq, gates, PAST_LEN, kc, vc, ks_blk, vs_blk, kw, vw, PAST_LEN - n_buf)
    y = layer_output(x, o_f, fz, o_n, nz, p_i, lw)
    state = (fk, fv, logf, ck, cv, sk, sv, kw[:, -n_buf:], vw[:, -n_buf:])
    return y, state


def setup_inputs(seed: int = 0) -> dict:
    key = jax.random.key(seed)
    k = jax.random.split(key, 32)

    def nrm(kk, shape, scale=1.0):
        return scale * jax.random.normal(kk, shape, jnp.float32)

    n_pages = PAST_LEN // PAGE_SIZE
    n_used = DEC_BATCH * n_pages
    n_pool = n_used + max(1, n_used // 4)
    win_buf = min(WINDOW, PAST_LEN)
    fox_kv = (DEPTH, n_pool, PAGE_SIZE, FOX_KV, HEAD_DIM)
    nsa_kv = (DEPTH, n_pool, PAGE_SIZE, NSA_KV, HEAD_DIM)
    win_kv = (DEPTH, DEC_BATCH, win_buf, NSA_KV, HEAD_DIM)
    page_table = jax.random.permutation(k[0], n_pool)[:n_used].reshape(DEC_BATCH, n_pages).astype(jnp.int32)
    return {
        'x_prompt': nrm(k[1], (BATCH, SEQ, D_MODEL)),
        'x_sample': nrm(k[2], (DEC_BATCH, DEC_SEQ, D_MODEL)),
        'cache_fox_k': nrm(k[3], fox_kv),
        'cache_fox_v': nrm(k[4], fox_kv),
        'cache_fox_logf': jax.nn.log_sigmoid(FORGET_BIAS_INIT + nrm(k[5], (DEPTH, n_pool, PAGE_SIZE, FOX_HEADS))),
        'cache_cmp_k': nrm(k[6], nsa_kv),
        'cache_cmp_v': nrm(k[7], nsa_kv),
        'cache_slc_k': nrm(k[8], nsa_kv),
        'cache_slc_v': nrm(k[9], nsa_kv),
        'cache_win_k': nrm(k[10], win_kv),
        'cache_win_v': nrm(k[11], win_kv),
        'page_table': page_table,
        'p_prompt': nrm(k[12], (DEPTH, BATCH, SEQ, PLE_DIM)),
        'p_sample': nrm(k[13], (DEPTH, DEC_BATCH, DEC_SEQ, PLE_DIM)),
        'attn_norm': 1.0 + nrm(k[14], (DEPTH, D_MODEL), 0.1),
        'w_in': nrm(k[15], (DEPTH, D_MODEL, IN_WIDTH), D_MODEL ** -0.5),
        'b_forget': FORGET_BIAS_INIT + nrm(k[16], (DEPTH, FOX_HEADS), 0.1),
        'fox_q_norm': 1.0 + nrm(k[17], (DEPTH, HEAD_DIM), 0.1),
        'fox_k_norm': 1.0 + nrm(k[18], (DEPTH, HEAD_DIM), 0.1),
        'nsa_q_norm': 1.0 + nrm(k[19], (DEPTH, HEAD_DIM), 0.1),
        'nsa_k_norm': 1.0 + nrm(k[20], (DEPTH, N_BRANCH, HEAD_DIM), 0.1),
        'cmp_k_w1': nrm(k[21], (DEPTH, CMP_BLOCK, HEAD_DIM, CMP_HIDDEN), (CMP_BLOCK * HEAD_DIM) ** -0.5),
        'cmp_k_b1': nrm(k[22], (DEPTH, CMP_HIDDEN), 0.02),
        'cmp_k_w2': nrm(k[23], (DEPTH, CMP_HIDDEN, HEAD_DIM), CMP_HIDDEN ** -0.5),
        'cmp_v_w1': nrm(k[24], (DEPTH, CMP_BLOCK, HEAD_DIM, CMP_HIDDEN), (CMP_BLOCK * HEAD_DIM) ** -0.5),
        'cmp_v_b1': nrm(k[25], (DEPTH, CMP_HIDDEN), 0.02),
        'cmp_v_w2': nrm(k[26], (DEPTH, CMP_HIDDEN, HEAD_DIM), CMP_HIDDEN ** -0.5),
        'w_out': nrm(k[27], (DEPTH, MIX_WIDTH, D_MODEL), MIX_WIDTH ** -0.5),
        'ple_norm': 1.0 + nrm(k[28], (DEPTH, D_MODEL), 0.1),
        'w_ple': nrm(k[29], (DEPTH, PLE_DIM, D_MODEL), PLE_DIM ** -0.5),
        'w_ple_gate': nrm(k[30], (DEPTH, D_MODEL, D_MODEL), D_MODEL ** -0.5),
    }


def reference(x_prompt, x_sample, cache_fox_k, cache_fox_v, cache_fox_logf, cache_cmp_k, cache_cmp_v,
              cache_slc_k, cache_slc_v, cache_win_k, cache_win_v, page_table, p_prompt, p_sample,
              attn_norm, w_in, b_forget, fox_q_norm, fox_k_norm, nsa_q_norm, nsa_k_norm,
              cmp_k_w1, cmp_k_b1, cmp_k_w2, cmp_v_w1, cmp_v_b1, cmp_v_w2,
              w_out, ple_norm, w_ple, w_ple_gate):
    y_prompt, y_sample = x_prompt, x_sample
    p_states, s_states = [], []
    for i in range(DEPTH):
        lw = {
            'attn_norm': attn_norm[i], 'w_in': w_in[i], 'b_forget': b_forget[i],
            'fox_q_norm': fox_q_norm[i], 'fox_k_norm': fox_k_norm[i],
            'nsa_q_norm': nsa_q_norm[i], 'nsa_k_norm': nsa_k_norm[i],
            'cmp_k_w1': cmp_k_w1[i], 'cmp_k_b1': cmp_k_b1[i], 'cmp_k_w2': cmp_k_w2[i],
            'cmp_v_w1': cmp_v_w1[i], 'cmp_v_b1': cmp_v_b1[i], 'cmp_v_w2': cmp_v_w2[i],
            'w_out': w_out[i], 'ple_norm': ple_norm[i], 'w_ple': w_ple[i], 'w_ple_gate': w_ple_gate[i],
        }
        y_prompt, st_p = prompt_layer(y_prompt, p_prompt[i], lw)
        p_states.append(st_p)
        y_sample, st_s = sample_layer(y_sample, p_sample[i], cache_fox_k[i], cache_fox_v[i], cache_fox_logf[i],
                                      cache_cmp_k[i], cache_cmp_v[i], cache_slc_k[i], cache_slc_v[i],
                                      cache_win_k[i], cache_win_v[i], page_table, lw)
        s_states.append(st_s)
    (p_fk, p_fv, p_flf, p_ck, p_cv, p_sk, p_sv, p_wk, p_wv) = [jnp.stack(s) for s in zip(*p_states)]
    (s_fk, s_fv, s_flf, s_ck, s_cv, s_sk, s_sv, s_wk, s_wv) = [jnp.stack(s) for s in zip(*s_states)]
    return (y_prompt, y_sample,
            p_fk, p_fv, p_flf, p_ck, p_cv, p_sk, p_sv, p_wk, p_wv,
            s_fk, s_fv, s_flf, s_ck, s_cv, s_sk, s_sv, s_wk, s_wv)
```

```python
import functools

import numpy as np
import jax
import jax.numpy as jnp
from jax import lax
from jax.experimental import pallas as pl
from jax.experimental.pallas import tpu as pltpu

F32 = jnp.float32
BF16 = jnp.bfloat16

HEAD_DIM = 128
FOX_HEADS = 8
FOX_KV = 2
NSA_HEADS = 8
NSA_KV = 2
GROUP = 4
N_BRANCH = 3
ROPE_THETA = 500000.0
ROT_DIM = HEAD_DIM // 4
ROT_HALF = ROT_DIM // 2
CMP_STRIDE = 16
CMP_BLOCK = 2 * CMP_STRIDE
SLC_BLOCK = 64
SLC_SHIFT = 6
SLC_TOPN = 16
WINDOW = 512
EPS = 1e-6
NEG_BIG = -1e30
FORCE_BONUS = 1e4
SEL_OFF = -30000.0
PAGE = 128
SCALE = HEAD_DIM ** -0.5

SPLIT_SIZES = (1024, 256, 256, 8, 1024, 1024, 256, 256, 256, 256, 256, 256, 24, 1024)

CH = 256
C_FQ, C_NQ, C_FZ, C_NZ = 0, 4, 8, 12
C_FK, C_CK, C_SK, C_WK, C_FV, C_CV, C_SV, C_WV, C_SM = 16, 17, 18, 19, 20, 21, 22, 23, 24
N_CH = 25
NP = N_CH * CH
LANES = 128
DEC_PAD = 16

VMEM_LIMIT = 56 * 1024 * 1024


def _cparams(sem):
    return pltpu.CompilerParams(dimension_semantics=sem, vmem_limit_bytes=VMEM_LIMIT)


def _sigmoid(x):
    return 1.0 / (1.0 + jnp.exp(-x))


def _silu(x):
    return x * _sigmoid(x)


def _inproj_kernel(x_ref, g_ref, w_ref, gain_ref, bf_ref, rc_ref, rs1_ref, rs2_ref, o_ref, xn_ref):
    j = pl.program_id(1)

    @pl.when(j == 0)
    def _():
        x = x_ref[...]
        ms = jnp.mean(x * x, axis=-1, keepdims=True)
        xn_ref[...] = (x * lax.rsqrt(ms + EPS) * g_ref[...]).astype(BF16)

    y = jnp.dot(xn_ref[...], w_ref[...], preferred_element_type=F32)

    is_q_rope = (j >= C_NQ) & (j < C_FZ)
    is_k_rope = (j >= C_CK) & (j <= C_WK)
    is_norm_only = (j < C_NQ) | (j == C_FK)
    is_rope = is_q_rope | is_k_rope
    is_raw = ((j >= C_FZ) & (j < C_FK)) | ((j >= C_FV) & (j < C_SM))

    def normed(h):
        yh = y[:, h * LANES:(h + 1) * LANES]
        ms = jnp.mean(yh * yh, axis=-1, keepdims=True)
        return yh * lax.rsqrt(ms + EPS) * gain_ref[:, h * LANES:(h + 1) * LANES]

    @pl.when(is_norm_only)
    def _():
        for h in range(CH // LANES):
            o_ref[:, h * LANES:(h + 1) * LANES] = normed(h)

    @pl.when(is_rope)
    def _():
        for h in range(CH // LANES):
            yn = normed(h)
            lo = pltpu.roll(yn, LANES - ROT_HALF, axis=1)
            hi = pltpu.roll(yn, ROT_HALF, axis=1)
            o_ref[:, h * LANES:(h + 1) * LANES] = yn * rc_ref[...] + lo * rs1_ref[...] + hi * rs2_ref[...]

    @pl.when(is_raw)
    def _():
        o_ref[...] = y

    @pl.when(j == C_SM)
    def _():
        t = y[:, :LANES] + bf_ref[...]
        lane = lax.broadcasted_iota(jnp.int32, t.shape, 1)
        e = jnp.exp(-jnp.abs(t))
        logsig = jnp.minimum(t, 0.0) - jnp.log(1.0 + e)
        o_ref[:, :LANES] = jnp.where(lane < FOX_HEADS, logsig, _sigmoid(t))
        o_ref[:, LANES:] = jnp.zeros_like(t)


def _inproj(x, pos, attn_norm, w_perm, gain, bf_pad, tm):
    R, D = x.shape
    inv = ROPE_THETA ** (-(2.0 / ROT_DIM) * jnp.arange(ROT_HALF, dtype=F32))
    ang = pos.astype(F32)[:, None] * inv[None, :]
    cos, sin = jnp.cos(ang), jnp.sin(ang)
    z = jnp.zeros((R, LANES - ROT_DIM), F32)
    zh = jnp.zeros((R, ROT_HALF), F32)
    rc = jnp.concatenate([cos, cos, z + 1.0], axis=1)
    rs1 = jnp.concatenate([-sin, zh, z], axis=1)
    rs2 = jnp.concatenate([zh, sin, z], axis=1)
    row = lambda i, j: (i, 0)
    return pl.pallas_call(
        _inproj_kernel,
        out_shape=jax.ShapeDtypeStruct((R, NP), F32),
        grid=(R // tm, N_CH),
        in_specs=[
            pl.BlockSpec((tm, D), row),
            pl.BlockSpec((1, D), lambda i, j: (0, 0)),
            pl.BlockSpec((D, CH), lambda i, j: (0, j)),
            pl.BlockSpec((1, CH), lambda i, j: (0, j)),
            pl.BlockSpec((1, LANES), lambda i, j: (0, 0)),
            pl.BlockSpec((tm, LANES), row),
            pl.BlockSpec((tm, LANES), row),
            pl.BlockSpec((tm, LANES), row),
        ],
        out_specs=pl.BlockSpec((tm, CH), lambda i, j: (i, j)),
        scratch_shapes=[pltpu.VMEM((tm, D), BF16)],
        compiler_params=_cparams(("arbitrary", "arbitrary")),
        name="inproj",
    )(x, attn_norm.reshape(1, D), w_perm, gain, bf_pad, rc, rs1, rs2)


def _prep_inproj_params(w_in, b_forget, fox_q_norm, fox_k_norm, nsa_q_norm, nsa_k_norm):
    D = w_in.shape[0]
    offs = np.concatenate([[0], np.cumsum(SPLIT_SIZES)]).tolist()
    seg = [w_in[:, offs[i]:offs[i + 1]] for i in range(len(SPLIT_SIZES))]
    fq, fk, fv, flg, fz, nq, ck, cv, sk, sv, wk, wv, ng, nz = seg
    small = jnp.concatenate([flg, ng, jnp.zeros((D, CH - FOX_HEADS - NSA_HEADS * N_BRANCH), F32)], axis=1)
    w_perm = jnp.concatenate([fq, nq, fz, nz, fk, ck, sk, wk, fv, cv, sv, wv, small], axis=1).astype(BF16)
    zeros = lambda n: jnp.zeros((n,), F32)
    gain = jnp.concatenate([
        jnp.tile(fox_q_norm, FOX_HEADS), jnp.tile(nsa_q_norm, NSA_HEADS), zeros(2048),
        jnp.tile(fox_k_norm, FOX_KV), jnp.tile(nsa_k_norm[0], NSA_KV), jnp.tile(nsa_k_norm[1], NSA_KV),
        jnp.tile(nsa_k_norm[2], NSA_KV), zeros(4 * CH + CH)]).reshape(1, NP)
    bf_pad = jnp.concatenate([b_forget, zeros(LANES - FOX_HEADS)]).reshape(1, LANES)
    return w_perm, gain, bf_pad


def _gather_pages_kernel(pt_ref, *refs):
    n = len(refs) - 1
    o_ref = refs[n]
    for u in range(n):
        o_ref[u] = refs[u][...]


def _gather_logf_pages(cache_t, page_table, pages_per_step):
    B, n_pages = page_table.shape
    P = pages_per_step
    pt = page_table.reshape(-1)

    def src_map(u):
        return lambda b, c, pt_ref: (pt_ref[b * n_pages + c * P + u], 0, 0)

    return pl.pallas_call(
        _gather_pages_kernel,
        out_shape=jax.ShapeDtypeStruct((B, n_pages, FOX_HEADS, PAGE), F32),
        grid_spec=pltpu.PrefetchScalarGridSpec(
            num_scalar_prefetch=1,
            grid=(B, n_pages // P),
            in_specs=[pl.BlockSpec((None, FOX_HEADS, PAGE), src_map(u)) for u in range(P)],
            out_specs=pl.BlockSpec((None, P, FOX_HEADS, PAGE), lambda b, c, pt_ref: (b, c, 0, 0)),
        ),
        compiler_params=_cparams(("arbitrary", "arbitrary")),
        name="gather_logf_pages",
    )(pt, *([cache_t] * P))


def _cumsum_kernel(x_ref, new_ref, o_ref, onew_ref, *, nr):
    hp = lax.Precision.HIGHEST
    x = x_ref[...].reshape(nr * FOX_HEADS, LANES)
    ci = lax.broadcasted_iota(jnp.int32, (LANES, LANES), 0)
    cj = lax.broadcasted_iota(jnp.int32, (LANES, LANES), 1)
    upper = (ci <= cj).astype(F32)
    lastcol = (ci == LANES - 1).astype(F32)
    within = jnp.dot(x, upper, precision=hp, preferred_element_type=F32)
    rowtot = jnp.dot(within, lastcol, precision=hp, preferred_element_type=F32)
    n = nr * FOX_HEADS
    ri = lax.broadcasted_iota(jnp.int32, (n, n), 0)
    rj = lax.broadcasted_iota(jnp.int32, (n, n), 1)
    same_head = (ri & (FOX_HEADS - 1)) == (rj & (FOX_HEADS - 1))
    before = (same_head & (rj < ri)).astype(F32)
    prefix = jnp.dot(before, rowtot, precision=hp, preferred_element_type=F32)
    o_ref[...] = (within + prefix).reshape(nr, FOX_HEADS, LANES)
    total = (prefix + rowtot)[n - FOX_HEADS:, :]
    nw = new_ref[...]
    ti = lax.broadcasted_iota(jnp.int32, (DEC_PAD, DEC_PAD), 0)
    tj = lax.broadcasted_iota(jnp.int32, (DEC_PAD, DEC_PAD), 1)
    onew_ref[...] = total[:, :DEC_PAD] + jnp.dot(nw, (ti <= tj).astype(F32), precision=hp,
                                                  preferred_element_type=F32)


def _cumsum(x4, new):
    B, nr = x4.shape[:2]
    return pl.pallas_call(
        functools.partial(_cumsum_kernel, nr=nr),
        out_shape=(jax.ShapeDtypeStruct(x4.shape, F32), jax.ShapeDtypeStruct(new.shape, F32)),
        grid=(B,),
        in_specs=[pl.BlockSpec((None, nr, FOX_HEADS, LANES), lambda b: (b, 0, 0, 0)),
                  pl.BlockSpec((None, FOX_HEADS, DEC_PAD), lambda b: (b, 0, 0))],
        out_specs=(pl.BlockSpec((None, nr, FOX_HEADS, LANES), lambda b: (b, 0, 0, 0)),
                   pl.BlockSpec((None, FOX_HEADS, DEC_PAD), lambda b: (b, 0, 0))),
        compiler_params=_cparams(("arbitrary",)),
        name="cumsum_logf",
    )(x4, new)


def _flash_kernel(*refs, n_sp, n_src, tq, n_keys, has_bias, has_sel, has_tail, windowed,
                  q_base, k_base, tail_base):
    qi_tab, kj_tab, first_tab, last_tab = refs[:4]
    refs = refs[n_sp:]
    q_ref = refs[0]
    k_srcs = refs[1:1 + n_src]
    v_srcs = refs[1 + n_src:1 + 2 * n_src]
    pos = 1 + 2 * n_src
    cq_ref = cs_ref = sel_ref = kt_ref = vt_ref = cst_ref = None
    if has_bias:
        cq_ref, cs_ref = refs[pos], refs[pos + 1]
        pos += 2
    if has_sel:
        sel_ref = refs[pos]
        pos += 1
    if has_tail:
        kt_ref, vt_ref = refs[pos], refs[pos + 1]
        pos += 2
        if has_bias:
            cst_ref = refs[pos]
            pos += 1
    o_ref = refs[pos]
    qs_ref, m_ref, l_ref, acc_ref = refs[pos + 1:pos + 5]

    step = pl.program_id(2)
    g = pl.program_id(1)
    qi = qi_tab[step]
    kj = kj_tab[step]

    @pl.when(first_tab[step] == 1)
    def _():
        qs_ref[...] = (q_ref[...] * SCALE).astype(BF16)
        m_ref[...] = jnp.full(m_ref.shape, NEG_BIG, F32)
        l_ref[...] = jnp.zeros(l_ref.shape, F32)
        acc_ref[...] = jnp.zeros(acc_ref.shape, F32)

    q_pos = q_base + qi * tq + lax.broadcasted_iota(jnp.int32, (tq, 1), 0)

    def attend(k, v, k_pos, cs, sel_bias):
        kb = k.astype(BF16)
        vb = v.astype(BF16)
        mask = k_pos <= q_pos
        if windowed:
            mask = mask & ((q_pos - k_pos) < WINDOW)
        for h in range(GROUP):
            s = lax.dot_general(qs_ref[:, h * HEAD_DIM:(h + 1) * HEAD_DIM], kb, (((1,), (1,)), ((), ())),
                                preferred_element_type=F32)
            if cs is not None:
                hh = g * GROUP + h
                cq = jnp.sum(jnp.where(lax.broadcasted_iota(jnp.int32, cq_ref.shape, 1) == hh, cq_ref[...], 0.0),
                             axis=1, keepdims=True)
                s = s + (cq - cs[h:h + 1, :])
            if sel_bias is not None:
                s = s + sel_bias
            s = jnp.where(mask, s, NEG_BIG)
            m_prev = m_ref[h]
            m_new = jnp.maximum(m_prev, jnp.max(s, axis=-1, keepdims=True))
            alpha = jnp.exp(m_prev - m_new)
            p = jnp.exp(s - m_new)
            l_ref[h] = alpha * l_ref[h] + jnp.sum(p, axis=-1, keepdims=True)
            acc_ref[h] = alpha * acc_ref[h] + jnp.dot(p.astype(BF16), vb, preferred_element_type=F32)
            m_ref[h] = m_new

    if n_src == 1:
        k, v = k_srcs[0][...], v_srcs[0][...]
    else:
        k = jnp.concatenate([r[...] for r in k_srcs], axis=0)
        v = jnp.concatenate([r[...] for r in v_srcs], axis=0)
    k_pos = k_base + kj * n_keys + lax.broadcasted_iota(jnp.int32, (1, n_keys), 1)
    sel_bias = None
    if has_sel:
        blk = ((kj * n_keys + lax.broadcasted_iota(jnp.int32, (1, n_keys), 1)) >> SLC_SHIFT) & (LANES - 1)
        onehot = (lax.broadcasted_iota(jnp.int32, (LANES, n_keys), 0) == blk).astype(BF16)
        sel_bias = jnp.dot(sel_ref[...], onehot, preferred_element_type=F32)
    attend(k, v, k_pos, cs_ref[...] if has_bias else None, sel_bias)

    @pl.when(last_tab[step] == 1)
    def _():
        if has_tail:
            t_pos = tail_base + lax.broadcasted_iota(jnp.int32, (1, DEC_PAD), 1)
            attend(kt_ref[...], vt_ref[...], t_pos, cst_ref[...] if has_bias else None, None)
        for h in range(GROUP):
            o_ref[:, h * HEAD_DIM:(h + 1) * HEAD_DIM] = acc_ref[h] / jnp.maximum(l_ref[h], 1e-30)


def _flash(q_arr, q_col, tq, n_qblk, B, tables, k_specs, v_specs, kv_args, n_keys, *, prefetch=(),
           bias=None, sel=None, tail=None, windowed=False, q_base=0, k_base=0, tail_base=0, name="flash"):
    n_sp = len(tables) + len(prefetch)
    n_steps = tables[0].shape[0]
    n_src = len(k_specs)
    qmap = lambda b, g, s, qi, *_: (b * n_qblk + qi[s], q_col + g)
    in_specs = [pl.BlockSpec((tq, GROUP * HEAD_DIM), qmap)] + list(k_specs) + list(v_specs)
    args = [q_arr] + list(kv_args)
    if bias is not None:
        in_specs.append(pl.BlockSpec((tq, FOX_HEADS), lambda b, g, s, qi, *_: (b * n_qblk + qi[s], 0)))
        in_specs.append(pl.BlockSpec((None, None, GROUP, n_keys), lambda b, g, s, qi, kj, *_: (b, g, 0, kj[s])))
        args += [bias[0], bias[1]]
    if sel is not None:
        blocks_per_step = n_keys // SLC_BLOCK
        in_specs.append(pl.BlockSpec(
            (None, None, None, tq, LANES),
            lambda b, g, s, qi, kj, *_: (b, g, (kj[s] * blocks_per_step) // LANES, qi[s], 0)))
        args.append(sel)
    if tail is not None:
        k_arr, k_col, v_arr, v_col = tail
        in_specs.append(pl.BlockSpec((DEC_PAD, HEAD_DIM), lambda b, g, s, *_: (b, k_col + g)))
        in_specs.append(pl.BlockSpec((DEC_PAD, HEAD_DIM), lambda b, g, s, *_: (b, v_col + g)))
        args += [k_arr, v_arr]
        if bias is not None:
            in_specs.append(pl.BlockSpec((None, None, GROUP, DEC_PAD), lambda b, g, s, *_: (b, g, 0, 0)))
            args.append(bias[2])
    kern = functools.partial(
        _flash_kernel, n_sp=n_sp, n_src=n_src, tq=tq, n_keys=n_keys, has_bias=bias is not None,
        has_sel=sel is not None, has_tail=tail is not None, windowed=windowed,
        q_base=q_base, k_base=k_base, tail_base=tail_base)
    return pl.pallas_call(
        kern,
        out_shape=jax.ShapeDtypeStruct((B * n_qblk * tq, 2 * GROUP * HEAD_DIM), F32),
        grid_spec=pltpu.PrefetchScalarGridSpec(
            num_scalar_prefetch=n_sp,
            grid=(B, 2, n_steps),
            in_specs=in_specs,
            out_specs=pl.BlockSpec((tq, GROUP * HEAD_DIM), lambda b, g, s, qi, *_: (b * n_qblk + qi[s], g)),
            scratch_shapes=[pltpu.VMEM((tq, GROUP * HEAD_DIM), BF16),
                            pltpu.VMEM((GROUP, tq, 1), F32),
                            pltpu.VMEM((GROUP, tq, 1), F32),
                            pltpu.VMEM((GROUP, tq, HEAD_DIM), F32)],
        ),
        compiler_params=_cparams(("arbitrary", "arbitrary", "arbitrary")),
        name=name,
    )(*tables, *prefetch, *args)


def _causal_tables(n_blk, lookback=None):
    qi, kj, first, last = [], [], [], []
    for i in range(n_blk):
        lo = 0 if lookback is None else max(0, i - lookback)
        for j in range(lo, i + 1):
            qi.append(i)
            kj.append(j)
            first.append(int(j == lo))
            last.append(int(j == i))
    return tuple(jnp.asarray(a, jnp.int32) for a in (qi, kj, first, last))


def _linear_tables(n_steps):
    z = np.zeros((n_steps,), np.int32)
    first, last = z.copy(), z.copy()
    first[0], last[-1] = 1, 1
    return tuple(jnp.asarray(a, jnp.int32) for a in (z, np.arange(n_steps, dtype=np.int32), first, last))


def _cmp_partial_prompt_kernel(*refs):
    xk, xv = refs[:CMP_STRIDE], refs[CMP_STRIDE:2 * CMP_STRIDE]
    w1k_ref, w1v_ref, ok_ref, ov_ref = refs[2 * CMP_STRIDE:]
    for xs, w_ref, o_ref in ((xk, w1k_ref, ok_ref), (xv, w1v_ref, ov_ref)):
        for half in range(2):
            for kv in range(NSA_KV):
                acc = None
                for r in range(CMP_STRIDE):
                    d = jnp.dot(xs[r][:, kv * HEAD_DIM:(kv + 1) * HEAD_DIM].astype(BF16),
                                w_ref[half * CMP_STRIDE + r], preferred_element_type=F32)
                    acc = d if acc is None else acc + d
                c = (half * NSA_KV + kv) * HEAD_DIM
                o_ref[:, c:c + HEAD_DIM] = acc


def _cmp_partial_prompt(big, T, w1k, w1v, tn):
    ns = T // CMP_STRIDE
    view = big.reshape(ns, CMP_STRIDE * NP)
    spec = lambda col: [pl.BlockSpec((tn, CH), (lambda r: (lambda i: (i, r * N_CH + col)))(r))
                        for r in range(CMP_STRIDE)]
    wspec = pl.BlockSpec((CMP_BLOCK, HEAD_DIM, HEAD_DIM), lambda i: (0, 0, 0))
    out = jax.ShapeDtypeStruct((ns, 4 * HEAD_DIM), F32)
    return pl.pallas_call(
        _cmp_partial_prompt_kernel,
        out_shape=(out, out),
        grid=(ns // tn,),
        in_specs=spec(C_CK) + spec(C_CV) + [wspec, wspec],
        out_specs=(pl.BlockSpec((tn, 4 * HEAD_DIM), lambda i: (i, 0)),) * 2,
        compiler_params=_cparams(("arbitrary",)),
        name="cmp_partial_prompt",
    )(*([view] * (2 * CMP_STRIDE)), w1k, w1v)


def _cmp_partial_paged_kernel(pt_ref, *refs, n_pages):
    xk, xv = refs[:n_pages], refs[n_pages:2 * n_pages]
    w1k_ref, w1v_ref, ok_ref, ov_ref = refs[2 * n_pages:]
    sub = PAGE // CMP_STRIDE
    for xs, w_ref, o_ref in ((xk, w1k_ref, ok_ref), (xv, w1v_ref, ov_ref)):
        x = jnp.concatenate([r[...] for r in xs], axis=0).astype(BF16)
        for half in range(2):
            for kv in range(NSA_KV):
                acc = None
                for r in range(CMP_STRIDE):
                    c0 = r * NSA_KV * HEAD_DIM + kv * HEAD_DIM
                    d = jnp.dot(x[:, c0:c0 + HEAD_DIM], w_ref[half * CMP_STRIDE + r],
                                preferred_element_type=F32)
                    acc = d if acc is None else acc + d
                c = (half * NSA_KV + kv) * HEAD_DIM
                o_ref[:, c:c + HEAD_DIM] = acc
    del sub


def _cmp_partial_paged(cache_k, cache_v, page_table, w1k, w1v, pages_per_step):
    B, n_pages = page_table.shape
    P = pages_per_step
    sub = PAGE // CMP_STRIDE
    n_pool = cache_k.shape[0]
    vk = cache_k.reshape(n_pool, sub, CMP_STRIDE * NSA_KV * HEAD_DIM)
    vv = cache_v.reshape(n_pool, sub, CMP_STRIDE * NSA_KV * HEAD_DIM)
    pt = page_table.reshape(-1)

    def src(u):
        return pl.BlockSpec((None, sub, CMP_STRIDE * NSA_KV * HEAD_DIM),
                            lambda b, c, pt_ref: (pt_ref[b * n_pages + c * P + u], 0, 0))

    wspec = pl.BlockSpec((CMP_BLOCK, HEAD_DIM, HEAD_DIM), lambda b, c, pt_ref: (0, 0, 0))
    out = jax.ShapeDtypeStruct((B, n_pages * sub, 4 * HEAD_DIM), F32)
    ospec = pl.BlockSpec((None, P * sub, 4 * HEAD_DIM), lambda b, c, pt_ref: (b, c, 0))
    return pl.pallas_call(
        functools.partial(_cmp_partial_paged_kernel, n_pages=P),
        out_shape=(out, out),
        grid_spec=pltpu.PrefetchScalarGridSpec(
            num_scalar_prefetch=1,
            grid=(B, n_pages // P),
            in_specs=[src(u) for u in range(P)] * 2 + [wspec, wspec],
            out_specs=(ospec, ospec),
        ),
        compiler_params=_cparams(("arbitrary", "arbitrary")),
        name="cmp_partial_paged",
    )(pt, *([vk] * P), *([vv] * P), w1k, w1v)


def _cmp_mlp_kernel(pk_ref, pv_ref, b1k_ref, b1v_ref, w2k_ref, w2v_ref, ok_ref, ov_ref, *, ns):
    for p_ref, b_ref, w_ref, o_ref in ((pk_ref, b1k_ref, w2k_ref, ok_ref), (pv_ref, b1v_ref, w2v_ref, ov_ref)):
        for kv in range(NSA_KV):
            p0 = p_ref[:, kv * HEAD_DIM:(kv + 1) * HEAD_DIM]
            p1 = p_ref[:, (NSA_KV + kv) * HEAD_DIM:(NSA_KV + kv + 1) * HEAD_DIM]
            nxt = pltpu.roll(p1, ns - 1, axis=0)
            h = _silu(p0 + nxt + b_ref[...])
            o_ref[:, kv * HEAD_DIM:(kv + 1) * HEAD_DIM] = jnp.dot(h.astype(BF16), w_ref[...],
                                                                  preferred_element_type=F32)


def _cmp_mlp(pk, pv, b1k, b1v, w2k, w2v):
    B, ns, _ = pk.shape
    pspec = pl.BlockSpec((None, ns, 4 * HEAD_DIM), lambda b: (b, 0, 0))
    bspec = pl.BlockSpec((1, HEAD_DIM), lambda b: (0, 0))
    wspec = pl.BlockSpec((HEAD_DIM, HEAD_DIM), lambda b: (0, 0))
    out = jax.ShapeDtypeStruct((B, ns, NSA_KV * HEAD_DIM), F32)
    ospec = pl.BlockSpec((None, ns, NSA_KV * HEAD_DIM), lambda b: (b, 0, 0))
    return pl.pallas_call(
        functools.partial(_cmp_mlp_kernel, ns=ns),
        out_shape=(out, out),
        grid=(B,),
        in_specs=[pspec, pspec, bspec, bspec, wspec, wspec],
        out_specs=(ospec, ospec),
        compiler_params=_cparams(("arbitrary",)),
        name="cmp_mlp",
    )(pk, pv, b1k.reshape(1, -1), b1v.reshape(1, -1), w2k.astype(BF16), w2v.astype(BF16))


def _cmp_attn_kernel(q_ref, kc_ref, vc_ref, o_ref, sel_ref, *, tq, ns, nb, q_base):
    qi = pl.program_id(1)
    q_pos = q_base + qi * tq + lax.broadcasted_iota(jnp.int32, (tq, 1), 0)
    n_idx = lax.broadcasted_iota(jnp.int32, (1, ns), 1)
    cmp_end = n_idx * CMP_STRIDE + (CMP_BLOCK - 1)
    mask = (cmp_end <= q_pos) & (n_idx < ns - 1)
    maskf = mask.astype(F32)
    per = SLC_BLOCK // CMP_STRIDE
    ci = lax.broadcasted_iota(jnp.int32, (ns, nb), 0)
    cb = lax.broadcasted_iota(jnp.int32, (ns, nb), 1)
    overlap = ((ci >= per * cb - 1) & (ci <= per * cb + per - 1)).astype(F32)
    blk = lax.broadcasted_iota(jnp.int32, (tq, nb), 1)
    blkf = blk.astype(F32)
    cur = q_pos >> SLC_SHIFT
    forced = (blk == 0) | (blk == cur) | (blk == cur - 1)
    valid = blk <= cur
    n_pick = SLC_TOPN - jnp.where(cur >= nb, 1, 0)
    for g in range(NSA_KV):
        kb = kc_ref[:, g * HEAD_DIM:(g + 1) * HEAD_DIM].astype(BF16)
        vb = vc_ref[:, g * HEAD_DIM:(g + 1) * HEAD_DIM].astype(BF16)
        imp = jnp.zeros((tq, ns), F32)
        for h in range(GROUP):
            c = (g * GROUP + h) * HEAD_DIM
            qh = (q_ref[:, c:c + HEAD_DIM] * SCALE).astype(BF16)
            s = lax.dot_general(qh, kb, (((1,), (1,)), ((), ())), preferred_element_type=F32)
            s = jnp.where(mask, s, NEG_BIG)
            m = jnp.max(s, axis=-1, keepdims=True)
            e = jnp.exp(s - m) * maskf
            p = e / jnp.maximum(jnp.sum(e, axis=-1, keepdims=True), 1e-30)
            o_ref[:, c:c + HEAD_DIM] = jnp.dot(p.astype(BF16), vb, preferred_element_type=F32)
            imp = imp + p
        imp_slc = jnp.dot(imp, overlap, precision=lax.Precision.HIGHEST, preferred_element_type=F32)
        score = jnp.where(valid, imp_slc + FORCE_BONUS * forced.astype(F32), NEG_BIG)

        def pick(it, carry):
            score, chosen = carry
            best = jnp.max(score, axis=-1, keepdims=True)
            first = jnp.min(jnp.where(score == best, blkf, float(nb)), axis=-1, keepdims=True)
            hit = blkf == first
            chosen = jnp.where(hit & (it < n_pick), 1.0, chosen)
            return jnp.where(hit, -jnp.inf, score), chosen

        _, chosen = lax.fori_loop(0, SLC_TOPN, pick, (score, jnp.zeros((tq, nb), F32)))
        bias = jnp.where((chosen > 0.5) & valid, 0.0, SEL_OFF).astype(BF16)
        for half in range(nb // LANES):
            sel_ref[g, half] = bias[:, half * LANES:(half + 1) * LANES]


def _cmp_attn(q_arr, q_col, B, n_qblk, tq, kc, vc, q_base, nb):
    ns = kc.shape[1]
    kspec = pl.BlockSpec((None, ns, NSA_KV * HEAD_DIM), lambda b, i: (b, 0, 0))
    return pl.pallas_call(
        functools.partial(_cmp_attn_kernel, tq=tq, ns=ns, nb=nb, q_base=q_base),
        out_shape=(jax.ShapeDtypeStruct((B * n_qblk * tq, NSA_HEADS * HEAD_DIM), F32),
                   jax.ShapeDtypeStruct((B, NSA_KV, nb // LANES, n_qblk * tq, LANES), BF16)),
        grid=(B, n_qblk),
        in_specs=[pl.BlockSpec((tq, NSA_HEADS * HEAD_DIM), lambda b, i: (b * n_qblk + i, q_col)), kspec, kspec],
        out_specs=(pl.BlockSpec((tq, NSA_HEADS * HEAD_DIM), lambda b, i: (b * n_qblk + i, 0)),
                   pl.BlockSpec((None, NSA_KV, nb // LANES, tq, LANES), lambda b, i: (b, 0, 0, i, 0))),
        compiler_params=_cparams(("arbitrary", "arbitrary")),
        name="cmp_attn_select",
    )(q_arr, kc, vc)


def _mix_out_kernel(x_ref, of_ref, oc_ref, os_ref, ow_ref, sm_ref, fz_ref, nz_ref, w_ref, h_ref):
    mix_f = of_ref[...] * _silu(fz_ref[...])
    sm = sm_ref[...]
    parts = []
    for h in range(NSA_HEADS):
        sl = slice(h * HEAD_DIM, (h + 1) * HEAD_DIM)
        c = FOX_HEADS + h * N_BRANCH
        parts.append(sm[:, c:c + 1] * oc_ref[:, sl] + sm[:, c + 1:c + 2] * os_ref[:, sl]
                     + sm[:, c + 2:c + 3] * ow_ref[:, sl])
    mix_n = jnp.concatenate(parts, axis=1) * _silu(nz_ref[...])
    mix = jnp.concatenate([mix_f, mix_n], axis=1).astype(BF16)
    h_ref[...] = x_ref[...] + jnp.dot(mix, w_ref[...], preferred_element_type=F32)


def _mix_out(x, o_f, o_c, o_s, o_w, big, w_out, tm):
    R, D = x.shape
    row = lambda i: (i, 0)
    wide = pl.BlockSpec((tm, 1024), row)
    return pl.pallas_call(
        _mix_out_kernel,
        out_shape=jax.ShapeDtypeStruct((R, D), F32),
        grid=(R // tm,),
        in_specs=[pl.BlockSpec((tm, D), row), wide, wide, wide, wide,
                  pl.BlockSpec((tm, LANES), lambda i: (i, C_SM * 2)),
                  pl.BlockSpec((tm, 1024), lambda i: (i, C_FZ // 4)),
                  pl.BlockSpec((tm, 1024), lambda i: (i, C_NZ // 4)),
                  pl.BlockSpec(w_out.shape, lambda i: (0, 0))],
        out_specs=pl.BlockSpec((tm, D), row),
        compiler_params=_cparams(("arbitrary",)),
        name="mix_out",
    )(x, o_f, o_c, o_s, o_w, big, big, big, w_out)


def _ple_kernel(h_ref, p_ref, g_ref, wg_ref, wp_ref, y_ref):
    h = h_ref[...]
    ms = jnp.mean(h * h, axis=-1, keepdims=True)
    hn = (h * lax.rsqrt(ms + EPS) * g_ref[...]).astype(BF16)
    gate = _sigmoid(jnp.dot(hn, wg_ref[...], preferred_element_type=F32))
    y_ref[...] = h + gate * jnp.dot(p_ref[...].astype(BF16), wp_ref[...], preferred_element_type=F32)


def _ple(h, p, ple_norm, w_gate, w_ple, tm):
    R, D = h.shape
    row = lambda i: (i, 0)
    return pl.pallas_call(
        _ple_kernel,
        out_shape=jax.ShapeDtypeStruct((R, D), F32),
        grid=(R // tm,),
        in_specs=[pl.BlockSpec((tm, D), row), pl.BlockSpec((tm, p.shape[1]), row),
                  pl.BlockSpec((1, D), lambda i: (0, 0)),
                  pl.BlockSpec(w_gate.shape, lambda i: (0, 0)),
                  pl.BlockSpec(w_ple.shape, lambda i: (0, 0))],
        out_specs=pl.BlockSpec((tm, D), row),
        compiler_params=_cparams(("arbitrary",)),
        name="ple_gate",
    )(h, p, ple_norm.reshape(1, D), w_gate, w_ple)


def _pick(n, cands):
    for c in cands:
        if n % c == 0:
            return c
    raise ValueError(f"no tile in {cands} divides {n}")


def _col(big, c, width=CH):
    return big[:, c * CH:c * CH + width]


def _prompt_layer(x, p_i, prm):
    T, D = x.shape
    big = _inproj(x, jnp.arange(T, dtype=jnp.int32), prm["attn_norm"], prm["w_perm"], prm["gain"], prm["bf_pad"],
                  _pick(T, (512, 256, 128)))
    logf = big[:, C_SM * CH:C_SM * CH + FOX_HEADS]
    nr = T // LANES
    c4, _ = _cumsum(logf.reshape(1, nr, LANES, FOX_HEADS).transpose(0, 1, 3, 2),
                    jnp.zeros((1, FOX_HEADS, DEC_PAD), F32))
    c_rows = c4.transpose(0, 1, 3, 2).reshape(T, FOX_HEADS)
    c_keys = c4.transpose(0, 2, 1, 3).reshape(1, FOX_KV, GROUP, T)

    tq = _pick(T, (512, 256, 128))
    nqb = T // tq
    col128 = lambda c: (lambda b, g, s, qi, kj, *_: (kj[s], c * 2 + g))
    kv = lambda c: [pl.BlockSpec((tq, HEAD_DIM), col128(c))]
    tabs = _causal_tables(nqb)
    o_f = _flash(big, C_FQ // 2, tq, nqb, 1, tabs, kv(C_FK), kv(C_FV), [big, big], tq,
                 bias=(c_rows, c_keys), name="fox_prompt")

    pk, pv = _cmp_partial_prompt(big, T, prm["w1k"], prm["w1v"], _pick(T // CMP_STRIDE, (256, 128, 64)))
    kc, vc = _cmp_mlp(pk[None], pv[None], prm["b1k"], prm["b1v"], prm["w2k"], prm["w2v"])
    nb = -(-(T // SLC_BLOCK) // LANES) * LANES
    tqc = 128
    o_c, sel = _cmp_attn(big, C_NQ // 4, 1, T // tqc, tqc, kc, vc, 0, nb)
    o_s = _flash(big, C_NQ // 2, tq, nqb, 1, tabs, kv(C_SK), kv(C_SV), [big, big], tq, sel=sel,
                 name="slc_prompt")
    o_w = _flash(big, C_NQ // 2, tq, nqb, 1, _causal_tables(nqb, lookback=-(-WINDOW // tq)), kv(C_WK), kv(C_WV),
                 [big, big], tq, windowed=True, name="win_prompt")

    tm = _pick(T, (256, 128))
    h = _mix_out(x, o_f, o_c, o_s, o_w, big, prm["w_out"], tm)
    y = _ple(h, p_i, prm["ple_norm"], prm["w_gate"], prm["w_ple"], tm)
    n_win = min(WINDOW, T)
    kv5 = lambda c: _col(big, c).reshape(1, 1, T, 2, HEAD_DIM)
    state = (kv5(C_FK), kv5(C_FV), logf.reshape(1, 1, T, FOX_HEADS), kv5(C_CK), kv5(C_CV), kv5(C_SK), kv5(C_SV),
             kv5(C_WK)[:, :, T - n_win:], kv5(C_WV)[:, :, T - n_win:])
    return y.reshape(1, T, D), state


def _sample_layer(x, p_i, caches, page_table, prm):
    c_fk, c_fv, c_flogf, c_ck, c_cv, c_sk, c_sv, c_wk, c_wv = caches
    B, Tn, D = x.shape
    n_pages = page_table.shape[1]
    past = n_pages * PAGE
    R = B * DEC_PAD
    xp = jnp.pad(x, ((0, 0), (0, DEC_PAD - Tn), (0, 0))).reshape(R, D)
    pos = jnp.tile(past + jnp.arange(DEC_PAD, dtype=jnp.int32), B)
    big = _inproj(xp, pos, prm["attn_norm"], prm["w_perm"], prm["gain"], prm["bf_pad"], _pick(R, (256, 128, 16)))
    logf = big[:, C_SM * CH:C_SM * CH + FOX_HEADS]

    P = _pick(n_pages, (8, 4, 2, 1))
    n_pool = c_fk.shape[0]
    lf_pages = _gather_logf_pages(c_flogf.transpose(0, 2, 1), page_table, _pick(n_pages, (16, 8, 4, 2, 1)))
    c4, c_new = _cumsum(lf_pages, logf.reshape(B, DEC_PAD, FOX_HEADS).transpose(0, 2, 1))
    c_keys = c4.transpose(0, 2, 1, 3).reshape(B, FOX_KV, GROUP, past)
    c_tail = c_new.reshape(B, FOX_KV, GROUP, DEC_PAD)
    c_rows = c_new.transpose(0, 2, 1).reshape(R, FOX_HEADS)

    pt = page_table.reshape(-1)
    n_keys = P * PAGE
    n_steps = n_pages // P
    tabs = _linear_tables(n_steps)

    def page_specs():
        mk = lambda u: pl.BlockSpec(
            (None, PAGE, HEAD_DIM), lambda b, g, s, qi, kj, fi, la, pt_ref: (pt_ref[b * n_pages + s * P + u], 0, g))
        return [mk(u) for u in range(P)]

    flat = lambda c: c.reshape(n_pool, PAGE, NSA_KV * HEAD_DIM)
    o_f = _flash(big, C_FQ // 2, DEC_PAD, 1, B, tabs, page_specs(), page_specs(),
                 [flat(c_fk)] * P + [flat(c_fv)] * P, n_keys, prefetch=(pt,),
                 bias=(c_rows, c_keys, c_tail), tail=(big, C_FK * 2, big, C_FV * 2),
                 q_base=past, tail_base=past, name="fox_decode")

    pk, pv = _cmp_partial_paged(c_ck, c_cv, page_table, prm["w1k"], prm["w1v"], P)
    kc, vc = _cmp_mlp(pk, pv, prm["b1k"], prm["b1v"], prm["w2k"], prm["w2v"])
    nb = -(-(past // SLC_BLOCK) // LANES) * LANES
    o_c, sel = _cmp_attn(big, C_NQ // 4, B, 1, DEC_PAD, kc, vc, past, nb)
    o_s = _flash(big, C_NQ // 2, DEC_PAD, 1, B, tabs, page_specs(), page_specs(),
                 [flat(c_sk)] * P + [flat(c_sv)] * P, n_keys, prefetch=(pt,),
                 sel=sel, tail=(big, C_SK * 2, big, C_SV * 2),
                 q_base=past, tail_base=past, name="slc_decode")
    n_buf = c_wk.shape[1]
    wspec = [pl.BlockSpec((n_buf, HEAD_DIM), lambda b, g, s, *_: (b, g))]
    wflat = lambda c: c.reshape(B * n_buf, NSA_KV * HEAD_DIM)
    o_w = _flash(big, C_NQ // 2, DEC_PAD, 1, B, _linear_tables(1), wspec, wspec, [wflat(c_wk), wflat(c_wv)], n_buf,
                 tail=(big, C_WK * 2, big, C_WV * 2), windowed=True,
                 q_base=past, k_base=past - n_buf, tail_base=past, name="win_decode")

    tm = _pick(R, (256, 128, 16))
    pp = jnp.pad(p_i, ((0, 0), (0, DEC_PAD - Tn), (0, 0))).reshape(R, -1)
    h = _mix_out(xp, o_f, o_c, o_s, o_w, big, prm["w_out"], tm)
    y = _ple(h, pp, prm["ple_norm"], prm["w_gate"], prm["w_ple"], tm)
    y = y.reshape(B, DEC_PAD, D)[:, :Tn]
    new = lambda c: _col(big, c).reshape(B, DEC_PAD, 2, HEAD_DIM)[:, :Tn]
    wk_new, wv_new = new(C_WK), new(C_WV)
    kw = jnp.concatenate([c_wk, wk_new], axis=1)[:, -n_buf:]
    vw = jnp.concatenate([c_wv, wv_new], axis=1)[:, -n_buf:]
    state = (new(C_FK), new(C_FV), logf.reshape(B, DEC_PAD, FOX_HEADS)[:, :Tn], new(C_CK), new(C_CV), new(C_SK),
             new(C_SV), kw, vw)
    return y, tuple(s[None] for s in state)


def kernel(x_prompt, x_sample, cache_fox_k, cache_fox_v, cache_fox_logf, cache_cmp_k, cache_cmp_v, cache_slc_k,
           cache_slc_v, cache_win_k, cache_win_v, page_table, p_prompt, p_sample, attn_norm, w_in, b_forget,
           fox_q_norm, fox_k_norm, nsa_q_norm, nsa_k_norm, cmp_k_w1, cmp_k_b1, cmp_k_w2, cmp_v_w1, cmp_v_b1,
           cmp_v_w2, w_out, ple_norm, w_ple, w_ple_gate):
    assert x_prompt.shape[0] == 1 and w_in.shape[0] == 1, "one prompt sequence, one layer"
    w_perm, gain, bf_pad = _prep_inproj_params(w_in[0], b_forget[0], fox_q_norm[0], fox_k_norm[0], nsa_q_norm[0],
                                               nsa_k_norm[0])
    prm = dict(attn_norm=attn_norm[0], w_perm=w_perm, gain=gain, bf_pad=bf_pad,
               w1k=cmp_k_w1[0].astype(BF16), w1v=cmp_v_w1[0].astype(BF16), b1k=cmp_k_b1[0], b1v=cmp_v_b1[0],
               w2k=cmp_k_w2[0], w2v=cmp_v_w2[0], w_out=w_out[0].astype(BF16), ple_norm=ple_norm[0],
               w_gate=w_ple_gate[0].astype(BF16), w_ple=w_ple[0].astype(BF16))
    y_p, st_p = _prompt_layer(x_prompt[0], p_prompt[0, 0], prm)
    caches = (cache_fox_k[0], cache_fox_v[0], cache_fox_logf[0], cache_cmp_k[0], cache_cmp_v[0], cache_slc_k[0],
              cache_slc_v[0], cache_win_k[0], cache_win_v[0])
    y_s, st_s = _sample_layer(x_sample, p_sample[0], caches, page_table, prm)
    return (y_p, y_s) + tuple(st_p) + tuple(st_s)
```

```python
import functools

import numpy as np
import jax
import jax.numpy as jnp
from jax import lax
from jax.experimental import pallas as pl
from jax.experimental.pallas import tpu as pltpu

F32 = jnp.float32
BF16 = jnp.bfloat16

HEAD_DIM = 128
FOX_HEADS = 8
FOX_KV = 2
NSA_HEADS = 8
NSA_KV = 2
GROUP = 4
N_BRANCH = 3
ROPE_THETA = 500000.0
ROT_DIM = HEAD_DIM // 4
ROT_HALF = ROT_DIM // 2
CMP_STRIDE = 16
CMP_BLOCK = 2 * CMP_STRIDE
SLC_BLOCK = 64
SLC_SHIFT = 6
SLC_TOPN = 16
WINDOW = 512
EPS = 1e-6
NEG_BIG = -1e30
FORCE_BONUS = 1e4
SEL_OFF = -30000.0
PAGE = 128
SCALE = HEAD_DIM ** -0.5

SPLIT_SIZES = (1024, 256, 256, 8, 1024, 1024, 256, 256, 256, 256, 256, 256, 24, 1024)

CH = 256
C_FQ, C_NQ, C_FZ, C_NZ = 0, 4, 8, 12
C_FK, C_CK, C_SK, C_WK, C_FV, C_CV, C_SV, C_WV, C_SM = 16, 17, 18, 19, 20, 21, 22, 23, 24
N_CH = 25
NP = N_CH * CH
LANES = 128
DEC_PAD = 16

VMEM_LIMIT = 56 * 1024 * 1024


def _cparams(sem):
    return pltpu.CompilerParams(dimension_semantics=sem, vmem_limit_bytes=VMEM_LIMIT)


def _sigmoid(x):
    return 1.0 / (1.0 + jnp.exp(-x))


def _silu(x):
    return x * _sigmoid(x)


def _inproj_kernel(x_ref, g_ref, w_ref, gain_ref, bf_ref, rc_ref, rs1_ref, rs2_ref, o_ref, xn_ref):
    j = pl.program_id(1)

    @pl.when(j == 0)
    def _():
        x = x_ref[...]
        ms = jnp.mean(x * x, axis=-1, keepdims=True)
        xn_ref[...] = (x * lax.rsqrt(ms + EPS) * g_ref[...]).astype(BF16)

    y = jnp.dot(xn_ref[...], w_ref[...], preferred_element_type=F32)

    is_q_rope = (j >= C_NQ) & (j < C_FZ)
    is_k_rope = (j >= C_CK) & (j <= C_WK)
    is_norm_only = (j < C_NQ) | (j == C_FK)
    is_rope = is_q_rope | is_k_rope
    is_raw = ((j >= C_FZ) & (j < C_FK)) | ((j >= C_FV) & (j < C_SM))

    def normed(h):
        yh = y[:, h * LANES:(h + 1) * LANES]
        ms = jnp.mean(yh * yh, axis=-1, keepdims=True)
        return yh * lax.rsqrt(ms + EPS) * gain_ref[:, h * LANES:(h + 1) * LANES]

    @pl.when(is_norm_only)
    def _():
        for h in range(CH // LANES):
            o_ref[:, h * LANES:(h + 1) * LANES] = normed(h)

    @pl.when(is_rope)
    def _():
        for h in range(CH // LANES):
            yn = normed(h)
            lo = pltpu.roll(yn, LANES - ROT_HALF, axis=1)
            hi = pltpu.roll(yn, ROT_HALF, axis=1)
            o_ref[:, h * LANES:(h + 1) * LANES] = yn * rc_ref[...] + lo * rs1_ref[...] + hi * rs2_ref[...]

    @pl.when(is_raw)
    def _():
        o_ref[...] = y

    @pl.when(j == C_SM)
    def _():
        t = y[:, :LANES] + bf_ref[...]
        lane = lax.broadcasted_iota(jnp.int32, t.shape, 1)
        e = jnp.exp(-jnp.abs(t))
        logsig = jnp.minimum(t, 0.0) - jnp.log(1.0 + e)
        o_ref[:, :LANES] = jnp.where(lane < FOX_HEADS, logsig, _sigmoid(t))
        o_ref[:, LANES:] = jnp.zeros_like(t)


def _inproj(x, pos, attn_norm, w_perm, gain, bf_pad, tm):
    R, D = x.shape
    inv = ROPE_THETA ** (-(2.0 / ROT_DIM) * jnp.arange(ROT_HALF, dtype=F32))
    ang = pos.astype(F32)[:, None] * inv[None, :]
    cos, sin = jnp.cos(ang), jnp.sin(ang)
    z = jnp.zeros((R, LANES - ROT_DIM), F32)
    zh = jnp.zeros((R, ROT_HALF), F32)
    rc = jnp.concatenate([cos, cos, z + 1.0], axis=1)
    rs1 = jnp.concatenate([-sin, zh, z], axis=1)
    rs2 = jnp.concatenate([zh, sin, z], axis=1)
    row = lambda i, j: (i, 0)
    return pl.pallas_call(
        _inproj_kernel,
        out_shape=jax.ShapeDtypeStruct((R, NP), F32),
        grid=(R // tm, N_CH),
        in_specs=[
            pl.BlockSpec((tm, D), row),
            pl.BlockSpec((1, D), lambda i, j: (0, 0)),
            pl.BlockSpec((D, CH), lambda i, j: (0, j)),
            pl.BlockSpec((1, CH), lambda i, j: (0, j)),
            pl.BlockSpec((1, LANES), lambda i, j: (0, 0)),
            pl.BlockSpec((tm, LANES), row),
            pl.BlockSpec((tm, LANES), row),
            pl.BlockSpec((tm, LANES), row),
        ],
        out_specs=pl.BlockSpec((tm, CH), lambda i, j: (i, j)),
        scratch_shapes=[pltpu.VMEM((tm, D), BF16)],
        compiler_params=_cparams(("arbitrary", "arbitrary")),
        name="inproj",
    )(x, attn_norm.reshape(1, D), w_perm, gain, bf_pad, rc, rs1, rs2)


def _prep_inproj_params(w_in, b_forget, fox_q_norm, fox_k_norm, nsa_q_norm, nsa_k_norm):
    D = w_in.shape[0]
    offs = np.concatenate([[0], np.cumsum(SPLIT_SIZES)]).tolist()
    seg = [w_in[:, offs[i]:offs[i + 1]] for i in range(len(SPLIT_SIZES))]
    fq, fk, fv, flg, fz, nq, ck, cv, sk, sv, wk, wv, ng, nz = seg
    small = jnp.concatenate([flg, ng, jnp.zeros((D, CH - FOX_HEADS - NSA_HEADS * N_BRANCH), F32)], axis=1)
    w_perm = jnp.concatenate([fq, nq, fz, nz, fk, ck, sk, wk, fv, cv, sv, wv, small], axis=1).astype(BF16)
    zeros = lambda n: jnp.zeros((n,), F32)
    gain = jnp.concatenate([
        jnp.tile(fox_q_norm, FOX_HEADS), jnp.tile(nsa_q_norm, NSA_HEADS), zeros(2048),
        jnp.tile(fox_k_norm, FOX_KV), jnp.tile(nsa_k_norm[0], NSA_KV), jnp.tile(nsa_k_norm[1], NSA_KV),
        jnp.tile(nsa_k_norm[2], NSA_KV), zeros(4 * CH + CH)]).reshape(1, NP)
    bf_pad = jnp.concatenate([b_forget, zeros(LANES - FOX_HEADS)]).reshape(1, LANES)
    return w_perm, gain, bf_pad


def _gather_pages_kernel(pt_ref, *refs):
    n = len(refs) - 1
    o_ref = refs[n]
    for u in range(n):
        o_ref[u] = refs[u][...]


def _gather_logf_pages(cache_t, page_table, pages_per_step):
    B, n_pages = page_table.shape
    P = pages_per_step
    pt = page_table.reshape(-1)

    def src_map(u):
        return lambda b, c, pt_ref: (pt_ref[b * n_pages + c * P + u], 0, 0)

    return pl.pallas_call(
        _gather_pages_kernel,
        out_shape=jax.ShapeDtypeStruct((B, n_pages, FOX_HEADS, PAGE), F32),
        grid_spec=pltpu.PrefetchScalarGridSpec(
            num_scalar_prefetch=1,
            grid=(B, n_pages // P),
            in_specs=[pl.BlockSpec((None, FOX_HEADS, PAGE), src_map(u)) for u in range(P)],
            out_specs=pl.BlockSpec((None, P, FOX_HEADS, PAGE), lambda b, c, pt_ref: (b, c, 0, 0)),
        ),
        compiler_params=_cparams(("arbitrary", "arbitrary")),
        name="gather_logf_pages",
    )(pt, *([cache_t] * P))


def _cumsum_kernel(x_ref, new_ref, o_ref, onew_ref, *, nr):
    hp = lax.Precision.HIGHEST
    x = x_ref[...].reshape(nr * FOX_HEADS, LANES)
    ci = lax.broadcasted_iota(jnp.int32, (LANES, LANES), 0)
    cj = lax.broadcasted_iota(jnp.int32, (LANES, LANES), 1)
    upper = (ci <= cj).astype(F32)
    lastcol = (ci == LANES - 1).astype(F32)
    within = jnp.dot(x, upper, precision=hp, preferred_element_type=F32)
    rowtot = jnp.dot(within, lastcol, precision=hp, preferred_element_type=F32)
    n = nr * FOX_HEADS
    ri = lax.broadcasted_iota(jnp.int32, (n, n), 0)
    rj = lax.broadcasted_iota(jnp.int32, (n, n), 1)
    same_head = (ri & (FOX_HEADS - 1)) == (rj & (FOX_HEADS - 1))
    before = (same_head & (rj < ri)).astype(F32)
    prefix = jnp.dot(before, rowtot, precision=hp, preferred_element_type=F32)
    o_ref[...] = (within + prefix).reshape(nr, FOX_HEADS, LANES)
    total = (prefix + rowtot)[n - FOX_HEADS:, :]
    nw = new_ref[...]
    ti = lax.broadcasted_iota(jnp.int32, (DEC_PAD, DEC_PAD), 0)
    tj = lax.broadcasted_iota(jnp.int32, (DEC_PAD, DEC_PAD), 1)
    onew_ref[...] = total[:, :DEC_PAD] + jnp.dot(nw, (ti <= tj).astype(F32), precision=hp,
                                                  preferred_element_type=F32)


def _cumsum(x4, new):
    B, nr = x4.shape[:2]
    return pl.pallas_call(
        functools.partial(_cumsum_kernel, nr=nr),
        out_shape=(jax.ShapeDtypeStruct(x4.shape, F32), jax.ShapeDtypeStruct(new.shape, F32)),
        grid=(B,),
        in_specs=[pl.BlockSpec((None, nr, FOX_HEADS, LANES), lambda b: (b, 0, 0, 0)),
                  pl.BlockSpec((None, FOX_HEADS, DEC_PAD), lambda b: (b, 0, 0))],
        out_specs=(pl.BlockSpec((None, nr, FOX_HEADS, LANES), lambda b: (b, 0, 0, 0)),
                   pl.BlockSpec((None, FOX_HEADS, DEC_PAD), lambda b: (b, 0, 0))),
        compiler_params=_cparams(("arbitrary",)),
        name="cumsum_logf",
    )(x4, new)


def _flash_kernel(*refs, n_sp, n_src, tq, n_keys, has_bias, has_sel, has_tail, windowed,
                  q_base, k_base, tail_base):
    qi_tab, kj_tab, first_tab, last_tab = refs[:4]
    refs = refs[n_sp:]
    q_ref = refs[0]
    k_srcs = refs[1:1 + n_src]
    v_srcs = refs[1 + n_src:1 + 2 * n_src]
    pos = 1 + 2 * n_src
    cq_ref = cs_ref = sel_ref = kt_ref = vt_ref = cst_ref = None
    if has_bias:
        cq_ref, cs_ref = refs[pos], refs[pos + 1]
        pos += 2
    if has_sel:
        sel_ref = refs[pos]
        pos += 1
    if has_tail:
        kt_ref, vt_ref = refs[pos], refs[pos + 1]
        pos += 2
        if has_bias:
            cst_ref = refs[pos]
            pos += 1
    o_ref = refs[pos]
    qs_ref, m_ref, l_ref, acc_ref = refs[pos + 1:pos + 5]

    step = pl.program_id(2)
    g = pl.program_id(1)
    qi = qi_tab[step]
    kj = kj_tab[step]

    @pl.when(first_tab[step] == 1)
    def _():
        qs_ref[...] = (q_ref[...] * SCALE).astype(BF16)
        m_ref[...] = jnp.full(m_ref.shape, NEG_BIG, F32)
        l_ref[...] = jnp.zeros(l_ref.shape, F32)
        acc_ref[...] = jnp.zeros(acc_ref.shape, F32)

    q_pos = q_base + qi * tq + lax.broadcasted_iota(jnp.int32, (tq, 1), 0)

    def attend(k, v, k_pos, cs, sel_bias):
        kb = k.astype(BF16)
        vb = v.astype(BF16)
        mask = k_pos <= q_pos
        if windowed:
            mask = mask & ((q_pos - k_pos) < WINDOW)
        for h in range(GROUP):
            s = lax.dot_general(qs_ref[:, h * HEAD_DIM:(h + 1) * HEAD_DIM], kb, (((1,), (1,)), ((), ())),
                                preferred_element_type=F32)
            if cs is not None:
                hh = g * GROUP + h
                cq = jnp.sum(jnp.where(lax.broadcasted_iota(jnp.int32, cq_ref.shape, 1) == hh, cq_ref[...], 0.0),
                             axis=1, keepdims=True)
                s = s + (cq - cs[h:h + 1, :])
            if sel_bias is not None:
                s = s + sel_bias
            s = jnp.where(mask, s, NEG_BIG)
            m_prev = m_ref[h]
            m_new = jnp.maximum(m_prev, jnp.max(s, axis=-1, keepdims=True))
            alpha = jnp.exp(m_prev - m_new)
            p = jnp.exp(s - m_new)
            l_ref[h] = alpha * l_ref[h] + jnp.sum(p, axis=-1, keepdims=True)
            acc_ref[h] = alpha * acc_ref[h] + jnp.dot(p.astype(BF16), vb, preferred_element_type=F32)
            m_ref[h] = m_new

    if n_src == 1:
        k, v = k_srcs[0][...], v_srcs[0][...]
    else:
        k = jnp.concatenate([r[...] for r in k_srcs], axis=0)
        v = jnp.concatenate([r[...] for r in v_srcs], axis=0)
    k_pos = k_base + kj * n_keys + lax.broadcasted_iota(jnp.int32, (1, n_keys), 1)
    sel_bias = None
    if has_sel:
        blk = ((kj * n_keys + lax.broadcasted_iota(jnp.int32, (1, n_keys), 1)) >> SLC_SHIFT) & (LANES - 1)
        onehot = (lax.broadcasted_iota(jnp.int32, (LANES, n_keys), 0) == blk).astype(BF16)
        sel_bias = jnp.dot(sel_ref[...], onehot, preferred_element_type=F32)
    attend(k, v, k_pos, cs_ref[...] if has_bias else None, sel_bias)

    @pl.when(last_tab[step] == 1)
    def _():
        if has_tail:
            t_pos = tail_base + lax.broadcasted_iota(jnp.int32, (1, DEC_PAD), 1)
            attend(kt_ref[...], vt_ref[...], t_pos, cst_ref[...] if has_bias else None, None)
        for h in range(GROUP):
            o_ref[:, h * HEAD_DIM:(h + 1) * HEAD_DIM] = acc_ref[h] / jnp.maximum(l_ref[h], 1e-30)


def _flash(q_arr, q_col, tq, n_qblk, B, tables, k_specs, v_specs, kv_args, n_keys, *, prefetch=(),
           bias=None, sel=None, tail=None, windowed=False, q_base=0, k_base=0, tail_base=0, name="flash"):
    n_sp = len(tables) + len(prefetch)
    n_steps = tables[0].shape[0]
    n_src = len(k_specs)
    qmap = lambda b, g, s, qi, *_: (b * n_qblk + qi[s], q_col + g)
    in_specs = [pl.BlockSpec((tq, GROUP * HEAD_DIM), qmap)] + list(k_specs) + list(v_specs)
    args = [q_arr] + list(kv_args)
    if bias is not None:
        in_specs.append(pl.BlockSpec((tq, FOX_HEADS), lambda b, g, s, qi, *_: (b * n_qblk + qi[s], 0)))
        in_specs.append(pl.BlockSpec((None, None, GROUP, n_keys), lambda b, g, s, qi, kj, *_: (b, g, 0, kj[s])))
        args += [bias[0], bias[1]]
    if sel is not None:
        blocks_per_step = n_keys // SLC_BLOCK
        in_specs.append(pl.BlockSpec(
            (None, None, None, tq, LANES),
            lambda b, g, s, qi, kj, *_: (b, g, (kj[s] * blocks_per_step) // LANES, qi[s], 0)))
        args.append(sel)
    if tail is not None:
        k_arr, k_col, v_arr, v_col = tail
        in_specs.append(pl.BlockSpec((DEC_PAD, HEAD_DIM), lambda b, g, s, *_: (b, k_col + g)))
        in_specs.append(pl.BlockSpec((DEC_PAD, HEAD_DIM), lambda b, g, s, *_: (b, v_col + g)))
        args += [k_arr, v_arr]
        if bias is not None:
            in_specs.append(pl.BlockSpec((None, None, GROUP, DEC_PAD), lambda b, g, s, *_: (b, g, 0, 0)))
            args.append(bias[2])
    kern = functools.partial(
        _flash_kernel, n_sp=n_sp, n_src=n_src, tq=tq, n_keys=n_keys, has_bias=bias is not None,
        has_sel=sel is not None, has_tail=tail is not None, windowed=windowed,
        q_base=q_base, k_base=k_base, tail_base=tail_base)
    return pl.pallas_call(
        kern,
        out_shape=jax.ShapeDtypeStruct((B * n_qblk * tq, 2 * GROUP * HEAD_DIM), F32),
        grid_spec=pltpu.PrefetchScalarGridSpec(
            num_scalar_prefetch=n_sp,
            grid=(B, 2, n_steps),
            in_specs=in_specs,
            out_specs=pl.BlockSpec((tq, GROUP * HEAD_DIM), lambda b, g, s, qi, *_: (b * n_qblk + qi[s], g)),
            scratch_shapes=[pltpu.VMEM((tq, GROUP * HEAD_DIM), BF16),
                            pltpu.VMEM((GROUP, tq, 1), F32),
                            pltpu.VMEM((GROUP, tq, 1), F32),
                            pltpu.VMEM((GROUP, tq, HEAD_DIM), F32)],
        ),
        compiler_params=_cparams(("arbitrary", "arbitrary", "arbitrary")),
        name=name,
    )(*tables, *prefetch, *args)


def _causal_tables(n_blk, lookback=None):
    qi, kj, first, last = [], [], [], []
    for i in range(n_blk):
        lo = 0 if lookback is None else max(0, i - lookback)
        for j in range(lo, i + 1):
            qi.append(i)
            kj.append(j)
            first.append(int(j == lo))
            last.append(int(j == i))
    return tuple(jnp.asarray(a, jnp.int32) for a in (qi, kj, first, last))


def _linear_tables(n_steps):
    z = np.zeros((n_steps,), np.int32)
    first, last = z.copy(), z.copy()
    first[0], last[-1] = 1, 1
    return tuple(jnp.asarray(a, jnp.int32) for a in (z, np.arange(n_steps, dtype=np.int32), first, last))


LOG2E = 1.4426950408889634
AUG = 128
N_SPLIT = 3
AUG_C0 = N_SPLIT


def _split_part(x, part):
    hi = x.astype(BF16)
    r1 = x - hi.astype(F32)
    mid = r1.astype(BF16)
    lo = (r1 - mid.astype(F32)).astype(BF16)
    return jnp.where(part == 0, hi, jnp.where(part == 1, mid, lo))


def _pack_kernel(big_q, big_kv, c_ref, qm_ref, fa_ref, kf_ref, vf_ref, ks_ref, vs_ref, kw_ref, vw_ref, *, tr):
    i = pl.program_id(0)
    lane = lax.broadcasted_iota(jnp.int32, (tr, AUG), 1)
    in_c = (lane >= AUG_C0) & (lane < AUG_C0 + GROUP * N_SPLIT)
    slot = jnp.zeros_like(lane)
    for j in range(1, GROUP):
        slot = slot + jnp.where(lane >= AUG_C0 + j * N_SPLIT, 1, 0)
    part = jnp.where(lane < AUG_C0, lane, lane - AUG_C0 - N_SPLIT * slot)
    c2 = c_ref[...] * LOG2E
    ones_lane = jnp.where(lane == 0, 1.0, 0.0).astype(BF16)
    zeros = jnp.zeros((tr, AUG), BF16)

    qm_ref[...] = (big_q[...] * (SCALE * LOG2E)).astype(BF16)
    for h in range(FOX_HEADS):
        cq = jnp.where(lane < AUG_C0, c2[:, h:h + 1], 0.0)
        own = in_c & (slot == h % GROUP)
        fa_ref[:, h * AUG:(h + 1) * AUG] = jnp.where(own, -1.0, _split_part(cq, part).astype(F32)).astype(BF16)

    tok = i * tr + lax.broadcasted_iota(jnp.int32, (tr, AUG), 0)
    blk_onehot = jnp.where(lane == ((tok >> SLC_SHIFT) & (AUG - 1)), 1.0, 0.0).astype(BF16)
    for g in range(NSA_KV):
        cs = jnp.zeros((tr, AUG), F32)
        for j in range(GROUP):
            cs = jnp.where(in_c & (slot == j), c2[:, g * GROUP + j:g * GROUP + j + 1], cs)
        k_aug = jnp.where(lane < AUG_C0, 1.0, _split_part(cs, part).astype(F32)).astype(BF16)
        col = lambda c: big_kv[:, (c - C_FK) * CH + g * HEAD_DIM:(c - C_FK) * CH + (g + 1) * HEAD_DIM].astype(BF16)
        lo, hi = g * 2 * HEAD_DIM, g * 2 * HEAD_DIM + HEAD_DIM
        for k_out, v_out, kc, vc, aug in ((kf_ref, vf_ref, C_FK, C_FV, k_aug), (ks_ref, vs_ref, C_SK, C_SV, blk_onehot),
                                          (kw_ref, vw_ref, C_WK, C_WV, zeros)):
            k_out[:, lo:hi] = col(kc)
            k_out[:, hi:hi + AUG] = aug
            v_out[:, lo:hi] = col(vc)
            v_out[:, hi:hi + AUG] = ones_lane


def _pack(big, c_rows, tr):
    T = big.shape[0]
    row = lambda i: (i, 0)
    kv = jax.ShapeDtypeStruct((T, NSA_KV * (HEAD_DIM + AUG)), BF16)
    kvspec = pl.BlockSpec((tr, NSA_KV * (HEAD_DIM + AUG)), row)
    return pl.pallas_call(
        functools.partial(_pack_kernel, tr=tr),
        out_shape=(jax.ShapeDtypeStruct((T, 2048), BF16), jax.ShapeDtypeStruct((T, FOX_HEADS * AUG), BF16)) + (kv,) * 6,
        grid=(T // tr,),
        in_specs=[pl.BlockSpec((tr, 2048), row),
                  pl.BlockSpec((tr, 2048), lambda i: (i, C_FK // 8)),
                  pl.BlockSpec((tr, FOX_HEADS), row)],
        out_specs=(pl.BlockSpec((tr, 2048), row), pl.BlockSpec((tr, FOX_HEADS * AUG), row)) + (kvspec,) * 6,
        compiler_params=_cparams(("arbitrary",)),
        name="pack_qkv",
    )(big, big, c_rows)


def _pflash_kernel(qi_tab, kj_tab, first_tab, last_tab, kind_tab, aug_tab, *refs, tq, aug_mode):
    qm_ref, k_ref, v_ref = refs[:3]
    pos = 3
    a_ref = None
    if aug_mode != "none":
        a_ref = refs[pos]
        pos += 1
    o_ref, qs_ref, m_ref, acc_ref = refs[pos:pos + 4]
    step = pl.program_id(1)

    @pl.when(first_tab[step] == 1)
    def _():
        for h in range(GROUP):
            qs_ref[h, :, :HEAD_DIM] = qm_ref[:, h * HEAD_DIM:(h + 1) * HEAD_DIM]
            if aug_mode == "head":
                qs_ref[h, :, HEAD_DIM:] = a_ref[:, h * AUG:(h + 1) * AUG]
            elif aug_mode == "none":
                qs_ref[h, :, HEAD_DIM:] = jnp.zeros((tq, AUG), BF16)
        m_ref[...] = jnp.full(m_ref.shape, NEG_BIG, F32)
        acc_ref[...] = jnp.zeros(acc_ref.shape, F32)

    if aug_mode == "group":
        @pl.when(aug_tab[step] == 1)
        def _():
            for h in range(GROUP):
                qs_ref[h, :, HEAD_DIM:] = a_ref[...]

    def attend(kind):
        kb = k_ref[...]
        vb = v_ref[...]
        if kind:
            qr = lax.broadcasted_iota(jnp.int32, (tq, tq), 0)
            kr = lax.broadcasted_iota(jnp.int32, (tq, tq), 1)
            keep = (kr <= qr) if kind == 1 else (kr > qr)
        for h in range(GROUP):
            s = lax.dot_general(qs_ref[h], kb, (((1,), (1,)), ((), ())), preferred_element_type=F32)
            if kind:
                s = jnp.where(keep, s, NEG_BIG)
            m_prev = m_ref[h]
            m_tile = s[:, :LANES]
            for c in range(1, tq // LANES):
                m_tile = jnp.maximum(m_tile, s[:, c * LANES:(c + 1) * LANES])
            m_new = jnp.maximum(m_prev, jnp.max(m_tile, axis=-1, keepdims=True))
            p = jnp.exp2(s - jnp.tile(m_new, (1, tq // LANES))).astype(BF16)
            alpha = jnp.exp2(m_prev - m_new)
            acc_ref[h] = jnp.tile(alpha, (1, 2)) * acc_ref[h] + jnp.dot(p, vb, preferred_element_type=F32)
            m_ref[h] = m_new

    for kind in range(3):
        pl.when(kind_tab[step] == kind)(functools.partial(attend, kind))

    @pl.when(last_tab[step] == 1)
    def _():
        for h in range(GROUP):
            acc = acc_ref[h]
            o_ref[:, h * HEAD_DIM:(h + 1) * HEAD_DIM] = acc[:, :HEAD_DIM] / jnp.maximum(
                acc[:, HEAD_DIM:HEAD_DIM + 1], 1e-30)


def _pflash(qm, q_col, k_arr, v_arr, tables, tq, *, aug=None, aug_mode="none", name):
    T = qm.shape[0]
    n_steps = tables[0].shape[0]
    kvw = HEAD_DIM + AUG
    in_specs = [pl.BlockSpec((tq, GROUP * HEAD_DIM), lambda g, s, qi, *_: (qi[s], q_col * 2 + g)),
                pl.BlockSpec((tq, kvw), lambda g, s, qi, kj, *_: (kj[s], g)),
                pl.BlockSpec((tq, kvw), lambda g, s, qi, kj, *_: (kj[s], g))]
    args = [qm, k_arr, v_arr]
    if aug_mode == "head":
        in_specs.append(pl.BlockSpec((tq, GROUP * AUG), lambda g, s, qi, *_: (qi[s], g)))
        args.append(aug)
    elif aug_mode == "group":
        per_step = tq // SLC_BLOCK
        in_specs.append(pl.BlockSpec((None, None, tq, AUG),
                                     lambda g, s, qi, kj, *_: (g, (kj[s] * per_step) // AUG, qi[s], 0)))
        args.append(aug)
    return pl.pallas_call(
        functools.partial(_pflash_kernel, tq=tq, aug_mode=aug_mode),
        out_shape=jax.ShapeDtypeStruct((T, 2 * GROUP * HEAD_DIM), F32),
        grid_spec=pltpu.PrefetchScalarGridSpec(
            num_scalar_prefetch=6,
            grid=(2, n_steps),
            in_specs=in_specs,
            out_specs=pl.BlockSpec((tq, GROUP * HEAD_DIM), lambda g, s, qi, *_: (qi[s], g)),
            scratch_shapes=[pltpu.VMEM((GROUP, tq, kvw), BF16),
                            pltpu.VMEM((GROUP, tq, LANES), F32),
                            pltpu.VMEM((GROUP, tq, kvw), F32)],
        ),
        compiler_params=_cparams(("arbitrary", "arbitrary")),
        name=name,
    )(*tables, *args)


def _prompt_tables(n_blk, tq, window=False):
    per_half = AUG * SLC_BLOCK // tq
    rows = []
    for i in range(n_blk):
        lo = max(0, i - 1) if window else 0
        for j in range(lo, i + 1):
            kind = 1 if j == i else (2 if window else 0)
            rows.append((i, j, int(j == lo), int(j == i), kind, int(j == lo or j % per_half == 0)))
    return tuple(jnp.asarray(a, jnp.int32) for a in zip(*rows))


def _cmp_cat(w1):
    return jnp.concatenate([w1[:CMP_STRIDE], w1[CMP_STRIDE:]], axis=-1)


def _cmp_partial_prompt_kernel(xk0_ref, xk1_ref, xv0_ref, xv1_ref, wk_ref, wv_ref, ok_ref, ov_ref, *, tn):
    for x_refs, w_ref, o_ref in (((xk0_ref, xk1_ref), wk_ref, ok_ref), ((xv0_ref, xv1_ref), wv_ref, ov_ref)):
        for kv in range(NSA_KV):
            acc = None
            for r in range(CMP_STRIDE):
                x = x_refs[kv][pl.ds(r, tn, stride=CMP_STRIDE), :]
                d = jnp.dot(x.astype(BF16), w_ref[r], preferred_element_type=F32)
                acc = d if acc is None else acc + d
            o_ref[:, kv * HEAD_DIM:(kv + 1) * HEAD_DIM] = acc[:, :HEAD_DIM]
            o_ref[:, (NSA_KV + kv) * HEAD_DIM:(NSA_KV + kv + 1) * HEAD_DIM] = acc[:, HEAD_DIM:]


def _cmp_partial_prompt(big, T, w1k, w1v, tn):
    ns = T // CMP_STRIDE
    wspec = pl.BlockSpec((CMP_STRIDE, HEAD_DIM, 2 * HEAD_DIM), lambda i: (0, 0, 0))
    out = jax.ShapeDtypeStruct((ns, 4 * HEAD_DIM), F32)
    return pl.pallas_call(
        functools.partial(_cmp_partial_prompt_kernel, tn=tn),
        out_shape=(out, out),
        grid=(ns // tn,),
        in_specs=[pl.BlockSpec((tn * CMP_STRIDE, HEAD_DIM), (lambda c: (lambda i: (i, c)))(c))
                  for c in (2 * C_CK, 2 * C_CK + 1, 2 * C_CV, 2 * C_CV + 1)] + [wspec, wspec],
        out_specs=(pl.BlockSpec((tn, 4 * HEAD_DIM), lambda i: (i, 0)),) * 2,
        compiler_params=_cparams(("arbitrary",)),
        name="cmp_partial_prompt",
    )(big, big, big, big, _cmp_cat(w1k), _cmp_cat(w1v))


def _cmp_partial_paged_kernel(pt_ref, *refs, n_pages):
    xk, xv = refs[:n_pages], refs[n_pages:2 * n_pages]
    wk_ref, wv_ref, ok_ref, ov_ref = refs[2 * n_pages:]
    sub = PAGE // CMP_STRIDE
    rows_per_sub = CMP_STRIDE * NSA_KV
    for xs, w_ref, o_ref in ((xk, wk_ref, ok_ref), (xv, wv_ref, ov_ref)):
        for kv in range(NSA_KV):
            acc = None
            for r in range(CMP_STRIDE):
                x = jnp.concatenate([p[pl.ds(NSA_KV * r + kv, sub, stride=rows_per_sub), :] for p in xs], axis=0)
                d = jnp.dot(x.astype(BF16), w_ref[r], preferred_element_type=F32)
                acc = d if acc is None else acc + d
            o_ref[:, kv * HEAD_DIM:(kv + 1) * HEAD_DIM] = acc[:, :HEAD_DIM]
            o_ref[:, (NSA_KV + kv) * HEAD_DIM:(NSA_KV + kv + 1) * HEAD_DIM] = acc[:, HEAD_DIM:]


def _cmp_partial_paged(cache_k, cache_v, page_table, w1k, w1v, pages_per_step):
    B, n_pages = page_table.shape
    P = pages_per_step
    sub = PAGE // CMP_STRIDE
    n_pool = cache_k.shape[0]
    vk = cache_k.reshape(n_pool * PAGE * NSA_KV, HEAD_DIM)
    vv = cache_v.reshape(n_pool * PAGE * NSA_KV, HEAD_DIM)
    pt = page_table.reshape(-1)

    def src(u):
        return pl.BlockSpec((PAGE * NSA_KV, HEAD_DIM), lambda b, c, pt_ref: (pt_ref[b * n_pages + c * P + u], 0))

    wspec = pl.BlockSpec((CMP_STRIDE, HEAD_DIM, 2 * HEAD_DIM), lambda b, c, pt_ref: (0, 0, 0))
    out = jax.ShapeDtypeStruct((B, n_pages * sub, 4 * HEAD_DIM), F32)
    ospec = pl.BlockSpec((None, P * sub, 4 * HEAD_DIM), lambda b, c, pt_ref: (b, c, 0))
    return pl.pallas_call(
        functools.partial(_cmp_partial_paged_kernel, n_pages=P),
        out_shape=(out, out),
        grid_spec=pltpu.PrefetchScalarGridSpec(
            num_scalar_prefetch=1,
            grid=(B, n_pages // P),
            in_specs=[src(u) for u in range(P)] * 2 + [wspec, wspec],
            out_specs=(ospec, ospec),
        ),
        compiler_params=_cparams(("arbitrary", "arbitrary")),
        name="cmp_partial_paged",
    )(pt, *([vk] * P), *([vv] * P), _cmp_cat(w1k), _cmp_cat(w1v))


def _cmp_mlp_kernel(pk_ref, pv_ref, b1k_ref, b1v_ref, w2k_ref, w2v_ref, ok_ref, ov_ref, *, ns):
    for p_ref, b_ref, w_ref, o_ref in ((pk_ref, b1k_ref, w2k_ref, ok_ref), (pv_ref, b1v_ref, w2v_ref, ov_ref)):
        for kv in range(NSA_KV):
            p0 = p_ref[:, kv * HEAD_DIM:(kv + 1) * HEAD_DIM]
            p1 = p_ref[:, (NSA_KV + kv) * HEAD_DIM:(NSA_KV + kv + 1) * HEAD_DIM]
            nxt = pltpu.roll(p1, ns - 1, axis=0)
            h = _silu(p0 + nxt + b_ref[...])
            o_ref[:, kv * HEAD_DIM:(kv + 1) * HEAD_DIM] = jnp.dot(h.astype(BF16), w_ref[...],
                                                                  preferred_element_type=F32).astype(BF16)


def _cmp_mlp(pk, pv, b1k, b1v, w2k, w2v):
    B, ns, _ = pk.shape
    pspec = pl.BlockSpec((None, ns, 4 * HEAD_DIM), lambda b: (b, 0, 0))
    bspec = pl.BlockSpec((1, HEAD_DIM), lambda b: (0, 0))
    wspec = pl.BlockSpec((HEAD_DIM, HEAD_DIM), lambda b: (0, 0))
    out = jax.ShapeDtypeStruct((B, ns, NSA_KV * HEAD_DIM), BF16)
    ospec = pl.BlockSpec((None, ns, NSA_KV * HEAD_DIM), lambda b: (b, 0, 0))
    return pl.pallas_call(
        functools.partial(_cmp_mlp_kernel, ns=ns),
        out_shape=(out, out),
        grid=(B,),
        in_specs=[pspec, pspec, bspec, bspec, wspec, wspec],
        out_specs=(ospec, ospec),
        compiler_params=_cparams(("arbitrary",)),
        name="cmp_mlp",
    )(pk, pv, b1k.reshape(1, -1), b1v.reshape(1, -1), w2k.astype(BF16), w2v.astype(BF16))


def _cmp_attn_kernel(q_ref, kc_ref, vc_ref, ov_ref, o_ref, sel_ref, *, tq, ns, nb, q_base):
    qi = pl.program_id(1)
    q_pos = q_base + qi * tq + lax.broadcasted_iota(jnp.int32, (tq, 1), 0)
    n_idx = lax.broadcasted_iota(jnp.int32, (1, ns), 1)
    cmp_end = n_idx * CMP_STRIDE + (CMP_BLOCK - 1)
    mask = (cmp_end <= q_pos) & (n_idx < ns - 1)
    overlap = ov_ref[...]
    blk = lax.broadcasted_iota(jnp.int32, (tq, nb), 1)
    blkf = blk.astype(F32)
    cur = q_pos >> SLC_SHIFT
    forced = (blk == 0) | (blk == cur) | (blk == cur - 1)
    valid = blk <= cur
    n_pick = SLC_TOPN - jnp.where(cur >= nb, 1, 0)
    scores = []
    for g in range(NSA_KV):
        kb = kc_ref[:, g * HEAD_DIM:(g + 1) * HEAD_DIM]
        vb = vc_ref[:, g * HEAD_DIM:(g + 1) * HEAD_DIM]
        imp = jnp.zeros((tq, ns), F32)
        for h in range(GROUP):
            c = (g * GROUP + h) * HEAD_DIM
            qh = (q_ref[:, c:c + HEAD_DIM] * SCALE).astype(BF16)
            s = lax.dot_general(qh, kb, (((1,), (1,)), ((), ())), preferred_element_type=F32)
            s = jnp.where(mask, s, NEG_BIG)
            m = jnp.max(s, axis=-1, keepdims=True)
            e = jnp.where(mask, jnp.exp(s - m), 0.0)
            p = e / jnp.maximum(jnp.sum(e, axis=-1, keepdims=True), 1e-30)
            o_ref[:, c:c + HEAD_DIM] = jnp.dot(p.astype(BF16), vb, preferred_element_type=F32)
            imp = imp + p
        hi = imp.astype(BF16)
        r1 = imp - hi.astype(F32)
        mid = r1.astype(BF16)
        lo = (r1 - mid.astype(F32)).astype(BF16)
        imp_slc = (jnp.dot(hi, overlap, preferred_element_type=F32) + jnp.dot(mid, overlap, preferred_element_type=F32)
                   + jnp.dot(lo, overlap, preferred_element_type=F32))
        scores.append(jnp.where(valid, imp_slc + FORCE_BONUS * forced.astype(F32), NEG_BIG))

    def pick(it, carry):
        out = []
        for score, chosen in carry:
            best = jnp.max(score, axis=-1, keepdims=True)
            first = jnp.min(jnp.where(score == best, blkf, float(nb)), axis=-1, keepdims=True)
            hit = blkf == first
            out.append((jnp.where(hit, -jnp.inf, score), jnp.where(hit & (it < n_pick), 1.0, chosen)))
        return tuple(out)

    picked = lax.fori_loop(0, SLC_TOPN, pick, tuple((sc, jnp.zeros((tq, nb), F32)) for sc in scores))
    for g in range(NSA_KV):
        bias = jnp.where((picked[g][1] > 0.5) & valid, 0.0, SEL_OFF).astype(BF16)
        for half in range(nb // LANES):
            sel_ref[g, half] = bias[:, half * LANES:(half + 1) * LANES]


def _cmp_attn(q_arr, q_col, B, n_qblk, tq, kc, vc, q_base, nb):
    ns = kc.shape[1]
    per = SLC_BLOCK // CMP_STRIDE
    ci = np.arange(ns)[:, None]
    cb = np.arange(nb)[None, :]
    overlap = jnp.asarray((ci >= per * cb - 1) & (ci <= per * cb + per - 1), BF16)
    kspec = pl.BlockSpec((None, ns, NSA_KV * HEAD_DIM), lambda b, i: (b, 0, 0))
    return pl.pallas_call(
        functools.partial(_cmp_attn_kernel, tq=tq, ns=ns, nb=nb, q_base=q_base),
        out_shape=(jax.ShapeDtypeStruct((B * n_qblk * tq, NSA_HEADS * HEAD_DIM), F32),
                   jax.ShapeDtypeStruct((B, NSA_KV, nb // LANES, n_qblk * tq, LANES), BF16)),
        grid=(B, n_qblk),
        in_specs=[pl.BlockSpec((tq, NSA_HEADS * HEAD_DIM), lambda b, i: (b * n_qblk + i, q_col)), kspec, kspec,
                  pl.BlockSpec((ns, nb), lambda b, i: (0, 0))],
        out_specs=(pl.BlockSpec((tq, NSA_HEADS * HEAD_DIM), lambda b, i: (b * n_qblk + i, 0)),
                   pl.BlockSpec((None, NSA_KV, nb // LANES, tq, LANES), lambda b, i: (b, 0, 0, i, 0))),
        compiler_params=_cparams(("arbitrary", "arbitrary")),
        name="cmp_attn_select",
    )(q_arr, kc, vc, overlap)


def _mix_out_kernel(x_ref, of_ref, oc_ref, os_ref, ow_ref, sm_ref, fz_ref, nz_ref, w_ref, h_ref):
    mix_f = of_ref[...] * _silu(fz_ref[...])
    sm = sm_ref[...]
    parts = []
    for h in range(NSA_HEADS):
        sl = slice(h * HEAD_DIM, (h + 1) * HEAD_DIM)
        c = FOX_HEADS + h * N_BRANCH
        parts.append(sm[:, c:c + 1] * oc_ref[:, sl] + sm[:, c + 1:c + 2] * os_ref[:, sl]
                     + sm[:, c + 2:c + 3] * ow_ref[:, sl])
    mix_n = jnp.concatenate(parts, axis=1) * _silu(nz_ref[...])
    mix = jnp.concatenate([mix_f, mix_n], axis=1).astype(BF16)
    h_ref[...] = x_ref[...] + jnp.dot(mix, w_ref[...], preferred_element_type=F32)


def _mix_out(x, o_f, o_c, o_s, o_w, big, w_out, tm):
    R, D = x.shape
    row = lambda i: (i, 0)
    wide = pl.BlockSpec((tm, 1024), row)
    return pl.pallas_call(
        _mix_out_kernel,
        out_shape=jax.ShapeDtypeStruct((R, D), F32),
        grid=(R // tm,),
        in_specs=[pl.BlockSpec((tm, D), row), wide, wide, wide, wide,
                  pl.BlockSpec((tm, LANES), lambda i: (i, C_SM * 2)),
                  pl.BlockSpec((tm, 1024), lambda i: (i, C_FZ // 4)),
                  pl.BlockSpec((tm, 1024), lambda i: (i, C_NZ // 4)),
                  pl.BlockSpec(w_out.shape, lambda i: (0, 0))],
        out_specs=pl.BlockSpec((tm, D), row),
        compiler_params=_cparams(("arbitrary",)),
        name="mix_out",
    )(x, o_f, o_c, o_s, o_w, big, big, big, w_out)


def _ple_kernel(h_ref, p_ref, g_ref, wg_ref, wp_ref, y_ref):
    h = h_ref[...]
    ms = jnp.mean(h * h, axis=-1, keepdims=True)
    hn = (h * lax.rsqrt(ms + EPS) * g_ref[...]).astype(BF16)
    gate = _sigmoid(jnp.dot(hn, wg_ref[...], preferred_element_type=F32))
    y_ref[...] = h + gate * jnp.dot(p_ref[...].astype(BF16), wp_ref[...], preferred_element_type=F32)


def _ple(h, p, ple_norm, w_gate, w_ple, tm):
    R, D = h.shape
    row = lambda i: (i, 0)
    return pl.pallas_call(
        _ple_kernel,
        out_shape=jax.ShapeDtypeStruct((R, D), F32),
        grid=(R // tm,),
        in_specs=[pl.BlockSpec((tm, D), row), pl.BlockSpec((tm, p.shape[1]), row),
                  pl.BlockSpec((1, D), lambda i: (0, 0)),
                  pl.BlockSpec(w_gate.shape, lambda i: (0, 0)),
                  pl.BlockSpec(w_ple.shape, lambda i: (0, 0))],
        out_specs=pl.BlockSpec((tm, D), row),
        compiler_params=_cparams(("arbitrary",)),
        name="ple_gate",
    )(h, p, ple_norm.reshape(1, D), w_gate, w_ple)


STK = GROUP * 4


def _decode_kernel(pt_ref, *refs, P, n_tok, has_bias, has_sel):
    q_ref = refs[0]
    k_pages, v_pages = refs[1:1 + P], refs[1 + P:1 + 2 * P]
    pos = 1 + 2 * P
    cq_ref = cs_ref = cst_ref = sel_ref = None
    if has_bias:
        cq_ref, cs_ref, cst_ref = refs[pos:pos + 3]
        pos += 3
    if has_sel:
        sel_ref = refs[pos]
        pos += 1
    kt_ref, vt_ref, o_ref, m_ref, l_ref, acc_ref = refs[pos:pos + 6]
    step = pl.program_id(1)
    n_keys = P * PAGE

    @pl.when(step == 0)
    def _():
        m_ref[...] = jnp.full(m_ref.shape, NEG_BIG, F32)
        l_ref[...] = jnp.zeros(l_ref.shape, F32)
        acc_ref[...] = jnp.zeros(acc_ref.shape, F32)

    def update(g, s, vb):
        m_prev = m_ref[g]
        m_new = jnp.maximum(m_prev, jnp.max(s, axis=-1, keepdims=True))
        alpha = jnp.exp(m_prev - m_new)
        p = jnp.exp(s - m_new)
        l_ref[g] = alpha * l_ref[g] + jnp.sum(p, axis=-1, keepdims=True)
        acc_ref[g] = alpha * acc_ref[g] + jnp.dot(p.astype(BF16), vb, preferred_element_type=F32)
        m_ref[g] = m_new

    if has_sel:
        blk = ((step * n_keys + lax.broadcasted_iota(jnp.int32, (1, n_keys), 1)) >> SLC_SHIFT) & (LANES - 1)
        onehot = (lax.broadcasted_iota(jnp.int32, (LANES, n_keys), 0) == blk).astype(BF16)

    qb = [(q_ref[g] * SCALE).astype(BF16) for g in range(NSA_KV)]
    for g in range(NSA_KV):
        kb = jnp.concatenate([r[pl.ds(g, PAGE, stride=NSA_KV), :] for r in k_pages], axis=0).astype(BF16)
        vb = jnp.concatenate([r[pl.ds(g, PAGE, stride=NSA_KV), :] for r in v_pages], axis=0).astype(BF16)
        s = lax.dot_general(qb[g], kb, (((1,), (1,)), ((), ())), preferred_element_type=F32)
        if has_bias:
            s = s + (cq_ref[g] - cs_ref[g])
        if has_sel:
            s = s + jnp.dot(sel_ref[g], onehot, preferred_element_type=F32)
        update(g, s, vb)

    @pl.when(step == pl.num_programs(1) - 1)
    def _():
        row_tok = lax.broadcasted_iota(jnp.int32, (STK, DEC_PAD), 0) & 3
        key_tok = lax.broadcasted_iota(jnp.int32, (STK, DEC_PAD), 1)
        mask = (key_tok <= row_tok) & (key_tok < n_tok)
        for g in range(NSA_KV):
            kb = kt_ref[:, g * HEAD_DIM:(g + 1) * HEAD_DIM].astype(BF16)
            vb = vt_ref[:, g * HEAD_DIM:(g + 1) * HEAD_DIM].astype(BF16)
            s = lax.dot_general(qb[g], kb, (((1,), (1,)), ((), ())), preferred_element_type=F32)
            if has_bias:
                s = s + (cq_ref[g] - cst_ref[g])
            update(g, jnp.where(mask, s, NEG_BIG), vb)
            o_ref[g] = acc_ref[g] / jnp.maximum(l_ref[g], 1e-30)


def _decode_attn(q_st, cache_k, cache_v, page_table, big, k_col, v_col, P, n_tok, *, bias=None, sel=None, name):
    B, n_pages = page_table.shape
    n_pool = cache_k.shape[0]
    k2 = cache_k.reshape(n_pool * PAGE * NSA_KV, HEAD_DIM)
    v2 = cache_v.reshape(n_pool * PAGE * NSA_KV, HEAD_DIM)
    pt = page_table.reshape(-1)
    n_keys = P * PAGE

    def page(u):
        return pl.BlockSpec((PAGE * NSA_KV, HEAD_DIM), lambda b, s, pt_ref: (pt_ref[b * n_pages + s * P + u], 0))

    in_specs = [pl.BlockSpec((None, NSA_KV, STK, HEAD_DIM), lambda b, s, pt_ref: (b, 0, 0, 0))]
    in_specs += [page(u) for u in range(P)] * 2
    args = [q_st] + [k2] * P + [v2] * P
    if bias is not None:
        in_specs += [pl.BlockSpec((None, NSA_KV, STK, 1), lambda b, s, pt_ref: (b, 0, 0, 0)),
                     pl.BlockSpec((None, NSA_KV, STK, n_keys), lambda b, s, pt_ref: (b, 0, 0, s)),
                     pl.BlockSpec((None, NSA_KV, STK, DEC_PAD), lambda b, s, pt_ref: (b, 0, 0, 0))]
        args += list(bias)
    if sel is not None:
        per_step = n_keys // SLC_BLOCK
        in_specs.append(pl.BlockSpec((None, NSA_KV, None, STK, LANES),
                                     lambda b, s, pt_ref: (b, 0, (s * per_step) // LANES, 0, 0)))
        args.append(sel)
    in_specs += [pl.BlockSpec((DEC_PAD, CH), lambda b, s, pt_ref: (b, k_col)),
                 pl.BlockSpec((DEC_PAD, CH), lambda b, s, pt_ref: (b, v_col))]
    args += [big, big]
    return pl.pallas_call(
        functools.partial(_decode_kernel, P=P, n_tok=n_tok, has_bias=bias is not None, has_sel=sel is not None),
        out_shape=jax.ShapeDtypeStruct((B, NSA_KV, STK, HEAD_DIM), F32),
        grid_spec=pltpu.PrefetchScalarGridSpec(
            num_scalar_prefetch=1,
            grid=(B, n_pages // P),
            in_specs=in_specs,
            out_specs=pl.BlockSpec((None, NSA_KV, STK, HEAD_DIM), lambda b, s, pt_ref: (b, 0, 0, 0)),
            scratch_shapes=[pltpu.VMEM((NSA_KV, STK, 1), F32), pltpu.VMEM((NSA_KV, STK, 1), F32),
                            pltpu.VMEM((NSA_KV, STK, HEAD_DIM), F32)],
        ),
        compiler_params=_cparams(("arbitrary", "arbitrary")),
        name=name,
    )(pt, *args)


def _stack_rows(a, n_tok):
    B = a.shape[0]
    x = a.shape[-1] // (NSA_KV * GROUP)
    return a[:, :n_tok].reshape(B, n_tok, NSA_KV, GROUP, x).transpose(0, 2, 3, 1, 4).reshape(B, NSA_KV, STK, x)


def _unstack_rows(o_st, n_tok):
    B = o_st.shape[0]
    o = o_st.reshape(B, NSA_KV, GROUP, n_tok, HEAD_DIM).transpose(0, 3, 1, 2, 4).reshape(B, n_tok, -1)
    return jnp.pad(o, ((0, 0), (0, DEC_PAD - n_tok), (0, 0))).reshape(B * DEC_PAD, -1)


def _pick(n, cands):
    for c in cands:
        if n % c == 0:
            return c
    raise ValueError(f"no tile in {cands} divides {n}")


def _col(big, c, width=CH):
    return big[:, c * CH:c * CH + width]


def _prompt_layer(x, p_i, prm):
    T, D = x.shape
    big = _inproj(x, jnp.arange(T, dtype=jnp.int32), prm["attn_norm"], prm["w_perm"], prm["gain"], prm["bf_pad"],
                  _pick(T, (1024, 512, 256, 128)))
    logf = big[:, C_SM * CH:C_SM * CH + FOX_HEADS]
    nr = T // LANES
    c4, _ = _cumsum(logf.reshape(1, nr, LANES, FOX_HEADS).transpose(0, 1, 3, 2),
                    jnp.zeros((1, FOX_HEADS, DEC_PAD), F32))
    c_rows = c4.transpose(0, 1, 3, 2).reshape(T, FOX_HEADS)

    tq = WINDOW
    assert T % tq == 0
    nqb = T // tq
    qm, fa, kf, vf, ks, vs, kw, vw = _pack(big, c_rows, tq)
    tabs = _prompt_tables(nqb, tq)
    o_f = _pflash(qm, 0, kf, vf, tabs, tq, aug=fa, aug_mode="head", name="fox_prompt")

    pk, pv = _cmp_partial_prompt(big, T, prm["w1k"], prm["w1v"], _pick(T // CMP_STRIDE, (256, 128, 64)))
    kc, vc = _cmp_mlp(pk[None], pv[None], prm["b1k"], prm["b1v"], prm["w2k"], prm["w2v"])
    nb = -(-(T // SLC_BLOCK) // LANES) * LANES
    tqc = _pick(T, (256, 128))
    o_c, sel = _cmp_attn(big, C_NQ // 4, 1, T // tqc, tqc, kc, vc, 0, nb)
    o_s = _pflash(qm, 1, ks, vs, tabs, tq, aug=sel[0], aug_mode="group", name="slc_prompt")
    o_w = _pflash(qm, 1, kw, vw, _prompt_tables(nqb, tq, window=True), tq, name="win_prompt")

    tm = _pick(T, (256, 128))
    h = _mix_out(x, o_f, o_c, o_s, o_w, big, prm["w_out"], tm)
    y = _ple(h, p_i, prm["ple_norm"], prm["w_gate"], prm["w_ple"], tm)
    n_win = min(WINDOW, T)
    kv5 = lambda c: _col(big, c).reshape(1, 1, T, 2, HEAD_DIM)
    state = (kv5(C_FK), kv5(C_FV), logf.reshape(1, 1, T, FOX_HEADS), kv5(C_CK), kv5(C_CV), kv5(C_SK), kv5(C_SV),
             kv5(C_WK)[:, :, T - n_win:], kv5(C_WV)[:, :, T - n_win:])
    return y.reshape(1, T, D), state


def _sample_layer(x, p_i, caches, page_table, prm):
    c_fk, c_fv, c_flogf, c_ck, c_cv, c_sk, c_sv, c_wk, c_wv = caches
    B, Tn, D = x.shape
    n_pages = page_table.shape[1]
    past = n_pages * PAGE
    R = B * DEC_PAD
    xp = jnp.pad(x, ((0, 0), (0, DEC_PAD - Tn), (0, 0))).reshape(R, D)
    pos = jnp.tile(past + jnp.arange(DEC_PAD, dtype=jnp.int32), B)
    big = _inproj(xp, pos, prm["attn_norm"], prm["w_perm"], prm["gain"], prm["bf_pad"], _pick(R, (256, 128, 16)))
    logf = big[:, C_SM * CH:C_SM * CH + FOX_HEADS]

    assert Tn * GROUP == STK, "decode kernel stacks 4 heads x 4 new tokens per KV group"
    P = _pick(n_pages, (16, 8, 4, 2, 1))
    lf_pages = _gather_logf_pages(c_flogf.transpose(0, 2, 1), page_table, P)
    c4, c_new = _cumsum(lf_pages, logf.reshape(B, DEC_PAD, FOX_HEADS).transpose(0, 2, 1))
    rep = lambda a: jnp.repeat(a, Tn, axis=2)
    cs_past = rep(c4.transpose(0, 2, 1, 3).reshape(B, FOX_KV, GROUP, past))
    cs_tail = rep(c_new.reshape(B, FOX_KV, GROUP, DEC_PAD))
    cq = c_new[:, :, :Tn].reshape(B, FOX_KV, STK, 1)

    q3 = lambda c: big[:, c * CH:c * CH + 1024].reshape(B, DEC_PAD, 1024)
    o_f = _unstack_rows(
        _decode_attn(_stack_rows(q3(C_FQ), Tn), c_fk, c_fv, page_table, big, C_FK, C_FV, P, Tn,
                     bias=(cq, cs_past, cs_tail), name="fox_decode"), Tn)

    pk, pv = _cmp_partial_paged(c_ck, c_cv, page_table, prm["w1k"], prm["w1v"], P)
    kc, vc = _cmp_mlp(pk, pv, prm["b1k"], prm["b1v"], prm["w2k"], prm["w2v"])
    nb = -(-(past // SLC_BLOCK) // LANES) * LANES
    o_c, sel = _cmp_attn(big, C_NQ // 4, B, 1, DEC_PAD, kc, vc, past, nb)
    sel_st = jnp.tile(sel[:, :, :, None, :Tn], (1, 1, 1, GROUP, 1, 1)).reshape(B, NSA_KV, nb // LANES, STK, LANES)
    o_s = _unstack_rows(
        _decode_attn(_stack_rows(q3(C_NQ), Tn), c_sk, c_sv, page_table, big, C_SK, C_SV, P, Tn,
                     sel=sel_st, name="slc_decode"), Tn)
    n_buf = c_wk.shape[1]
    wspec = [pl.BlockSpec((n_buf, HEAD_DIM), lambda b, g, s, *_: (b, g))]
    wflat = lambda c: c.reshape(B * n_buf, NSA_KV * HEAD_DIM)
    o_w = _flash(big, C_NQ // 2, DEC_PAD, 1, B, _linear_tables(1), wspec, wspec, [wflat(c_wk), wflat(c_wv)], n_buf,
                 tail=(big, C_WK * 2, big, C_WV * 2), windowed=True,
                 q_base=past, k_base=past - n_buf, tail_base=past, name="win_decode")

    tm = _pick(R, (256, 128, 16))
    pp = jnp.pad(p_i, ((0, 0), (0, DEC_PAD - Tn), (0, 0))).reshape(R, -1)
    h = _mix_out(xp, o_f, o_c, o_s, o_w, big, prm["w_out"], tm)
    y = _ple(h, pp, prm["ple_norm"], prm["w_gate"], prm["w_ple"], tm)
    y = y.reshape(B, DEC_PAD, D)[:, :Tn]
    new = lambda c: _col(big, c).reshape(B, DEC_PAD, 2, HEAD_DIM)[:, :Tn]
    wk_new, wv_new = new(C_WK), new(C_WV)
    kw = jnp.concatenate([c_wk, wk_new], axis=1)[:, -n_buf:]
    vw = jnp.concatenate([c_wv, wv_new], axis=1)[:, -n_buf:]
    state = (new(C_FK), new(C_FV), logf.reshape(B, DEC_PAD, FOX_HEADS)[:, :Tn], new(C_CK), new(C_CV), new(C_SK),
             new(C_SV), kw, vw)
    return y, tuple(s[None] for s in state)


def kernel(x_prompt, x_sample, cache_fox_k, cache_fox_v, cache_fox_logf, cache_cmp_k, cache_cmp_v, cache_slc_k,
           cache_slc_v, cache_win_k, cache_win_v, page_table, p_prompt, p_sample, attn_norm, w_in, b_forget,
           fox_q_norm, fox_k_norm, nsa_q_norm, nsa_k_norm, cmp_k_w1, cmp_k_b1, cmp_k_w2, cmp_v_w1, cmp_v_b1,
           cmp_v_w2, w_out, ple_norm, w_ple, w_ple_gate):
    assert x_prompt.shape[0] == 1 and w_in.shape[0] == 1, "one prompt sequence, one layer"
    w_perm, gain, bf_pad = _prep_inproj_params(w_in[0], b_forget[0], fox_q_norm[0], fox_k_norm[0], nsa_q_norm[0],
                                               nsa_k_norm[0])
    prm = dict(attn_norm=attn_norm[0], w_perm=w_perm, gain=gain, bf_pad=bf_pad,
               w1k=cmp_k_w1[0].astype(BF16), w1v=cmp_v_w1[0].astype(BF16), b1k=cmp_k_b1[0], b1v=cmp_v_b1[0],
               w2k=cmp_k_w2[0], w2v=cmp_v_w2[0], w_out=w_out[0].astype(BF16), ple_norm=ple_norm[0],
               w_gate=w_ple_gate[0].astype(BF16), w_ple=w_ple[0].astype(BF16))
    y_p, st_p = _prompt_layer(x_prompt[0], p_prompt[0, 0], prm)
    caches = (cache_fox_k[0], cache_fox_v[0], cache_fox_logf[0], cache_cmp_k[0], cache_cmp_v[0], cache_slc_k[0],
              cache_slc_v[0], cache_win_k[0], cache_win_v[0])
    y_s, st_s = _sample_layer(x_sample, p_sample[0], caches, page_table, prm)
    return (y_p, y_s) + tuple(st_p) + tuple(st_s)
```

```python
import functools

import numpy as np
import jax
import jax.numpy as jnp
from jax import lax
from jax.experimental import pallas as pl
from jax.experimental.pallas import tpu as pltpu

F32 = jnp.float32
BF16 = jnp.bfloat16

HEAD_DIM = 128
FOX_HEADS = 8
FOX_KV = 2
NSA_HEADS = 8
NSA_KV = 2
GROUP = 4
N_BRANCH = 3
ROPE_THETA = 500000.0
ROT_DIM = HEAD_DIM // 4
ROT_HALF = ROT_DIM // 2
CMP_STRIDE = 16
CMP_BLOCK = 2 * CMP_STRIDE
SLC_BLOCK = 64
SLC_SHIFT = 6
SLC_TOPN = 16
WINDOW = 512
EPS = 1e-6
NEG_BIG = -1e30
FORCE_BONUS = 1e4
SEL_OFF = -30000.0
PAGE = 128
SCALE = HEAD_DIM ** -0.5

SPLIT_SIZES = (1024, 256, 256, 8, 1024, 1024, 256, 256, 256, 256, 256, 256, 24, 1024)

CH = 256
C_FQ, C_NQ, C_FZ, C_NZ = 0, 4, 8, 12
C_FK, C_CK, C_SK, C_WK, C_FV, C_CV, C_SV, C_WV, C_SM = 16, 17, 18, 19, 20, 21, 22, 23, 24
N_CH = 25
NP = N_CH * CH
LANES = 128
DEC_PAD = 16

VMEM_LIMIT = 56 * 1024 * 1024


def _cparams(sem):
    return pltpu.CompilerParams(dimension_semantics=sem, vmem_limit_bytes=VMEM_LIMIT)


def _sigmoid(x):
    return 1.0 / (1.0 + jnp.exp(-x))


def _silu(x):
    return x * _sigmoid(x)


def _inproj_kernel(x_ref, g_ref, w_ref, gain_ref, bf_ref, rc_ref, rs1_ref, rs2_ref, o_ref, xn_ref):
    j = pl.program_id(1)

    @pl.when(j == 0)
    def _():
        x = x_ref[...]
        ms = jnp.mean(x * x, axis=-1, keepdims=True)
        xn_ref[...] = (x * lax.rsqrt(ms + EPS) * g_ref[...]).astype(BF16)

    y = jnp.dot(xn_ref[...], w_ref[...], preferred_element_type=F32)

    is_q_rope = (j >= C_NQ) & (j < C_FZ)
    is_k_rope = (j >= C_CK) & (j <= C_WK)
    is_norm_only = (j < C_NQ) | (j == C_FK)
    is_rope = is_q_rope | is_k_rope
    is_raw = ((j >= C_FZ) & (j < C_FK)) | ((j >= C_FV) & (j < C_SM))

    def normed(h):
        yh = y[:, h * LANES:(h + 1) * LANES]
        ms = jnp.mean(yh * yh, axis=-1, keepdims=True)
        return yh * lax.rsqrt(ms + EPS) * gain_ref[:, h * LANES:(h + 1) * LANES]

    @pl.when(is_norm_only)
    def _():
        for h in range(CH // LANES):
            o_ref[:, h * LANES:(h + 1) * LANES] = normed(h)

    @pl.when(is_rope)
    def _():
        for h in range(CH // LANES):
            yn = normed(h)
            lo = pltpu.roll(yn, LANES - ROT_HALF, axis=1)
            hi = pltpu.roll(yn, ROT_HALF, axis=1)
            o_ref[:, h * LANES:(h + 1) * LANES] = yn * rc_ref[...] + lo * rs1_ref[...] + hi * rs2_ref[...]

    @pl.when(is_raw)
    def _():
        o_ref[...] = y

    @pl.when(j == C_SM)
    def _():
        t = y[:, :LANES] + bf_ref[...]
        lane = lax.broadcasted_iota(jnp.int32, t.shape, 1)
        e = jnp.exp(-jnp.abs(t))
        logsig = jnp.minimum(t, 0.0) - jnp.log(1.0 + e)
        o_ref[:, :LANES] = jnp.where(lane < FOX_HEADS, logsig, _sigmoid(t))
        o_ref[:, LANES:] = jnp.zeros_like(t)


def _inproj(x, pos, attn_norm, w_perm, gain, bf_pad, tm):
    R, D = x.shape
    inv = ROPE_THETA ** (-(2.0 / ROT_DIM) * jnp.arange(ROT_HALF, dtype=F32))
    ang = pos.astype(F32)[:, None] * inv[None, :]
    cos, sin = jnp.cos(ang), jnp.sin(ang)
    z = jnp.zeros((R, LANES - ROT_DIM), F32)
    zh = jnp.zeros((R, ROT_HALF), F32)
    rc = jnp.concatenate([cos, cos, z + 1.0], axis=1)
    rs1 = jnp.concatenate([-sin, zh, z], axis=1)
    rs2 = jnp.concatenate([zh, sin, z], axis=1)
    row = lambda i, j: (i, 0)
    return pl.pallas_call(
        _inproj_kernel,
        out_shape=jax.ShapeDtypeStruct((R, NP), F32),
        grid=(R // tm, N_CH),
        in_specs=[
            pl.BlockSpec((tm, D), row),
            pl.BlockSpec((1, D), lambda i, j: (0, 0)),
            pl.BlockSpec((D, CH), lambda i, j: (0, j)),
            pl.BlockSpec((1, CH), lambda i, j: (0, j)),
            pl.BlockSpec((1, LANES), lambda i, j: (0, 0)),
            pl.BlockSpec((tm, LANES), row),
            pl.BlockSpec((tm, LANES), row),
            pl.BlockSpec((tm, LANES), row),
        ],
        out_specs=pl.BlockSpec((tm, CH), lambda i, j: (i, j)),
        scratch_shapes=[pltpu.VMEM((tm, D), BF16)],
        compiler_params=_cparams(("arbitrary", "arbitrary")),
        name="inproj",
    )(x, attn_norm.reshape(1, D), w_perm, gain, bf_pad, rc, rs1, rs2)


def _prep_inproj_params(w_in, b_forget, fox_q_norm, fox_k_norm, nsa_q_norm, nsa_k_norm):
    D = w_in.shape[0]
    offs = np.concatenate([[0], np.cumsum(SPLIT_SIZES)]).tolist()
    seg = [w_in[:, offs[i]:offs[i + 1]] for i in range(len(SPLIT_SIZES))]
    fq, fk, fv, flg, fz, nq, ck, cv, sk, sv, wk, wv, ng, nz = seg
    small = jnp.concatenate([flg, ng, jnp.zeros((D, CH - FOX_HEADS - NSA_HEADS * N_BRANCH), F32)], axis=1)
    w_perm = jnp.concatenate([fq, nq, fz, nz, fk, ck, sk, wk, fv, cv, sv, wv, small], axis=1).astype(BF16)
    zeros = lambda n: jnp.zeros((n,), F32)
    gain = jnp.concatenate([
        jnp.tile(fox_q_norm, FOX_HEADS), jnp.tile(nsa_q_norm, NSA_HEADS), zeros(2048),
        jnp.tile(fox_k_norm, FOX_KV), jnp.tile(nsa_k_norm[0], NSA_KV), jnp.tile(nsa_k_norm[1], NSA_KV),
        jnp.tile(nsa_k_norm[2], NSA_KV), zeros(4 * CH + CH)]).reshape(1, NP)
    bf_pad = jnp.concatenate([b_forget, zeros(LANES - FOX_HEADS)]).reshape(1, LANES)
    return w_perm, gain, bf_pad


def _gather_pages_kernel(pt_ref, *refs):
    n = len(refs) - 1
    o_ref = refs[n]
    for u in range(n):
        o_ref[u] = refs[u][...]


def _gather_logf_pages(cache_t, page_table, pages_per_step):
    B, n_pages = page_table.shape
    P = pages_per_step
    pt = page_table.reshape(-1)

    def src_map(u):
        return lambda b, c, pt_ref: (pt_ref[b * n_pages + c * P + u], 0, 0)

    return pl.pallas_call(
        _gather_pages_kernel,
        out_shape=jax.ShapeDtypeStruct((B, n_pages, FOX_HEADS, PAGE), F32),
        grid_spec=pltpu.PrefetchScalarGridSpec(
            num_scalar_prefetch=1,
            grid=(B, n_pages // P),
            in_specs=[pl.BlockSpec((None, FOX_HEADS, PAGE), src_map(u)) for u in range(P)],
            out_specs=pl.BlockSpec((None, P, FOX_HEADS, PAGE), lambda b, c, pt_ref: (b, c, 0, 0)),
        ),
        compiler_params=_cparams(("arbitrary", "arbitrary")),
        name="gather_logf_pages",
    )(pt, *([cache_t] * P))


def _cumsum_kernel(x_ref, new_ref, o_ref, onew_ref, *, nr):
    hp = lax.Precision.HIGHEST
    x = x_ref[...].reshape(nr * FOX_HEADS, LANES)
    ci = lax.broadcasted_iota(jnp.int32, (LANES, LANES), 0)
    cj = lax.broadcasted_iota(jnp.int32, (LANES, LANES), 1)
    upper = (ci <= cj).astype(F32)
    lastcol = (ci == LANES - 1).astype(F32)
    within = jnp.dot(x, upper, precision=hp, preferred_element_type=F32)
    rowtot = jnp.dot(within, lastcol, precision=hp, preferred_element_type=F32)
    n = nr * FOX_HEADS
    ri = lax.broadcasted_iota(jnp.int32, (n, n), 0)
    rj = lax.broadcasted_iota(jnp.int32, (n, n), 1)
    same_head = (ri & (FOX_HEADS - 1)) == (rj & (FOX_HEADS - 1))
    before = (same_head & (rj < ri)).astype(F32)
    prefix = jnp.dot(before, rowtot, precision=hp, preferred_element_type=F32)
    o_ref[...] = (within + prefix).reshape(nr, FOX_HEADS, LANES)
    total = (prefix + rowtot)[n - FOX_HEADS:, :]
    nw = new_ref[...]
    ti = lax.broadcasted_iota(jnp.int32, (DEC_PAD, DEC_PAD), 0)
    tj = lax.broadcasted_iota(jnp.int32, (DEC_PAD, DEC_PAD), 1)
    onew_ref[...] = total[:, :DEC_PAD] + jnp.dot(nw, (ti <= tj).astype(F32), precision=hp,
                                                  preferred_element_type=F32)


def _cumsum(x4, new):
    B, nr = x4.shape[:2]
    return pl.pallas_call(
        functools.partial(_cumsum_kernel, nr=nr),
        out_shape=(jax.ShapeDtypeStruct(x4.shape, F32), jax.ShapeDtypeStruct(new.shape, F32)),
        grid=(B,),
        in_specs=[pl.BlockSpec((None, nr, FOX_HEADS, LANES), lambda b: (b, 0, 0, 0)),
                  pl.BlockSpec((None, FOX_HEADS, DEC_PAD), lambda b: (b, 0, 0))],
        out_specs=(pl.BlockSpec((None, nr, FOX_HEADS, LANES), lambda b: (b, 0, 0, 0)),
                   pl.BlockSpec((None, FOX_HEADS, DEC_PAD), lambda b: (b, 0, 0))),
        compiler_params=_cparams(("arbitrary",)),
        name="cumsum_logf",
    )(x4, new)


def _flash_kernel(*refs, n_sp, n_src, tq, n_keys, has_bias, has_sel, has_tail, windowed,
                  q_base, k_base, tail_base):
    qi_tab, kj_tab, first_tab, last_tab = refs[:4]
    refs = refs[n_sp:]
    q_ref = refs[0]
    k_srcs = refs[1:1 + n_src]
    v_srcs = refs[1 + n_src:1 + 2 * n_src]
    pos = 1 + 2 * n_src
    cq_ref = cs_ref = sel_ref = kt_ref = vt_ref = cst_ref = None
    if has_bias:
        cq_ref, cs_ref = refs[pos], refs[pos + 1]
        pos += 2
    if has_sel:
        sel_ref = refs[pos]
        pos += 1
    if has_tail:
        kt_ref, vt_ref = refs[pos], refs[pos + 1]
        pos += 2
        if has_bias:
            cst_ref = refs[pos]
            pos += 1
    o_ref = refs[pos]
    qs_ref, m_ref, l_ref, acc_ref = refs[pos + 1:pos + 5]

    step = pl.program_id(2)
    g = pl.program_id(1)
    qi = qi_tab[step]
    kj = kj_tab[step]

    @pl.when(first_tab[step] == 1)
    def _():
        qs_ref[...] = (q_ref[...] * SCALE).astype(BF16)
        m_ref[...] = jnp.full(m_ref.shape, NEG_BIG, F32)
        l_ref[...] = jnp.zeros(l_ref.shape, F32)
        acc_ref[...] = jnp.zeros(acc_ref.shape, F32)

    q_pos = q_base + qi * tq + lax.broadcasted_iota(jnp.int32, (tq, 1), 0)

    def attend(k, v, k_pos, cs, sel_bias):
        kb = k.astype(BF16)
        vb = v.astype(BF16)
        mask = k_pos <= q_pos
        if windowed:
            mask = mask & ((q_pos - k_pos) < WINDOW)
        for h in range(GROUP):
            s = lax.dot_general(qs_ref[:, h * HEAD_DIM:(h + 1) * HEAD_DIM], kb, (((1,), (1,)), ((), ())),
                                preferred_element_type=F32)
            if cs is not None:
                hh = g * GROUP + h
                cq = jnp.sum(jnp.where(lax.broadcasted_iota(jnp.int32, cq_ref.shape, 1) == hh, cq_ref[...], 0.0),
                             axis=1, keepdims=True)
                s = s + (cq - cs[h:h + 1, :])
            if sel_bias is not None:
                s = s + sel_bias
            s = jnp.where(mask, s, NEG_BIG)
            m_prev = m_ref[h]
            m_new = jnp.maximum(m_prev, jnp.max(s, axis=-1, keepdims=True))
            alpha = jnp.exp(m_prev - m_new)
            p = jnp.exp(s - m_new)
            l_ref[h] = alpha * l_ref[h] + jnp.sum(p, axis=-1, keepdims=True)
            acc_ref[h] = alpha * acc_ref[h] + jnp.dot(p.astype(BF16), vb, preferred_element_type=F32)
            m_ref[h] = m_new

    if n_src == 1:
        k, v = k_srcs[0][...], v_srcs[0][...]
    else:
        k = jnp.concatenate([r[...] for r in k_srcs], axis=0)
        v = jnp.concatenate([r[...] for r in v_srcs], axis=0)
    k_pos = k_base + kj * n_keys + lax.broadcasted_iota(jnp.int32, (1, n_keys), 1)
    sel_bias = None
    if has_sel:
        blk = ((kj * n_keys + lax.broadcasted_iota(jnp.int32, (1, n_keys), 1)) >> SLC_SHIFT) & (LANES - 1)
        onehot = (lax.broadcasted_iota(jnp.int32, (LANES, n_keys), 0) == blk).astype(BF16)
        sel_bias = jnp.dot(sel_ref[...], onehot, preferred_element_type=F32)
    attend(k, v, k_pos, cs_ref[...] if has_bias else None, sel_bias)

    @pl.when(last_tab[step] == 1)
    def _():
        if has_tail:
            t_pos = tail_base + lax.broadcasted_iota(jnp.int32, (1, DEC_PAD), 1)
            attend(kt_ref[...], vt_ref[...], t_pos, cst_ref[...] if has_bias else None, None)
        for h in range(GROUP):
            o_ref[:, h * HEAD_DIM:(h + 1) * HEAD_DIM] = acc_ref[h] / jnp.maximum(l_ref[h], 1e-30)


def _flash(q_arr, q_col, tq, n_qblk, B, tables, k_specs, v_specs, kv_args, n_keys, *, prefetch=(),
           bias=None, sel=None, tail=None, windowed=False, q_base=0, k_base=0, tail_base=0, name="flash"):
    n_sp = len(tables) + len(prefetch)
    n_steps = tables[0].shape[0]
    n_src = len(k_specs)
    qmap = lambda b, g, s, qi, *_: (b * n_qblk + qi[s], q_col + g)
    in_specs = [pl.BlockSpec((tq, GROUP * HEAD_DIM), qmap)] + list(k_specs) + list(v_specs)
    args = [q_arr] + list(kv_args)
    if bias is not None:
        in_specs.append(pl.BlockSpec((tq, FOX_HEADS), lambda b, g, s, qi, *_: (b * n_qblk + qi[s], 0)))
        in_specs.append(pl.BlockSpec((None, None, GROUP, n_keys), lambda b, g, s, qi, kj, *_: (b, g, 0, kj[s])))
        args += [bias[0], bias[1]]
    if sel is not None:
        blocks_per_step = n_keys // SLC_BLOCK
        in_specs.append(pl.BlockSpec(
            (None, None, None, tq, LANES),
            lambda b, g, s, qi, kj, *_: (b, g, (kj[s] * blocks_per_step) // LANES, qi[s], 0)))
        args.append(sel)
    if tail is not None:
        k_arr, k_col, v_arr, v_col = tail
        in_specs.append(pl.BlockSpec((DEC_PAD, HEAD_DIM), lambda b, g, s, *_: (b, k_col + g)))
        in_specs.append(pl.BlockSpec((DEC_PAD, HEAD_DIM), lambda b, g, s, *_: (b, v_col + g)))
        args += [k_arr, v_arr]
        if bias is not None:
            in_specs.append(pl.BlockSpec((None, None, GROUP, DEC_PAD), lambda b, g, s, *_: (b, g, 0, 0)))
            args.append(bias[2])
    kern = functools.partial(
        _flash_kernel, n_sp=n_sp, n_src=n_src, tq=tq, n_keys=n_keys, has_bias=bias is not None,
        has_sel=sel is not None, has_tail=tail is not None, windowed=windowed,
        q_base=q_base, k_base=k_base, tail_base=tail_base)
    return pl.pallas_call(
        kern,
        out_shape=jax.ShapeDtypeStruct((B * n_qblk * tq, 2 * GROUP * HEAD_DIM), F32),
        grid_spec=pltpu.PrefetchScalarGridSpec(
            num_scalar_prefetch=n_sp,
            grid=(B, 2, n_steps),
            in_specs=in_specs,
            out_specs=pl.BlockSpec((tq, GROUP * HEAD_DIM), lambda b, g, s, qi, *_: (b * n_qblk + qi[s], g)),
            scratch_shapes=[pltpu.VMEM((tq, GROUP * HEAD_DIM), BF16),
                            pltpu.VMEM((GROUP, tq, 1), F32),
                            pltpu.VMEM((GROUP, tq, 1), F32),
                            pltpu.VMEM((GROUP, tq, HEAD_DIM), F32)],
        ),
        compiler_params=_cparams(("arbitrary", "arbitrary", "arbitrary")),
        name=name,
    )(*tables, *prefetch, *args)


def _causal_tables(n_blk, lookback=None):
    qi, kj, first, last = [], [], [], []
    for i in range(n_blk):
        lo = 0 if lookback is None else max(0, i - lookback)
        for j in range(lo, i + 1):
            qi.append(i)
            kj.append(j)
            first.append(int(j == lo))
            last.append(int(j == i))
    return tuple(jnp.asarray(a, jnp.int32) for a in (qi, kj, first, last))


def _linear_tables(n_steps):
    z = np.zeros((n_steps,), np.int32)
    first, last = z.copy(), z.copy()
    first[0], last[-1] = 1, 1
    return tuple(jnp.asarray(a, jnp.int32) for a in (z, np.arange(n_steps, dtype=np.int32), first, last))


LOG2E = 1.4426950408889634
AUG = 128
N_SPLIT = 3
AUG_C0 = N_SPLIT
NRM_ROWS = 16
UNDERFLOW_LOG2 = -160.0


def _split_part(x, part):
    hi = x.astype(BF16)
    r1 = x - hi.astype(F32)
    mid = r1.astype(BF16)
    lo = (r1 - mid.astype(F32)).astype(BF16)
    return jnp.where(part == 0, hi, jnp.where(part == 1, mid, lo))


def _pack_kernel(big_q, big_kv, c_ref, qm_ref, fa_ref, kf_ref, vf_ref, ks_ref, vs_ref, kw_ref, vw_ref, nrm_ref, *, tr):
    i = pl.program_id(0)
    lane = lax.broadcasted_iota(jnp.int32, (tr, AUG), 1)
    in_c = (lane >= AUG_C0) & (lane < AUG_C0 + GROUP * N_SPLIT)
    slot = jnp.zeros_like(lane)
    for j in range(1, GROUP):
        slot = slot + jnp.where(lane >= AUG_C0 + j * N_SPLIT, 1, 0)
    part = jnp.where(lane < AUG_C0, lane, lane - AUG_C0 - N_SPLIT * slot)
    c2 = c_ref[...] * LOG2E
    ones_lane = jnp.where(lane == 0, 1.0, 0.0).astype(BF16)
    zeros = jnp.zeros((tr, AUG), BF16)

    def max_sq_norm(x):
        xf = x.astype(F32)
        n2 = jnp.max(jnp.sum(xf * xf, axis=-1, keepdims=True), axis=0, keepdims=True)
        return jnp.broadcast_to(n2, (1, LANES))

    qmb = (big_q[...] * (SCALE * LOG2E)).astype(BF16)
    qm_ref[...] = qmb
    nrm_ref[...] = jnp.zeros(nrm_ref.shape, F32)
    for h in range(FOX_HEADS):
        nrm_ref[h:h + 1, :] = max_sq_norm(qmb[:, h * HEAD_DIM:(h + 1) * HEAD_DIM])
        cq = jnp.where(lane < AUG_C0, c2[:, h:h + 1], 0.0)
        own = in_c & (slot == h % GROUP)
        fa_ref[:, h * AUG:(h + 1) * AUG] = jnp.where(own, -1.0, _split_part(cq, part).astype(F32)).astype(BF16)

    tok = i * tr + lax.broadcasted_iota(jnp.int32, (tr, AUG), 0)
    blk_onehot = jnp.where(lane == ((tok >> SLC_SHIFT) & (AUG - 1)), 1.0, 0.0).astype(BF16)
    for g in range(NSA_KV):
        cs = jnp.zeros((tr, AUG), F32)
        for j in range(GROUP):
            cs = jnp.where(in_c & (slot == j), c2[:, g * GROUP + j:g * GROUP + j + 1], cs)
        k_aug = jnp.where(lane < AUG_C0, 1.0, _split_part(cs, part).astype(F32)).astype(BF16)
        col = lambda c: big_kv[:, (c - C_FK) * CH + g * HEAD_DIM:(c - C_FK) * CH + (g + 1) * HEAD_DIM].astype(BF16)
        lo, hi = g * 2 * HEAD_DIM, g * 2 * HEAD_DIM + HEAD_DIM
        for k_out, v_out, kc, vc, aug in ((kf_ref, vf_ref, C_FK, C_FV, k_aug), (ks_ref, vs_ref, C_SK, C_SV, blk_onehot),
                                          (kw_ref, vw_ref, C_WK, C_WV, zeros)):
            k_out[:, lo:hi] = col(kc)
            if kc == C_FK:
                nrm_ref[FOX_HEADS + g:FOX_HEADS + g + 1, :] = max_sq_norm(col(kc))
            k_out[:, hi:hi + AUG] = aug
            v_out[:, lo:hi] = col(vc)
            v_out[:, hi:hi + AUG] = ones_lane


def _pack(big, c_rows, tr):
    T = big.shape[0]
    row = lambda i: (i, 0)
    kv = jax.ShapeDtypeStruct((T, NSA_KV * (HEAD_DIM + AUG)), BF16)
    kvspec = pl.BlockSpec((tr, NSA_KV * (HEAD_DIM + AUG)), row)
    return pl.pallas_call(
        functools.partial(_pack_kernel, tr=tr),
        out_shape=(jax.ShapeDtypeStruct((T, 2048), BF16), jax.ShapeDtypeStruct((T, FOX_HEADS * AUG), BF16)) + (kv,) * 6
        + (jax.ShapeDtypeStruct((T // tr, NRM_ROWS, LANES), F32),),
        grid=(T // tr,),
        in_specs=[pl.BlockSpec((tr, 2048), row),
                  pl.BlockSpec((tr, 2048), lambda i: (i, C_FK // 8)),
                  pl.BlockSpec((tr, FOX_HEADS), row)],
        out_specs=(pl.BlockSpec((tr, 2048), row), pl.BlockSpec((tr, FOX_HEADS * AUG), row)) + (kvspec,) * 6
        + (pl.BlockSpec((None, NRM_ROWS, LANES), lambda i: (i, 0, 0)),),
        compiler_params=_cparams(("arbitrary",)),
        name="pack_qkv",
    )(big, big, c_rows)


def _pflash_kernel(qi_tab, kj_tab, first_tab, last_tab, kind_tab, aug_tab, skip_tab, kjd_tab, *refs, tq, tk,
                   aug_mode):
    qm_ref, k_ref, v_ref = refs[:3]
    pos = 3
    a_ref = None
    if aug_mode != "none":
        a_ref = refs[pos]
        pos += 1
    o_ref, qs_ref, m_ref, acc_ref = refs[pos:pos + 4]
    step = pl.program_id(1)
    active = skip_tab[pl.program_id(0) * pl.num_programs(1) + step] == 0

    @pl.when(first_tab[step] == 1)
    def _():
        for h in range(GROUP):
            qs_ref[h, :, :HEAD_DIM] = qm_ref[:, h * HEAD_DIM:(h + 1) * HEAD_DIM]
            if aug_mode == "head":
                qs_ref[h, :, HEAD_DIM:] = a_ref[:, h * AUG:(h + 1) * AUG]
            elif aug_mode == "none":
                qs_ref[h, :, HEAD_DIM:] = jnp.zeros((tq, AUG), BF16)
        m_ref[...] = jnp.full(m_ref.shape, NEG_BIG, F32)
        acc_ref[...] = jnp.zeros(acc_ref.shape, F32)

    if aug_mode == "group":
        @pl.when(aug_tab[step] == 1)
        def _():
            for h in range(GROUP):
                qs_ref[h, :, HEAD_DIM:] = a_ref[...]

    def attend(kind):
        kb = k_ref[...]
        vb = v_ref[...]
        if kind:
            ahead = (qi_tab[step] * tq - kj_tab[step] * tk + lax.broadcasted_iota(jnp.int32, (tq, tk), 0)
                     - lax.broadcasted_iota(jnp.int32, (tq, tk), 1))
            keep = (ahead >= 0) if kind == 1 else ((ahead >= 0) & (ahead < WINDOW))
        for h in range(GROUP):
            s = lax.dot_general(qs_ref[h], kb, (((1,), (1,)), ((), ())), preferred_element_type=F32)
            if kind:
                s = jnp.where(keep, s, NEG_BIG)
            m_prev = m_ref[h]
            m_tile = s[:, :LANES]
            for c in range(1, tk // LANES):
                m_tile = jnp.maximum(m_tile, s[:, c * LANES:(c + 1) * LANES])
            m_new = jnp.maximum(m_prev, jnp.max(m_tile, axis=-1, keepdims=True))
            p = jnp.exp2(s - jnp.tile(m_new, (1, tk // LANES))).astype(BF16)
            alpha = jnp.exp2(m_prev - m_new)
            acc_ref[h] = jnp.tile(alpha, (1, 2)) * acc_ref[h] + jnp.dot(p, vb, preferred_element_type=F32)
            m_ref[h] = m_new

    for kind in range(3):
        pl.when((kind_tab[step] == kind) & active)(functools.partial(attend, kind))

    @pl.when(last_tab[step] == 1)
    def _():
        for h in range(GROUP):
            acc = acc_ref[h]
            o_ref[:, h * HEAD_DIM:(h + 1) * HEAD_DIM] = acc[:, :HEAD_DIM] / jnp.maximum(
                acc[:, HEAD_DIM:HEAD_DIM + 1], 1e-30)


def _pflash(qm, q_col, k_arr, v_arr, tables, tq, tk, *, aug=None, aug_mode="none", skip=None, name):
    T = qm.shape[0]
    n_steps = tables[0].shape[0]
    kvw = HEAD_DIM + AUG
    in_specs = [pl.BlockSpec((tq, GROUP * HEAD_DIM), lambda g, s, qi, *_: (qi[s], q_col * 2 + g)),
                pl.BlockSpec((tk, kvw), lambda g, s, *tabs: (tabs[7][g * n_steps + s], g)),
                pl.BlockSpec((tk, kvw), lambda g, s, *tabs: (tabs[7][g * n_steps + s], g))]
    args = [qm, k_arr, v_arr]
    if aug_mode == "head":
        in_specs.append(pl.BlockSpec((tq, GROUP * AUG), lambda g, s, qi, *_: (qi[s], g)))
        args.append(aug)
    elif aug_mode == "group":
        per_step = tk // SLC_BLOCK
        in_specs.append(pl.BlockSpec((None, None, tq, AUG),
                                     lambda g, s, qi, kj, *_: (g, (kj[s] * per_step) // AUG, qi[s], 0)))
        args.append(aug)
    return pl.pallas_call(
        functools.partial(_pflash_kernel, tq=tq, tk=tk, aug_mode=aug_mode),
        out_shape=jax.ShapeDtypeStruct((T, 2 * GROUP * HEAD_DIM), F32),
        grid_spec=pltpu.PrefetchScalarGridSpec(
            num_scalar_prefetch=8,
            grid=(2, n_steps),
            in_specs=in_specs,
            out_specs=pl.BlockSpec((tq, GROUP * HEAD_DIM), lambda g, s, qi, *_: (qi[s], g)),
            scratch_shapes=[pltpu.VMEM((GROUP, tq, kvw), BF16),
                            pltpu.VMEM((GROUP, tq, LANES), F32),
                            pltpu.VMEM((GROUP, tq, kvw), F32)],
        ),
        compiler_params=_cparams(("arbitrary", "arbitrary")),
        name=name,
    )(*(jnp.asarray(t) for t in tables), *(skip if skip is not None else _no_skip(tables)), *args)


def _fox_skip(tables, nrm, c_rows, T, tq, tk):
    qi, kj, last = np.asarray(tables[0]), np.asarray(tables[1]), np.asarray(tables[3])
    nq, nt, r = T // tq, T // tk, tq // tk
    qn = jnp.sqrt(nrm[:, :FOX_HEADS, 0].reshape(nq, r, FOX_HEADS).max(axis=1))
    kn = jnp.sqrt(nrm[:, FOX_HEADS:FOX_HEADS + FOX_KV, 0])
    kn_own = kn.reshape(nq, r, FOX_KV).max(axis=1)
    c2 = c_rows * LOG2E
    c_first_q, c_last_k = c2[::tq], c2[tk - 1::tk]
    rep = lambda a: jnp.repeat(a, GROUP, axis=-1)
    logit_max = qn[qi] * rep(kn[kj]) + (c_first_q[qi] - c_last_k[kj])
    max_lb = -qn[qi] * rep(kn_own[qi])
    before = jnp.asarray((kj + 1) * tk - 1 < qi * tq)[:, None]
    dead = (before & (logit_max < max_lb + UNDERFLOW_LOG2)).reshape(-1, FOX_KV, GROUP).all(axis=-1)
    last_of_block = np.zeros_like(kj)
    nxt = 0
    for s in range(len(kj) - 1, -1, -1):
        if last[s] == 1:
            nxt = kj[s]
        last_of_block[s] = nxt
    skip = dead.T.astype(jnp.int32).reshape(-1)
    kjd = jnp.where(dead.T, jnp.asarray(last_of_block)[None, :], jnp.asarray(kj)[None, :]).astype(jnp.int32)
    return skip, kjd.reshape(-1)


def _no_skip(tables):
    n = tables[1].shape[0]
    return jnp.zeros((2 * n,), jnp.int32), jnp.asarray(np.tile(tables[1], 2))


def _prompt_tables(T, tq, tk, window=False):
    per_half = AUG * SLC_BLOCK // tk
    rows = []
    for i in range(T // tq):
        q_lo, q_hi = i * tq, (i + 1) * tq - 1
        lo = max(0, q_lo - WINDOW + 1) // tk if window else 0
        hi = q_hi // tk
        for j in range(lo, hi + 1):
            crosses = (j + 1) * tk - 1 > q_lo
            kind = 2 if window else int(crosses)
            rows.append((i, j, int(j == lo), int(j == hi), kind, int(j == lo or j % per_half == 0)))
    return tuple(np.asarray(a, np.int32) for a in zip(*rows))


def _cmp_cat(w1):
    return jnp.concatenate([w1[:CMP_STRIDE], w1[CMP_STRIDE:]], axis=-1)


def _cmp_partial_prompt_kernel(xk0_ref, xk1_ref, xv0_ref, xv1_ref, wk_ref, wv_ref, ok_ref, ov_ref, *, tn):
    for x_refs, w_ref, o_ref in (((xk0_ref, xk1_ref), wk_ref, ok_ref), ((xv0_ref, xv1_ref), wv_ref, ov_ref)):
        for kv in range(NSA_KV):
            acc = None
            for r in range(CMP_STRIDE):
                x = x_refs[kv][pl.ds(r, tn, stride=CMP_STRIDE), :]
                d = jnp.dot(x.astype(BF16), w_ref[r], preferred_element_type=F32)
                acc = d if acc is None else acc + d
            o_ref[:, kv * HEAD_DIM:(kv + 1) * HEAD_DIM] = acc[:, :HEAD_DIM]
            o_ref[:, (NSA_KV + kv) * HEAD_DIM:(NSA_KV + kv + 1) * HEAD_DIM] = acc[:, HEAD_DIM:]


def _cmp_partial_prompt(big, T, w1k, w1v, tn):
    ns = T // CMP_STRIDE
    wspec = pl.BlockSpec((CMP_STRIDE, HEAD_DIM, 2 * HEAD_DIM), lambda i: (0, 0, 0))
    out = jax.ShapeDtypeStruct((ns, 4 * HEAD_DIM), F32)
    return pl.pallas_call(
        functools.partial(_cmp_partial_prompt_kernel, tn=tn),
        out_shape=(out, out),
        grid=(ns // tn,),
        in_specs=[pl.BlockSpec((tn * CMP_STRIDE, HEAD_DIM), (lambda c: (lambda i: (i, c)))(c))
                  for c in (2 * C_CK, 2 * C_CK + 1, 2 * C_CV, 2 * C_CV + 1)] + [wspec, wspec],
        out_specs=(pl.BlockSpec((tn, 4 * HEAD_DIM), lambda i: (i, 0)),) * 2,
        compiler_params=_cparams(("arbitrary",)),
        name="cmp_partial_prompt",
    )(big, big, big, big, _cmp_cat(w1k), _cmp_cat(w1v))


def _cmp_partial_paged_kernel(pt_ref, *refs, n_pages):
    xk, xv = refs[:n_pages], refs[n_pages:2 * n_pages]
    wk_ref, wv_ref, ok_ref, ov_ref = refs[2 * n_pages:]
    sub = PAGE // CMP_STRIDE
    rows_per_sub = CMP_STRIDE * NSA_KV
    for xs, w_ref, o_ref in ((xk, wk_ref, ok_ref), (xv, wv_ref, ov_ref)):
        for kv in range(NSA_KV):
            acc = None
            for r in range(CMP_STRIDE):
                x = jnp.concatenate([p[pl.ds(NSA_KV * r + kv, sub, stride=rows_per_sub), :] for p in xs], axis=0)
                d = jnp.dot(x.astype(BF16), w_ref[r], preferred_element_type=F32)
                acc = d if acc is None else acc + d
            o_ref[:, kv * HEAD_DIM:(kv + 1) * HEAD_DIM] = acc[:, :HEAD_DIM]
            o_ref[:, (NSA_KV + kv) * HEAD_DIM:(NSA_KV + kv + 1) * HEAD_DIM] = acc[:, HEAD_DIM:]


def _cmp_partial_paged(cache_k, cache_v, page_table, w1k, w1v, pages_per_step):
    B, n_pages = page_table.shape
    P = pages_per_step
    sub = PAGE // CMP_STRIDE
    n_pool = cache_k.shape[0]
    vk = cache_k.reshape(n_pool * PAGE * NSA_KV, HEAD_DIM)
    vv = cache_v.reshape(n_pool * PAGE * NSA_KV, HEAD_DIM)
    pt = page_table.reshape(-1)

    def src(u):
        return pl.BlockSpec((PAGE * NSA_KV, HEAD_DIM), lambda b, c, pt_ref: (pt_ref[b * n_pages + c * P + u], 0))

    wspec = pl.BlockSpec((CMP_STRIDE, HEAD_DIM, 2 * HEAD_DIM), lambda b, c, pt_ref: (0, 0, 0))
    out = jax.ShapeDtypeStruct((B, n_pages * sub, 4 * HEAD_DIM), F32)
    ospec = pl.BlockSpec((None, P * sub, 4 * HEAD_DIM), lambda b, c, pt_ref: (b, c, 0))
    return pl.pallas_call(
        functools.partial(_cmp_partial_paged_kernel, n_pages=P),
        out_shape=(out, out),
        grid_spec=pltpu.PrefetchScalarGridSpec(
            num_scalar_prefetch=1,
            grid=(B, n_pages // P),
            in_specs=[src(u) for u in range(P)] * 2 + [wspec, wspec],
            out_specs=(ospec, ospec),
        ),
        compiler_params=_cparams(("arbitrary", "arbitrary")),
        name="cmp_partial_paged",
    )(pt, *([vk] * P), *([vv] * P), _cmp_cat(w1k), _cmp_cat(w1v))


def _cmp_mlp_kernel(pk_ref, pv_ref, b1k_ref, b1v_ref, w2k_ref, w2v_ref, ok_ref, ov_ref, *, ns):
    for p_ref, b_ref, w_ref, o_ref in ((pk_ref, b1k_ref, w2k_ref, ok_ref), (pv_ref, b1v_ref, w2v_ref, ov_ref)):
        for kv in range(NSA_KV):
            p0 = p_ref[:, kv * HEAD_DIM:(kv + 1) * HEAD_DIM]
            p1 = p_ref[:, (NSA_KV + kv) * HEAD_DIM:(NSA_KV + kv + 1) * HEAD_DIM]
            nxt = pltpu.roll(p1, ns - 1, axis=0)
            h = _silu(p0 + nxt + b_ref[...])
            o_ref[:, kv * HEAD_DIM:(kv + 1) * HEAD_DIM] = jnp.dot(h.astype(BF16), w_ref[...],
                                                                  preferred_element_type=F32).astype(BF16)


def _cmp_mlp(pk, pv, b1k, b1v, w2k, w2v):
    B, ns, _ = pk.shape
    pspec = pl.BlockSpec((None, ns, 4 * HEAD_DIM), lambda b: (b, 0, 0))
    bspec = pl.BlockSpec((1, HEAD_DIM), lambda b: (0, 0))
    wspec = pl.BlockSpec((HEAD_DIM, HEAD_DIM), lambda b: (0, 0))
    out = jax.ShapeDtypeStruct((B, ns, NSA_KV * HEAD_DIM), BF16)
    ospec = pl.BlockSpec((None, ns, NSA_KV * HEAD_DIM), lambda b: (b, 0, 0))
    return pl.pallas_call(
        functools.partial(_cmp_mlp_kernel, ns=ns),
        out_shape=(out, out),
        grid=(B,),
        in_specs=[pspec, pspec, bspec, bspec, wspec, wspec],
        out_specs=(ospec, ospec),
        compiler_params=_cparams(("arbitrary",)),
        name="cmp_mlp",
    )(pk, pv, b1k.reshape(1, -1), b1v.reshape(1, -1), w2k.astype(BF16), w2v.astype(BF16))


def _cmp_attn_kernel(q_ref, kc_ref, vc_ref, ov_ref, o_ref, sel_ref, *, tq, ns, nb, q_base):
    qi = pl.program_id(1)
    q_pos = q_base + qi * tq + lax.broadcasted_iota(jnp.int32, (tq, 1), 0)
    n_idx = lax.broadcasted_iota(jnp.int32, (1, ns), 1)
    cmp_end = n_idx * CMP_STRIDE + (CMP_BLOCK - 1)
    mask = (cmp_end <= q_pos) & (n_idx < ns - 1)
    overlap = ov_ref[...]
    blk = lax.broadcasted_iota(jnp.int32, (tq, nb), 1)
    blkf = blk.astype(F32)
    cur = q_pos >> SLC_SHIFT
    forced = (blk == 0) | (blk == cur) | (blk == cur - 1)
    valid = blk <= cur
    n_pick = SLC_TOPN - jnp.where(cur >= nb, 1, 0)
    scores = []
    for g in range(NSA_KV):
        kb = kc_ref[:, g * HEAD_DIM:(g + 1) * HEAD_DIM]
        vb = vc_ref[:, g * HEAD_DIM:(g + 1) * HEAD_DIM]
        imp = jnp.zeros((tq, ns), F32)
        for h in range(GROUP):
            c = (g * GROUP + h) * HEAD_DIM
            qh = (q_ref[:, c:c + HEAD_DIM] * SCALE).astype(BF16)
            s = lax.dot_general(qh, kb, (((1,), (1,)), ((), ())), preferred_element_type=F32)
            s = jnp.where(mask, s, NEG_BIG)
            m = jnp.max(s, axis=-1, keepdims=True)
            e = jnp.where(mask, jnp.exp(s - m), 0.0)
            p = e / jnp.maximum(jnp.sum(e, axis=-1, keepdims=True), 1e-30)
            o_ref[:, c:c + HEAD_DIM] = jnp.dot(p.astype(BF16), vb, preferred_element_type=F32)
            imp = imp + p
        hi = imp.astype(BF16)
        r1 = imp - hi.astype(F32)
        mid = r1.astype(BF16)
        lo = (r1 - mid.astype(F32)).astype(BF16)
        imp_slc = (jnp.dot(hi, overlap, preferred_element_type=F32) + jnp.dot(mid, overlap, preferred_element_type=F32)
                   + jnp.dot(lo, overlap, preferred_element_type=F32))
        scores.append(jnp.where(valid, imp_slc + FORCE_BONUS * forced.astype(F32), NEG_BIG))

    def pick(it, carry):
        out = []
        for score, chosen in carry:
            best = jnp.max(score, axis=-1, keepdims=True)
            first = jnp.min(jnp.where(score == best, blkf, float(nb)), axis=-1, keepdims=True)
            hit = blkf == first
            out.append((jnp.where(hit, -jnp.inf, score), jnp.where(hit & (it < n_pick), 1.0, chosen)))
        return tuple(out)

    picked = lax.fori_loop(0, SLC_TOPN, pick, tuple((sc, jnp.zeros((tq, nb), F32)) for sc in scores))
    for g in range(NSA_KV):
        bias = jnp.where((picked[g][1] > 0.5) & valid, 0.0, SEL_OFF).astype(BF16)
        for half in range(nb // LANES):
            sel_ref[g, half] = bias[:, half * LANES:(half + 1) * LANES]


def _cmp_attn(q_arr, q_col, B, n_qblk, tq, kc, vc, q_base, nb):
    ns = kc.shape[1]
    per = SLC_BLOCK // CMP_STRIDE
    ci = np.arange(ns)[:, None]
    cb = np.arange(nb)[None, :]
    overlap = jnp.asarray((ci >= per * cb - 1) & (ci <= per * cb + per - 1), BF16)
    kspec = pl.BlockSpec((None, ns, NSA_KV * HEAD_DIM), lambda b, i: (b, 0, 0))
    return pl.pallas_call(
        functools.partial(_cmp_attn_kernel, tq=tq, ns=ns, nb=nb, q_base=q_base),
        out_shape=(jax.ShapeDtypeStruct((B * n_qblk * tq, NSA_HEADS * HEAD_DIM), F32),
                   jax.ShapeDtypeStruct((B, NSA_KV, nb // LANES, n_qblk * tq, LANES), BF16)),
        grid=(B, n_qblk),
        in_specs=[pl.BlockSpec((tq, NSA_HEADS * HEAD_DIM), lambda b, i: (b * n_qblk + i, q_col)), kspec, kspec,
                  pl.BlockSpec((ns, nb), lambda b, i: (0, 0))],
        out_specs=(pl.BlockSpec((tq, NSA_HEADS * HEAD_DIM), lambda b, i: (b * n_qblk + i, 0)),
                   pl.BlockSpec((None, NSA_KV, nb // LANES, tq, LANES), lambda b, i: (b, 0, 0, i, 0))),
        compiler_params=_cparams(("arbitrary", "arbitrary")),
        name="cmp_attn_select",
    )(q_arr, kc, vc, overlap)


def _mix_out_kernel(x_ref, of_ref, oc_ref, os_ref, ow_ref, sm_ref, fz_ref, nz_ref, w_ref, h_ref):
    mix_f = of_ref[...] * _silu(fz_ref[...])
    sm = sm_ref[...]
    parts = []
    for h in range(NSA_HEADS):
        sl = slice(h * HEAD_DIM, (h + 1) * HEAD_DIM)
        c = FOX_HEADS + h * N_BRANCH
        parts.append(sm[:, c:c + 1] * oc_ref[:, sl] + sm[:, c + 1:c + 2] * os_ref[:, sl]
                     + sm[:, c + 2:c + 3] * ow_ref[:, sl])
    mix_n = jnp.concatenate(parts, axis=1) * _silu(nz_ref[...])
    mix = jnp.concatenate([mix_f, mix_n], axis=1).astype(BF16)
    h_ref[...] = x_ref[...] + jnp.dot(mix, w_ref[...], preferred_element_type=F32)


def _mix_out(x, o_f, o_c, o_s, o_w, big, w_out, tm):
    R, D = x.shape
    row = lambda i: (i, 0)
    wide = pl.BlockSpec((tm, 1024), row)
    return pl.pallas_call(
        _mix_out_kernel,
        out_shape=jax.ShapeDtypeStruct((R, D), F32),
        grid=(R // tm,),
        in_specs=[pl.BlockSpec((tm, D), row), wide, wide, wide, wide,
                  pl.BlockSpec((tm, LANES), lambda i: (i, C_SM * 2)),
                  pl.BlockSpec((tm, 1024), lambda i: (i, C_FZ // 4)),
                  pl.BlockSpec((tm, 1024), lambda i: (i, C_NZ // 4)),
                  pl.BlockSpec(w_out.shape, lambda i: (0, 0))],
        out_specs=pl.BlockSpec((tm, D), row),
        compiler_params=_cparams(("arbitrary",)),
        name="mix_out",
    )(x, o_f, o_c, o_s, o_w, big, big, big, w_out)


def _ple_kernel(h_ref, p_ref, g_ref, wg_ref, wp_ref, y_ref):
    h = h_ref[...]
    ms = jnp.mean(h * h, axis=-1, keepdims=True)
    hn = (h * lax.rsqrt(ms + EPS) * g_ref[...]).astype(BF16)
    gate = _sigmoid(jnp.dot(hn, wg_ref[...], preferred_element_type=F32))
    y_ref[...] = h + gate * jnp.dot(p_ref[...].astype(BF16), wp_ref[...], preferred_element_type=F32)


def _ple(h, p, ple_norm, w_gate, w_ple, tm):
    R, D = h.shape
    row = lambda i: (i, 0)
    return pl.pallas_call(
        _ple_kernel,
        out_shape=jax.ShapeDtypeStruct((R, D), F32),
        grid=(R // tm,),
        in_specs=[pl.BlockSpec((tm, D), row), pl.BlockSpec((tm, p.shape[1]), row),
                  pl.BlockSpec((1, D), lambda i: (0, 0)),
                  pl.BlockSpec(w_gate.shape, lambda i: (0, 0)),
                  pl.BlockSpec(w_ple.shape, lambda i: (0, 0))],
        out_specs=pl.BlockSpec((tm, D), row),
        compiler_params=_cparams(("arbitrary",)),
        name="ple_gate",
    )(h, p, ple_norm.reshape(1, D), w_gate, w_ple)


STK = GROUP * 4


def _decode_kernel(pt_ref, *refs, P, n_tok, has_bias, has_sel):
    q_ref = refs[0]
    k_pages, v_pages = refs[1:1 + P], refs[1 + P:1 + 2 * P]
    pos = 1 + 2 * P
    cq_ref = cs_ref = cst_ref = sel_ref = None
    if has_bias:
        cq_ref, cs_ref, cst_ref = refs[pos:pos + 3]
        pos += 3
    if has_sel:
        sel_ref = refs[pos]
        pos += 1
    kt_ref, vt_ref, o_ref, m_ref, l_ref, acc_ref = refs[pos:pos + 6]
    step = pl.program_id(1)
    n_keys = P * PAGE

    @pl.when(step == 0)
    def _():
        m_ref[...] = jnp.full(m_ref.shape, NEG_BIG, F32)
        l_ref[...] = jnp.zeros(l_ref.shape, F32)
        acc_ref[...] = jnp.zeros(acc_ref.shape, F32)

    def update(g, s, vb):
        m_prev = m_ref[g]
        m_new = jnp.maximum(m_prev, jnp.max(s, axis=-1, keepdims=True))
        alpha = jnp.exp(m_prev - m_new)
        p = jnp.exp(s - m_new)
        l_ref[g] = alpha * l_ref[g] + jnp.sum(p, axis=-1, keepdims=True)
        acc_ref[g] = alpha * acc_ref[g] + jnp.dot(p.astype(BF16), vb, preferred_element_type=F32)
        m_ref[g] = m_new

    if has_sel:
        blk = ((step * n_keys + lax.broadcasted_iota(jnp.int32, (1, n_keys), 1)) >> SLC_SHIFT) & (LANES - 1)
        onehot = (lax.broadcasted_iota(jnp.int32, (LANES, n_keys), 0) == blk).astype(BF16)

    qb = [(q_ref[g] * SCALE).astype(BF16) for g in range(NSA_KV)]
    for g in range(NSA_KV):
        kb = jnp.concatenate([r[pl.ds(g, PAGE, stride=NSA_KV), :] for r in k_pages], axis=0).astype(BF16)
        vb = jnp.concatenate([r[pl.ds(g, PAGE, stride=NSA_KV), :] for r in v_pages], axis=0).astype(BF16)
        s = lax.dot_general(qb[g], kb, (((1,), (1,)), ((), ())), preferred_element_type=F32)
        if has_bias:
            s = s + (cq_ref[g] - cs_ref[g])
        if has_sel:
            s = s + jnp.dot(sel_ref[g], onehot, preferred_element_type=F32)
        update(g, s, vb)

    @pl.when(step == pl.num_programs(1) - 1)
    def _():
        row_tok = lax.broadcasted_iota(jnp.int32, (STK, DEC_PAD), 0) & 3
        key_tok = lax.broadcasted_iota(jnp.int32, (STK, DEC_PAD), 1)
        mask = (key_tok <= row_tok) & (key_tok < n_tok)
        for g in range(NSA_KV):
            kb = kt_ref[:, g * HEAD_DIM:(g + 1) * HEAD_DIM].astype(BF16)
            vb = vt_ref[:, g * HEAD_DIM:(g + 1) * HEAD_DIM].astype(BF16)
            s = lax.dot_general(qb[g], kb, (((1,), (1,)), ((), ())), preferred_element_type=F32)
            if has_bias:
                s = s + (cq_ref[g] - cst_ref[g])
            update(g, jnp.where(mask, s, NEG_BIG), vb)
            o_ref[g] = acc_ref[g] / jnp.maximum(l_ref[g], 1e-30)


def _decode_attn(q_st, cache_k, cache_v, page_table, big, k_col, v_col, P, n_tok, *, bias=None, sel=None, name):
    B, n_pages = page_table.shape
    n_pool = cache_k.shape[0]
    k2 = cache_k.reshape(n_pool * PAGE * NSA_KV, HEAD_DIM)
    v2 = cache_v.reshape(n_pool * PAGE * NSA_KV, HEAD_DIM)
    pt = page_table.reshape(-1)
    n_keys = P * PAGE

    def page(u):
        return pl.BlockSpec((PAGE * NSA_KV, HEAD_DIM), lambda b, s, pt_ref: (pt_ref[b * n_pages + s * P + u], 0))

    in_specs = [pl.BlockSpec((None, NSA_KV, STK, HEAD_DIM), lambda b, s, pt_ref: (b, 0, 0, 0))]
    in_specs += [page(u) for u in range(P)] * 2
    args = [q_st] + [k2] * P + [v2] * P
    if bias is not None:
        in_specs += [pl.BlockSpec((None, NSA_KV, STK, 1), lambda b, s, pt_ref: (b, 0, 0, 0)),
                     pl.BlockSpec((None, NSA_KV, STK, n_keys), lambda b, s, pt_ref: (b, 0, 0, s)),
                     pl.BlockSpec((None, NSA_KV, STK, DEC_PAD), lambda b, s, pt_ref: (b, 0, 0, 0))]
        args += list(bias)
    if sel is not None:
        per_step = n_keys // SLC_BLOCK
        in_specs.append(pl.BlockSpec((None, NSA_KV, None, STK, LANES),
                                     lambda b, s, pt_ref: (b, 0, (s * per_step) // LANES, 0, 0)))
        args.append(sel)
    in_specs += [pl.BlockSpec((DEC_PAD, CH), lambda b, s, pt_ref: (b, k_col)),
                 pl.BlockSpec((DEC_PAD, CH), lambda b, s, pt_ref: (b, v_col))]
    args += [big, big]
    return pl.pallas_call(
        functools.partial(_decode_kernel, P=P, n_tok=n_tok, has_bias=bias is not None, has_sel=sel is not None),
        out_shape=jax.ShapeDtypeStruct((B, NSA_KV, STK, HEAD_DIM), F32),
        grid_spec=pltpu.PrefetchScalarGridSpec(
            num_scalar_prefetch=1,
            grid=(B, n_pages // P),
            in_specs=in_specs,
            out_specs=pl.BlockSpec((None, NSA_KV, STK, HEAD_DIM), lambda b, s, pt_ref: (b, 0, 0, 0)),
            scratch_shapes=[pltpu.VMEM((NSA_KV, STK, 1), F32), pltpu.VMEM((NSA_KV, STK, 1), F32),
                            pltpu.VMEM((NSA_KV, STK, HEAD_DIM), F32)],
        ),
        compiler_params=_cparams(("arbitrary", "arbitrary")),
        name=name,
    )(pt, *args)


def _stack_rows(a, n_tok):
    B = a.shape[0]
    x = a.shape[-1] // (NSA_KV * GROUP)
    return a[:, :n_tok].reshape(B, n_tok, NSA_KV, GROUP, x).transpose(0, 2, 3, 1, 4).reshape(B, NSA_KV, STK, x)


def _unstack_rows(o_st, n_tok):
    B = o_st.shape[0]
    o = o_st.reshape(B, NSA_KV, GROUP, n_tok, HEAD_DIM).transpose(0, 3, 1, 2, 4).reshape(B, n_tok, -1)
    return jnp.pad(o, ((0, 0), (0, DEC_PAD - n_tok), (0, 0))).reshape(B * DEC_PAD, -1)


def _pick(n, cands):
    for c in cands:
        if n % c == 0:
            return c
    raise ValueError(f"no tile in {cands} divides {n}")


def _col(big, c, width=CH):
    return big[:, c * CH:c * CH + width]


def _prompt_layer(x, p_i, prm):
    T, D = x.shape
    big = _inproj(x, jnp.arange(T, dtype=jnp.int32), prm["attn_norm"], prm["w_perm"], prm["gain"], prm["bf_pad"],
                  _pick(T, (1024, 512, 256, 128)))
    logf = big[:, C_SM * CH:C_SM * CH + FOX_HEADS]
    nr = T // LANES
    c4, _ = _cumsum(logf.reshape(1, nr, LANES, FOX_HEADS).transpose(0, 1, 3, 2),
                    jnp.zeros((1, FOX_HEADS, DEC_PAD), F32))
    c_rows = c4.transpose(0, 1, 3, 2).reshape(T, FOX_HEADS)

    tq, tk = _pick(T, (1024, 512)), 512
    qm, fa, kf, vf, ks, vs, kw, vw, nrm = _pack(big, c_rows, tk)
    tabs = _prompt_tables(T, tq, tk)
    o_f = _pflash(qm, 0, kf, vf, tabs, tq, tk, aug=fa, aug_mode="head",
                  skip=_fox_skip(tabs, nrm, c_rows, T, tq, tk), name="fox_prompt")

    pk, pv = _cmp_partial_prompt(big, T, prm["w1k"], prm["w1v"], _pick(T // CMP_STRIDE, (256, 128, 64)))
    kc, vc = _cmp_mlp(pk[None], pv[None], prm["b1k"], prm["b1v"], prm["w2k"], prm["w2v"])
    nb = -(-(T // SLC_BLOCK) // LANES) * LANES
    tqc = _pick(T, (256, 128))
    o_c, sel = _cmp_attn(big, C_NQ // 4, 1, T // tqc, tqc, kc, vc, 0, nb)
    o_s = _pflash(qm, 1, ks, vs, tabs, tq, tk, aug=sel[0], aug_mode="group", name="slc_prompt")
    o_w = _pflash(qm, 1, kw, vw, _prompt_tables(T, tq, tk, window=True), tq, tk, name="win_prompt")

    tm = _pick(T, (256, 128))
    h = _mix_out(x, o_f, o_c, o_s, o_w, big, prm["w_out"], tm)
    y = _ple(h, p_i, prm["ple_norm"], prm["w_gate"], prm["w_ple"], tm)
    n_win = min(WINDOW, T)
    kv5 = lambda c: _col(big, c).reshape(1, 1, T, 2, HEAD_DIM)
    state = (kv5(C_FK), kv5(C_FV), logf.reshape(1, 1, T, FOX_HEADS), kv5(C_CK), kv5(C_CV), kv5(C_SK), kv5(C_SV),
             kv5(C_WK)[:, :, T - n_win:], kv5(C_WV)[:, :, T - n_win:])
    return y.reshape(1, T, D), state


def _sample_layer(x, p_i, caches, page_table, prm):
    c_fk, c_fv, c_flogf, c_ck, c_cv, c_sk, c_sv, c_wk, c_wv = caches
    B, Tn, D = x.shape
    n_pages = page_table.shape[1]
    past = n_pages * PAGE
    R = B * DEC_PAD
    xp = jnp.pad(x, ((0, 0), (0, DEC_PAD - Tn), (0, 0))).reshape(R, D)
    pos = jnp.tile(past + jnp.arange(DEC_PAD, dtype=jnp.int32), B)
    big = _inproj(xp, pos, prm["attn_norm"], prm["w_perm"], prm["gain"], prm["bf_pad"], _pick(R, (256, 128, 16)))
    logf = big[:, C_SM * CH:C_SM * CH + FOX_HEADS]

    assert Tn * GROUP == STK, "decode kernel stacks 4 heads x 4 new tokens per KV group"
    P = _pick(n_pages, (16, 8, 4, 2, 1))
    lf_pages = _gather_logf_pages(c_flogf.transpose(0, 2, 1), page_table, P)
    c4, c_new = _cumsum(lf_pages, logf.reshape(B, DEC_PAD, FOX_HEADS).transpose(0, 2, 1))
    rep = lambda a: jnp.repeat(a, Tn, axis=2)
    cs_past = rep(c4.transpose(0, 2, 1, 3).reshape(B, FOX_KV, GROUP, past))
    cs_tail = rep(c_new.reshape(B, FOX_KV, GROUP, DEC_PAD))
    cq = c_new[:, :, :Tn].reshape(B, FOX_KV, STK, 1)

    q3 = lambda c: big[:, c * CH:c * CH + 1024].reshape(B, DEC_PAD, 1024)
    o_f = _unstack_rows(
        _decode_attn(_stack_rows(q3(C_FQ), Tn), c_fk, c_fv, page_table, big, C_FK, C_FV, P, Tn,
                     bias=(cq, cs_past, cs_tail), name="fox_decode"), Tn)

    pk, pv = _cmp_partial_paged(c_ck, c_cv, page_table, prm["w1k"], prm["w1v"], _pick(n_pages, (32, 16, 8, 4, 2, 1)))
    kc, vc = _cmp_mlp(pk, pv, prm["b1k"], prm["b1v"], prm["w2k"], prm["w2v"])
    nb = -(-(past // SLC_BLOCK) // LANES) * LANES
    o_c, sel = _cmp_attn(big, C_NQ // 4, B, 1, DEC_PAD, kc, vc, past, nb)
    sel_st = jnp.tile(sel[:, :, :, None, :Tn], (1, 1, 1, GROUP, 1, 1)).reshape(B, NSA_KV, nb // LANES, STK, LANES)
    o_s = _unstack_rows(
        _decode_attn(_stack_rows(q3(C_NQ), Tn), c_sk, c_sv, page_table, big, C_SK, C_SV, P, Tn,
                     sel=sel_st, name="slc_decode"), Tn)
    n_buf = c_wk.shape[1]
    wspec = [pl.BlockSpec((n_buf, HEAD_DIM), lambda b, g, s, *_: (b, g))]
    wflat = lambda c: c.reshape(B * n_buf, NSA_KV * HEAD_DIM)
    o_w = _flash(big, C_NQ // 2, DEC_PAD, 1, B, _linear_tables(1), wspec, wspec, [wflat(c_wk), wflat(c_wv)], n_buf,
                 tail=(big, C_WK * 2, big, C_WV * 2), windowed=True,
                 q_base=past, k_base=past - n_buf, tail_base=past, name="win_decode")

    tm = _pick(R, (256, 128, 16))
    pp = jnp.pad(p_i, ((0, 0), (0, DEC_PAD - Tn), (0, 0))).reshape(R, -1)
    h = _mix_out(xp, o_f, o_c, o_s, o_w, big, prm["w_out"], tm)
    y = _ple(h, pp, prm["ple_norm"], prm["w_gate"], prm["w_ple"], tm)
    y = y.reshape(B, DEC_PAD, D)[:, :Tn]
    new = lambda c: _col(big, c).reshape(B, DEC_PAD, 2, HEAD_DIM)[:, :Tn]
    wk_new, wv_new = new(C_WK), new(C_WV)
    kw = jnp.concatenate([c_wk, wk_new], axis=1)[:, -n_buf:]
    vw = jnp.concatenate([c_wv, wv_new], axis=1)[:, -n_buf:]
    state = (new(C_FK), new(C_FV), logf.reshape(B, DEC_PAD, FOX_HEADS)[:, :Tn], new(C_CK), new(C_CV), new(C_SK),
             new(C_SV), kw, vw)
    return y, tuple(s[None] for s in state)


def kernel(x_prompt, x_sample, cache_fox_k, cache_fox_v, cache_fox_logf, cache_cmp_k, cache_cmp_v, cache_slc_k,
           cache_slc_v, cache_win_k, cache_win_v, page_table, p_prompt, p_sample, attn_norm, w_in, b_forget,
           fox_q_norm, fox_k_norm, nsa_q_norm, nsa_k_norm, cmp_k_w1, cmp_k_b1, cmp_k_w2, cmp_v_w1, cmp_v_b1,
           cmp_v_w2, w_out, ple_norm, w_ple, w_ple_gate):
    assert x_prompt.shape[0] == 1 and w_in.shape[0] == 1, "one prompt sequence, one layer"
    w_perm, gain, bf_pad = _prep_inproj_params(w_in[0], b_forget[0], fox_q_norm[0], fox_k_norm[0], nsa_q_norm[0],
                                               nsa_k_norm[0])
    prm = dict(attn_norm=attn_norm[0], w_perm=w_perm, gain=gain, bf_pad=bf_pad,
               w1k=cmp_k_w1[0].astype(BF16), w1v=cmp_v_w1[0].astype(BF16), b1k=cmp_k_b1[0], b1v=cmp_v_b1[0],
               w2k=cmp_k_w2[0], w2v=cmp_v_w2[0], w_out=w_out[0].astype(BF16), ple_norm=ple_norm[0],
               w_gate=w_ple_gate[0].astype(BF16), w_ple=w_ple[0].astype(BF16))
    y_p, st_p = _prompt_layer(x_prompt[0], p_prompt[0, 0], prm)
    caches = (cache_fox_k[0], cache_fox_v[0], cache_fox_logf[0], cache_cmp_k[0], cache_cmp_v[0], cache_slc_k[0],
              cache_slc_v[0], cache_win_k[0], cache_win_v[0])
    y_s, st_s = _sample_layer(x_sample, p_sample[0], caches, page_table, prm)
    return (y_p, y_s) + tuple(st_p) + tuple(st_s)
```

```python
import functools

import numpy as np
import jax
import jax.numpy as jnp
from jax import lax
from jax.experimental import pallas as pl
from jax.experimental.pallas import tpu as pltpu

F32 = jnp.float32
BF16 = jnp.bfloat16

HEAD_DIM = 128
FOX_HEADS = 8
FOX_KV = 2
NSA_HEADS = 8
NSA_KV = 2
GROUP = 4
N_BRANCH = 3
ROPE_THETA = 500000.0
ROT_DIM = HEAD_DIM // 4
ROT_HALF = ROT_DIM // 2
CMP_STRIDE = 16
CMP_BLOCK = 2 * CMP_STRIDE
SLC_BLOCK = 64
SLC_SHIFT = 6
SLC_TOPN = 16
WINDOW = 512
EPS = 1e-6
NEG_BIG = -1e30
FORCE_BONUS = 1e4
SEL_OFF = -30000.0
PAGE = 128
SCALE = HEAD_DIM ** -0.5

SPLIT_SIZES = (1024, 256, 256, 8, 1024, 1024, 256, 256, 256, 256, 256, 256, 24, 1024)

CH = 256
C_FQ, C_NQ, C_FZ, C_NZ = 0, 4, 8, 12
C_FK, C_CK, C_SK, C_WK, C_FV, C_CV, C_SV, C_WV, C_SM = 16, 17, 18, 19, 20, 21, 22, 23, 24
N_CH = 25
NP = N_CH * CH
LANES = 128
DEC_PAD = 16

VMEM_LIMIT = 56 * 1024 * 1024


def _cparams(sem):
    return pltpu.CompilerParams(dimension_semantics=sem, vmem_limit_bytes=VMEM_LIMIT)


def _sigmoid(x):
    return 1.0 / (1.0 + jnp.exp(-x))


def _silu(x):
    return x * _sigmoid(x)


def _inproj_kernel(x_ref, g_ref, w_ref, gain_ref, bf_ref, rc_ref, rs1_ref, rs2_ref, o_ref, xn_ref):
    j = pl.program_id(1)

    @pl.when(j == 0)
    def _():
        x = x_ref[...]
        ms = jnp.mean(x * x, axis=-1, keepdims=True)
        xn_ref[...] = (x * lax.rsqrt(ms + EPS) * g_ref[...]).astype(BF16)

    y = jnp.dot(xn_ref[...], w_ref[...], preferred_element_type=F32)

    is_q_rope = (j >= C_NQ) & (j < C_FZ)
    is_k_rope = (j >= C_CK) & (j <= C_WK)
    is_norm_only = (j < C_NQ) | (j == C_FK)
    is_rope = is_q_rope | is_k_rope
    is_raw = ((j >= C_FZ) & (j < C_FK)) | ((j >= C_FV) & (j < C_SM))

    def normed(h):
        yh = y[:, h * LANES:(h + 1) * LANES]
        ms = jnp.mean(yh * yh, axis=-1, keepdims=True)
        return yh * lax.rsqrt(ms + EPS) * gain_ref[:, h * LANES:(h + 1) * LANES]

    @pl.when(is_norm_only)
    def _():
        for h in range(CH // LANES):
            o_ref[:, h * LANES:(h + 1) * LANES] = normed(h)

    @pl.when(is_rope)
    def _():
        for h in range(CH // LANES):
            yn = normed(h)
            lo = pltpu.roll(yn, LANES - ROT_HALF, axis=1)
            hi = pltpu.roll(yn, ROT_HALF, axis=1)
            o_ref[:, h * LANES:(h + 1) * LANES] = yn * rc_ref[...] + lo * rs1_ref[...] + hi * rs2_ref[...]

    @pl.when(is_raw)
    def _():
        o_ref[...] = y

    @pl.when(j == C_SM)
    def _():
        t = y[:, :LANES] + bf_ref[...]
        lane = lax.broadcasted_iota(jnp.int32, t.shape, 1)
        e = jnp.exp(-jnp.abs(t))
        logsig = jnp.minimum(t, 0.0) - jnp.log(1.0 + e)
        o_ref[:, :LANES] = jnp.where(lane < FOX_HEADS, logsig, _sigmoid(t))
        o_ref[:, LANES:] = jnp.zeros_like(t)


def _inproj(x, pos, attn_norm, w_perm, gain, bf_pad, tm):
    R, D = x.shape
    inv = ROPE_THETA ** (-(2.0 / ROT_DIM) * jnp.arange(ROT_HALF, dtype=F32))
    ang = pos.astype(F32)[:, None] * inv[None, :]
    cos, sin = jnp.cos(ang), jnp.sin(ang)
    z = jnp.zeros((R, LANES - ROT_DIM), F32)
    zh = jnp.zeros((R, ROT_HALF), F32)
    rc = jnp.concatenate([cos, cos, z + 1.0], axis=1)
    rs1 = jnp.concatenate([-sin, zh, z], axis=1)
    rs2 = jnp.concatenate([zh, sin, z], axis=1)
    row = lambda i, j: (i, 0)
    return pl.pallas_call(
        _inproj_kernel,
        out_shape=jax.ShapeDtypeStruct((R, NP), F32),
        grid=(R // tm, N_CH),
        in_specs=[
            pl.BlockSpec((tm, D), row),
            pl.BlockSpec((1, D), lambda i, j: (0, 0)),
            pl.BlockSpec((D, CH), lambda i, j: (0, j)),
            pl.BlockSpec((1, CH), lambda i, j: (0, j)),
            pl.BlockSpec((1, LANES), lambda i, j: (0, 0)),
            pl.BlockSpec((tm, LANES), row),
            pl.BlockSpec((tm, LANES), row),
            pl.BlockSpec((tm, LANES), row),
        ],
        out_specs=pl.BlockSpec((tm, CH), lambda i, j: (i, j)),
        scratch_shapes=[pltpu.VMEM((tm, D), BF16)],
        compiler_params=_cparams(("arbitrary", "arbitrary")),
        name="inproj",
    )(x, attn_norm.reshape(1, D), w_perm, gain, bf_pad, rc, rs1, rs2)


def _prep_inproj_params(w_in, b_forget, fox_q_norm, fox_k_norm, nsa_q_norm, nsa_k_norm):
    D = w_in.shape[0]
    offs = np.concatenate([[0], np.cumsum(SPLIT_SIZES)]).tolist()
    seg = [w_in[:, offs[i]:offs[i + 1]] for i in range(len(SPLIT_SIZES))]
    fq, fk, fv, flg, fz, nq, ck, cv, sk, sv, wk, wv, ng, nz = seg
    small = jnp.concatenate([flg, ng, jnp.zeros((D, CH - FOX_HEADS - NSA_HEADS * N_BRANCH), F32)], axis=1)
    w_perm = jnp.concatenate([fq, nq, fz, nz, fk, ck, sk, wk, fv, cv, sv, wv, small], axis=1).astype(BF16)
    zeros = lambda n: jnp.zeros((n,), F32)
    gain = jnp.concatenate([
        jnp.tile(fox_q_norm, FOX_HEADS), jnp.tile(nsa_q_norm, NSA_HEADS), zeros(2048),
        jnp.tile(fox_k_norm, FOX_KV), jnp.tile(nsa_k_norm[0], NSA_KV), jnp.tile(nsa_k_norm[1], NSA_KV),
        jnp.tile(nsa_k_norm[2], NSA_KV), zeros(4 * CH + CH)]).reshape(1, NP)
    bf_pad = jnp.concatenate([b_forget, zeros(LANES - FOX_HEADS)]).reshape(1, LANES)
    return w_perm, gain, bf_pad


def _gather_pages_kernel(pt_ref, *refs):
    n = len(refs) - 1
    o_ref = refs[n]
    for u in range(n):
        o_ref[u] = refs[u][...]


def _gather_logf_pages(cache_t, page_table, pages_per_step):
    B, n_pages = page_table.shape
    P = pages_per_step
    pt = page_table.reshape(-1)

    def src_map(u):
        return lambda b, c, pt_ref: (pt_ref[b * n_pages + c * P + u], 0, 0)

    return pl.pallas_call(
        _gather_pages_kernel,
        out_shape=jax.ShapeDtypeStruct((B, n_pages, FOX_HEADS, PAGE), F32),
        grid_spec=pltpu.PrefetchScalarGridSpec(
            num_scalar_prefetch=1,
            grid=(B, n_pages // P),
            in_specs=[pl.BlockSpec((None, FOX_HEADS, PAGE), src_map(u)) for u in range(P)],
            out_specs=pl.BlockSpec((None, P, FOX_HEADS, PAGE), lambda b, c, pt_ref: (b, c, 0, 0)),
        ),
        compiler_params=_cparams(("arbitrary", "arbitrary")),
        name="gather_logf_pages",
    )(pt, *([cache_t] * P))


def _cumsum_kernel(x_ref, new_ref, o_ref, onew_ref, *, nr):
    hp = lax.Precision.HIGHEST
    x = x_ref[...].reshape(nr * FOX_HEADS, LANES)
    ci = lax.broadcasted_iota(jnp.int32, (LANES, LANES), 0)
    cj = lax.broadcasted_iota(jnp.int32, (LANES, LANES), 1)
    upper = (ci <= cj).astype(F32)
    lastcol = (ci == LANES - 1).astype(F32)
    within = jnp.dot(x, upper, precision=hp, preferred_element_type=F32)
    rowtot = jnp.dot(within, lastcol, precision=hp, preferred_element_type=F32)
    n = nr * FOX_HEADS
    ri = lax.broadcasted_iota(jnp.int32, (n, n), 0)
    rj = lax.broadcasted_iota(jnp.int32, (n, n), 1)
    same_head = (ri & (FOX_HEADS - 1)) == (rj & (FOX_HEADS - 1))
    before = (same_head & (rj < ri)).astype(F32)
    prefix = jnp.dot(before, rowtot, precision=hp, preferred_element_type=F32)
    o_ref[...] = (within + prefix).reshape(nr, FOX_HEADS, LANES)
    total = (prefix + rowtot)[n - FOX_HEADS:, :]
    nw = new_ref[...]
    ti = lax.broadcasted_iota(jnp.int32, (DEC_PAD, DEC_PAD), 0)
    tj = lax.broadcasted_iota(jnp.int32, (DEC_PAD, DEC_PAD), 1)
    onew_ref[...] = total[:, :DEC_PAD] + jnp.dot(nw, (ti <= tj).astype(F32), precision=hp,
                                                  preferred_element_type=F32)


def _cumsum(x4, new):
    B, nr = x4.shape[:2]
    return pl.pallas_call(
        functools.partial(_cumsum_kernel, nr=nr),
        out_shape=(jax.ShapeDtypeStruct(x4.shape, F32), jax.ShapeDtypeStruct(new.shape, F32)),
        grid=(B,),
        in_specs=[pl.BlockSpec((None, nr, FOX_HEADS, LANES), lambda b: (b, 0, 0, 0)),
                  pl.BlockSpec((None, FOX_HEADS, DEC_PAD), lambda b: (b, 0, 0))],
        out_specs=(pl.BlockSpec((None, nr, FOX_HEADS, LANES), lambda b: (b, 0, 0, 0)),
                   pl.BlockSpec((None, FOX_HEADS, DEC_PAD), lambda b: (b, 0, 0))),
        compiler_params=_cparams(("arbitrary",)),
        name="cumsum_logf",
    )(x4, new)


def _flash_kernel(*refs, n_sp, n_src, tq, n_keys, has_bias, has_sel, has_tail, windowed,
                  q_base, k_base, tail_base):
    qi_tab, kj_tab, first_tab, last_tab = refs[:4]
    refs = refs[n_sp:]
    q_ref = refs[0]
    k_srcs = refs[1:1 + n_src]
    v_srcs = refs[1 + n_src:1 + 2 * n_src]
    pos = 1 + 2 * n_src
    cq_ref = cs_ref = sel_ref = kt_ref = vt_ref = cst_ref = None
    if has_bias:
        cq_ref, cs_ref = refs[pos], refs[pos + 1]
        pos += 2
    if has_sel:
        sel_ref = refs[pos]
        pos += 1
    if has_tail:
        kt_ref, vt_ref = refs[pos], refs[pos + 1]
        pos += 2
        if has_bias:
            cst_ref = refs[pos]
            pos += 1
    o_ref = refs[pos]
    qs_ref, m_ref, l_ref, acc_ref = refs[pos + 1:pos + 5]

    step = pl.program_id(2)
    g = pl.program_id(1)
    qi = qi_tab[step]
    kj = kj_tab[step]

    @pl.when(first_tab[step] == 1)
    def _():
        qs_ref[...] = (q_ref[...] * SCALE).astype(BF16)
        m_ref[...] = jnp.full(m_ref.shape, NEG_BIG, F32)
        l_ref[...] = jnp.zeros(l_ref.shape, F32)
        acc_ref[...] = jnp.zeros(acc_ref.shape, F32)

    q_pos = q_base + qi * tq + lax.broadcasted_iota(jnp.int32, (tq, 1), 0)

    def attend(k, v, k_pos, cs, sel_bias):
        kb = k.astype(BF16)
        vb = v.astype(BF16)
        mask = k_pos <= q_pos
        if windowed:
            mask = mask & ((q_pos - k_pos) < WINDOW)
        for h in range(GROUP):
            s = lax.dot_general(qs_ref[:, h * HEAD_DIM:(h + 1) * HEAD_DIM], kb, (((1,), (1,)), ((), ())),
                                preferred_element_type=F32)
            if cs is not None:
                hh = g * GROUP + h
                cq = jnp.sum(jnp.where(lax.broadcasted_iota(jnp.int32, cq_ref.shape, 1) == hh, cq_ref[...], 0.0),
                             axis=1, keepdims=True)
                s = s + (cq - cs[h:h + 1, :])
            if sel_bias is not None:
                s = s + sel_bias
            s = jnp.where(mask, s, NEG_BIG)
            m_prev = m_ref[h]
            m_new = jnp.maximum(m_prev, jnp.max(s, axis=-1, keepdims=True))
            alpha = jnp.exp(m_prev - m_new)
            p = jnp.exp(s - m_new)
            l_ref[h] = alpha * l_ref[h] + jnp.sum(p, axis=-1, keepdims=True)
            acc_ref[h] = alpha * acc_ref[h] + jnp.dot(p.astype(BF16), vb, preferred_element_type=F32)
            m_ref[h] = m_new

    if n_src == 1:
        k, v = k_srcs[0][...], v_srcs[0][...]
    else:
        k = jnp.concatenate([r[...] for r in k_srcs], axis=0)
        v = jnp.concatenate([r[...] for r in v_srcs], axis=0)
    k_pos = k_base + kj * n_keys + lax.broadcasted_iota(jnp.int32, (1, n_keys), 1)
    sel_bias = None
    if has_sel:
        blk = ((kj * n_keys + lax.broadcasted_iota(jnp.int32, (1, n_keys), 1)) >> SLC_SHIFT) & (LANES - 1)
        onehot = (lax.broadcasted_iota(jnp.int32, (LANES, n_keys), 0) == blk).astype(BF16)
        sel_bias = jnp.dot(sel_ref[...], onehot, preferred_element_type=F32)
    attend(k, v, k_pos, cs_ref[...] if has_bias else None, sel_bias)

    @pl.when(last_tab[step] == 1)
    def _():
        if has_tail:
            t_pos = tail_base + lax.broadcasted_iota(jnp.int32, (1, DEC_PAD), 1)
            attend(kt_ref[...], vt_ref[...], t_pos, cst_ref[...] if has_bias else None, None)
        for h in range(GROUP):
            o_ref[:, h * HEAD_DIM:(h + 1) * HEAD_DIM] = acc_ref[h] / jnp.maximum(l_ref[h], 1e-30)


def _flash(q_arr, q_col, tq, n_qblk, B, tables, k_specs, v_specs, kv_args, n_keys, *, prefetch=(),
           bias=None, sel=None, tail=None, windowed=False, q_base=0, k_base=0, tail_base=0, name="flash"):
    n_sp = len(tables) + len(prefetch)
    n_steps = tables[0].shape[0]
    n_src = len(k_specs)
    qmap = lambda b, g, s, qi, *_: (b * n_qblk + qi[s], q_col + g)
    in_specs = [pl.BlockSpec((tq, GROUP * HEAD_DIM), qmap)] + list(k_specs) + list(v_specs)
    args = [q_arr] + list(kv_args)
    if bias is not None:
        in_specs.append(pl.BlockSpec((tq, FOX_HEADS), lambda b, g, s, qi, *_: (b * n_qblk + qi[s], 0)))
        in_specs.append(pl.BlockSpec((None, None, GROUP, n_keys), lambda b, g, s, qi, kj, *_: (b, g, 0, kj[s])))
        args += [bias[0], bias[1]]
    if sel is not None:
        blocks_per_step = n_keys // SLC_BLOCK
        in_specs.append(pl.BlockSpec(
            (None, None, None, tq, LANES),
            lambda b, g, s, qi, kj, *_: (b, g, (kj[s] * blocks_per_step) // LANES, qi[s], 0)))
        args.append(sel)
    if tail is not None:
        k_arr, k_col, v_arr, v_col = tail
        in_specs.append(pl.BlockSpec((DEC_PAD, HEAD_DIM), lambda b, g, s, *_: (b, k_col + g)))
        in_specs.append(pl.BlockSpec((DEC_PAD, HEAD_DIM), lambda b, g, s, *_: (b, v_col + g)))
        args += [k_arr, v_arr]
        if bias is not None:
            in_specs.append(pl.BlockSpec((None, None, GROUP, DEC_PAD), lambda b, g, s, *_: (b, g, 0, 0)))
            args.append(bias[2])
    kern = functools.partial(
        _flash_kernel, n_sp=n_sp, n_src=n_src, tq=tq, n_keys=n_keys, has_bias=bias is not None,
        has_sel=sel is not None, has_tail=tail is not None, windowed=windowed,
        q_base=q_base, k_base=k_base, tail_base=tail_base)
    return pl.pallas_call(
        kern,
        out_shape=jax.ShapeDtypeStruct((B * n_qblk * tq, 2 * GROUP * HEAD_DIM), F32),
        grid_spec=pltpu.PrefetchScalarGridSpec(
            num_scalar_prefetch=n_sp,
            grid=(B, 2, n_steps),
            in_specs=in_specs,
            out_specs=pl.BlockSpec((tq, GROUP * HEAD_DIM), lambda b, g, s, qi, *_: (b * n_qblk + qi[s], g)),
            scratch_shapes=[pltpu.VMEM((tq, GROUP * HEAD_DIM), BF16),
                            pltpu.VMEM((GROUP, tq, 1), F32),
                            pltpu.VMEM((GROUP, tq, 1), F32),
                            pltpu.VMEM((GROUP, tq, HEAD_DIM), F32)],
        ),
        compiler_params=_cparams(("arbitrary", "arbitrary", "arbitrary")),
        name=name,
    )(*tables, *prefetch, *args)


def _causal_tables(n_blk, lookback=None):
    qi, kj, first, last = [], [], [], []
    for i in range(n_blk):
        lo = 0 if lookback is None else max(0, i - lookback)
        for j in range(lo, i + 1):
            qi.append(i)
            kj.append(j)
            first.append(int(j == lo))
            last.append(int(j == i))
    return tuple(jnp.asarray(a, jnp.int32) for a in (qi, kj, first, last))


def _linear_tables(n_steps):
    z = np.zeros((n_steps,), np.int32)
    first, last = z.copy(), z.copy()
    first[0], last[-1] = 1, 1
    return tuple(jnp.asarray(a, jnp.int32) for a in (z, np.arange(n_steps, dtype=np.int32), first, last))


LOG2E = 1.4426950408889634
AUG = 128
N_SPLIT = 3
AUG_C0 = N_SPLIT
NRM_ROWS = 16
UNDERFLOW_LOG2 = -160.0


def _split_part(x, part):
    hi = x.astype(BF16)
    r1 = x - hi.astype(F32)
    mid = r1.astype(BF16)
    lo = (r1 - mid.astype(F32)).astype(BF16)
    return jnp.where(part == 0, hi, jnp.where(part == 1, mid, lo))


def _pack_kernel(big_q, big_kv, c_ref, qm_ref, fa_ref, kf_ref, vf_ref, ks_ref, vs_ref, kw_ref, vw_ref, nrm_ref, *, tr):
    i = pl.program_id(0)
    lane = lax.broadcasted_iota(jnp.int32, (tr, AUG), 1)
    in_c = (lane >= AUG_C0) & (lane < AUG_C0 + GROUP * N_SPLIT)
    slot = jnp.zeros_like(lane)
    for j in range(1, GROUP):
        slot = slot + jnp.where(lane >= AUG_C0 + j * N_SPLIT, 1, 0)
    part = jnp.where(lane < AUG_C0, lane, lane - AUG_C0 - N_SPLIT * slot)
    c2 = c_ref[...] * LOG2E
    ones_lane = jnp.where(lane == 0, 1.0, 0.0).astype(BF16)
    zeros = jnp.zeros((tr, AUG), BF16)

    def max_sq_norm(x):
        xf = x.astype(F32)
        n2 = jnp.max(jnp.sum(xf * xf, axis=-1, keepdims=True), axis=0, keepdims=True)
        return jnp.broadcast_to(n2, (1, LANES))

    qmb = (big_q[...] * (SCALE * LOG2E)).astype(BF16)
    qm_ref[...] = qmb
    nrm_ref[...] = jnp.zeros(nrm_ref.shape, F32)
    for h in range(FOX_HEADS):
        nrm_ref[h:h + 1, :] = max_sq_norm(qmb[:, h * HEAD_DIM:(h + 1) * HEAD_DIM])
        cq = jnp.where(lane < AUG_C0, c2[:, h:h + 1], 0.0)
        own = in_c & (slot == h % GROUP)
        fa_ref[:, h * AUG:(h + 1) * AUG] = jnp.where(own, -1.0, _split_part(cq, part).astype(F32)).astype(BF16)

    tok = i * tr + lax.broadcasted_iota(jnp.int32, (tr, AUG), 0)
    blk_onehot = jnp.where(lane == ((tok >> SLC_SHIFT) & (AUG - 1)), 1.0, 0.0).astype(BF16)
    for g in range(NSA_KV):
        cs = jnp.zeros((tr, AUG), F32)
        for j in range(GROUP):
            cs = jnp.where(in_c & (slot == j), c2[:, g * GROUP + j:g * GROUP + j + 1], cs)
        k_aug = jnp.where(lane < AUG_C0, 1.0, _split_part(cs, part).astype(F32)).astype(BF16)
        col = lambda c: big_kv[:, (c - C_FK) * CH + g * HEAD_DIM:(c - C_FK) * CH + (g + 1) * HEAD_DIM].astype(BF16)
        lo, hi = g * 2 * HEAD_DIM, g * 2 * HEAD_DIM + HEAD_DIM
        for k_out, v_out, kc, vc, aug in ((kf_ref, vf_ref, C_FK, C_FV, k_aug), (ks_ref, vs_ref, C_SK, C_SV, blk_onehot),
                                          (kw_ref, vw_ref, C_WK, C_WV, zeros)):
            k_out[:, lo:hi] = col(kc)
            if kc == C_FK:
                nrm_ref[FOX_HEADS + g:FOX_HEADS + g + 1, :] = max_sq_norm(col(kc))
            k_out[:, hi:hi + AUG] = aug
            v_out[:, lo:hi] = col(vc)
            v_out[:, hi:hi + AUG] = ones_lane


def _pack(big, c_rows, tr):
    T = big.shape[0]
    row = lambda i: (i, 0)
    kv = jax.ShapeDtypeStruct((T, NSA_KV * (HEAD_DIM + AUG)), BF16)
    kvspec = pl.BlockSpec((tr, NSA_KV * (HEAD_DIM + AUG)), row)
    return pl.pallas_call(
        functools.partial(_pack_kernel, tr=tr),
        out_shape=(jax.ShapeDtypeStruct((T, 2048), BF16), jax.ShapeDtypeStruct((T, FOX_HEADS * AUG), BF16)) + (kv,) * 6
        + (jax.ShapeDtypeStruct((T // tr, NRM_ROWS, LANES), F32),),
        grid=(T // tr,),
        in_specs=[pl.BlockSpec((tr, 2048), row),
                  pl.BlockSpec((tr, 2048), lambda i: (i, C_FK // 8)),
                  pl.BlockSpec((tr, FOX_HEADS), row)],
        out_specs=(pl.BlockSpec((tr, 2048), row), pl.BlockSpec((tr, FOX_HEADS * AUG), row)) + (kvspec,) * 6
        + (pl.BlockSpec((None, NRM_ROWS, LANES), lambda i: (i, 0, 0)),),
        compiler_params=_cparams(("arbitrary",)),
        name="pack_qkv",
    )(big, big, c_rows)


def _pflash_kernel(qi_tab, kj_tab, first_tab, last_tab, kind_tab, aug_tab, skip_tab, kjd_tab, *refs, tq, tk,
                   aug_mode):
    qm_ref, k_ref, v_ref = refs[:3]
    pos = 3
    a_ref = None
    if aug_mode != "none":
        a_ref = refs[pos]
        pos += 1
    o_ref, qs_ref, m_ref, acc_ref = refs[pos:pos + 4]
    step = pl.program_id(1)
    active = skip_tab[pl.program_id(0) * pl.num_programs(1) + step] == 0

    @pl.when(first_tab[step] == 1)
    def _():
        for h in range(GROUP):
            qs_ref[h, :, :HEAD_DIM] = qm_ref[:, h * HEAD_DIM:(h + 1) * HEAD_DIM]
            if aug_mode == "head":
                qs_ref[h, :, HEAD_DIM:] = a_ref[:, h * AUG:(h + 1) * AUG]
            elif aug_mode == "none":
                qs_ref[h, :, HEAD_DIM:] = jnp.zeros((tq, AUG), BF16)
        m_ref[...] = jnp.full(m_ref.shape, NEG_BIG, F32)
        acc_ref[...] = jnp.zeros(acc_ref.shape, F32)

    if aug_mode == "group":
        @pl.when(aug_tab[step] == 1)
        def _():
            for h in range(GROUP):
                qs_ref[h, :, HEAD_DIM:] = a_ref[...]

    def attend(kind):
        kb = k_ref[...]
        vb = v_ref[...]
        if kind:
            ahead = (qi_tab[step] * tq - kj_tab[step] * tk + lax.broadcasted_iota(jnp.int32, (tq, tk), 0)
                     - lax.broadcasted_iota(jnp.int32, (tq, tk), 1))
            keep = (ahead >= 0) if kind == 1 else ((ahead >= 0) & (ahead < WINDOW))
        for h in range(GROUP):
            s = lax.dot_general(qs_ref[h], kb, (((1,), (1,)), ((), ())), preferred_element_type=F32)
            if kind:
                s = jnp.where(keep, s, NEG_BIG)
            m_prev = m_ref[h]
            m_tile = s[:, :LANES]
            for c in range(1, tk // LANES):
                m_tile = jnp.maximum(m_tile, s[:, c * LANES:(c + 1) * LANES])
            m_new = jnp.maximum(m_prev, jnp.max(m_tile, axis=-1, keepdims=True))
            p = jnp.exp2(s - jnp.tile(m_new, (1, tk // LANES))).astype(BF16)
            alpha = jnp.exp2(m_prev - m_new)
            acc_ref[h] = jnp.tile(alpha, (1, 2)) * acc_ref[h] + jnp.dot(p, vb, preferred_element_type=F32)
            m_ref[h] = m_new

    for kind in range(3):
        pl.when((kind_tab[step] == kind) & active)(functools.partial(attend, kind))

    @pl.when(last_tab[step] == 1)
    def _():
        for h in range(GROUP):
            acc = acc_ref[h]
            o_ref[:, h * HEAD_DIM:(h + 1) * HEAD_DIM] = acc[:, :HEAD_DIM] / jnp.maximum(
                acc[:, HEAD_DIM:HEAD_DIM + 1], 1e-30)


def _pflash(qm, q_col, k_arr, v_arr, tables, tq, tk, *, aug=None, aug_mode="none", skip=None, name):
    T = qm.shape[0]
    n_steps = tables[0].shape[0]
    kvw = HEAD_DIM + AUG
    in_specs = [pl.BlockSpec((tq, GROUP * HEAD_DIM), lambda g, s, qi, *_: (qi[s], q_col * 2 + g)),
                pl.BlockSpec((tk, kvw), lambda g, s, *tabs: (tabs[7][g * n_steps + s], g)),
                pl.BlockSpec((tk, kvw), lambda g, s, *tabs: (tabs[7][g * n_steps + s], g))]
    args = [qm, k_arr, v_arr]
    if aug_mode == "head":
        in_specs.append(pl.BlockSpec((tq, GROUP * AUG), lambda g, s, qi, *_: (qi[s], g)))
        args.append(aug)
    elif aug_mode == "group":
        per_step = tk // SLC_BLOCK
        in_specs.append(pl.BlockSpec((None, None, tq, AUG),
                                     lambda g, s, qi, kj, *_: (g, (kj[s] * per_step) // AUG, qi[s], 0)))
        args.append(aug)
    return pl.pallas_call(
        functools.partial(_pflash_kernel, tq=tq, tk=tk, aug_mode=aug_mode),
        out_shape=jax.ShapeDtypeStruct((T, 2 * GROUP * HEAD_DIM), F32),
        grid_spec=pltpu.PrefetchScalarGridSpec(
            num_scalar_prefetch=8,
            grid=(2, n_steps),
            in_specs=in_specs,
            out_specs=pl.BlockSpec((tq, GROUP * HEAD_DIM), lambda g, s, qi, *_: (qi[s], g)),
            scratch_shapes=[pltpu.VMEM((GROUP, tq, kvw), BF16),
                            pltpu.VMEM((GROUP, tq, LANES), F32),
                            pltpu.VMEM((GROUP, tq, kvw), F32)],
        ),
        compiler_params=_cparams(("arbitrary", "arbitrary")),
        name=name,
    )(*(jnp.asarray(t) for t in tables), *(skip if skip is not None else _no_skip(tables)), *args)


def _fox_skip(tables, nrm, c_rows, T, tq, tk):
    qi, kj, last = np.asarray(tables[0]), np.asarray(tables[1]), np.asarray(tables[3])
    nq, nt, r = T // tq, T // tk, tq // tk
    qn = jnp.sqrt(nrm[:, :FOX_HEADS, 0].reshape(nq, r, FOX_HEADS).max(axis=1))
    kn = jnp.sqrt(nrm[:, FOX_HEADS:FOX_HEADS + FOX_KV, 0])
    kn_own = kn.reshape(nq, r, FOX_KV).max(axis=1)
    c2 = c_rows * LOG2E
    c_first_q, c_last_k = c2[::tq], c2[tk - 1::tk]
    rep = lambda a: jnp.repeat(a, GROUP, axis=-1)
    logit_max = qn[qi] * rep(kn[kj]) + (c_first_q[qi] - c_last_k[kj])
    max_lb = -qn[qi] * rep(kn_own[qi])
    before = jnp.asarray((kj + 1) * tk - 1 < qi * tq)[:, None]
    dead = (before & (logit_max < max_lb + UNDERFLOW_LOG2)).reshape(-1, FOX_KV, GROUP).all(axis=-1)
    last_of_block = np.zeros_like(kj)
    nxt = 0
    for s in range(len(kj) - 1, -1, -1):
        if last[s] == 1:
            nxt = kj[s]
        last_of_block[s] = nxt
    skip = dead.T.astype(jnp.int32).reshape(-1)
    kjd = jnp.where(dead.T, jnp.asarray(last_of_block)[None, :], jnp.asarray(kj)[None, :]).astype(jnp.int32)
    return skip, kjd.reshape(-1)


def _no_skip(tables):
    n = tables[1].shape[0]
    return jnp.zeros((2 * n,), jnp.int32), jnp.asarray(np.tile(tables[1], 2))


def _prompt_tables(T, tq, tk, window=False):
    per_half = AUG * SLC_BLOCK // tk
    rows = []
    for i in range(T // tq):
        q_lo, q_hi = i * tq, (i + 1) * tq - 1
        lo = max(0, q_lo - WINDOW + 1) // tk if window else 0
        hi = q_hi // tk
        for j in range(lo, hi + 1):
            crosses = (j + 1) * tk - 1 > q_lo
            kind = 2 if window else int(crosses)
            rows.append((i, j, int(j == lo), int(j == hi), kind, int(j == lo or j % per_half == 0)))
    return tuple(np.asarray(a, np.int32) for a in zip(*rows))


def _cmp_cat(w1):
    return jnp.concatenate([w1[:CMP_STRIDE], w1[CMP_STRIDE:]], axis=-1)


def _cmp_partial_prompt_kernel(xk0_ref, xk1_ref, xv0_ref, xv1_ref, wk_ref, wv_ref, ok_ref, ov_ref, *, tn):
    for x_refs, w_ref, o_ref in (((xk0_ref, xk1_ref), wk_ref, ok_ref), ((xv0_ref, xv1_ref), wv_ref, ov_ref)):
        for kv in range(NSA_KV):
            acc = None
            for r in range(CMP_STRIDE):
                x = x_refs[kv][pl.ds(r, tn, stride=CMP_STRIDE), :]
                d = jnp.dot(x.astype(BF16), w_ref[r], preferred_element_type=F32)
                acc = d if acc is None else acc + d
            o_ref[:, kv * HEAD_DIM:(kv + 1) * HEAD_DIM] = acc[:, :HEAD_DIM]
            o_ref[:, (NSA_KV + kv) * HEAD_DIM:(NSA_KV + kv + 1) * HEAD_DIM] = acc[:, HEAD_DIM:]


def _cmp_partial_prompt(big, T, w1k, w1v, tn):
    ns = T // CMP_STRIDE
    wspec = pl.BlockSpec((CMP_STRIDE, HEAD_DIM, 2 * HEAD_DIM), lambda i: (0, 0, 0))
    out = jax.ShapeDtypeStruct((ns, 4 * HEAD_DIM), F32)
    return pl.pallas_call(
        functools.partial(_cmp_partial_prompt_kernel, tn=tn),
        out_shape=(out, out),
        grid=(ns // tn,),
        in_specs=[pl.BlockSpec((tn * CMP_STRIDE, HEAD_DIM), (lambda c: (lambda i: (i, c)))(c))
                  for c in (2 * C_CK, 2 * C_CK + 1, 2 * C_CV, 2 * C_CV + 1)] + [wspec, wspec],
        out_specs=(pl.BlockSpec((tn, 4 * HEAD_DIM), lambda i: (i, 0)),) * 2,
        compiler_params=_cparams(("arbitrary",)),
        name="cmp_partial_prompt",
    )(big, big, big, big, _cmp_cat(w1k), _cmp_cat(w1v))


def _cmp_partial_paged_kernel(pt_ref, *refs, n_pages):
    xk, xv = refs[:n_pages], refs[n_pages:2 * n_pages]
    wk_ref, wv_ref, ok_ref, ov_ref = refs[2 * n_pages:]
    sub = PAGE // CMP_STRIDE
    rows_per_sub = CMP_STRIDE * NSA_KV
    for xs, w_ref, o_ref in ((xk, wk_ref, ok_ref), (xv, wv_ref, ov_ref)):
        for kv in range(NSA_KV):
            acc = None
            for r in range(CMP_STRIDE):
                x = jnp.concatenate([p[pl.ds(NSA_KV * r + kv, sub, stride=rows_per_sub), :] for p in xs], axis=0)
                d = jnp.dot(x.astype(BF16), w_ref[r], preferred_element_type=F32)
                acc = d if acc is None else acc + d
            o_ref[:, kv * HEAD_DIM:(kv + 1) * HEAD_DIM] = acc[:, :HEAD_DIM]
            o_ref[:, (NSA_KV + kv) * HEAD_DIM:(NSA_KV + kv + 1) * HEAD_DIM] = acc[:, HEAD_DIM:]


def _cmp_partial_paged(cache_k, cache_v, page_table, w1k, w1v, pages_per_step):
    B, n_pages = page_table.shape
    P = pages_per_step
    sub = PAGE // CMP_STRIDE
    n_pool = cache_k.shape[0]
    vk = cache_k.reshape(n_pool * PAGE * NSA_KV, HEAD_DIM)
    vv = cache_v.reshape(n_pool * PAGE * NSA_KV, HEAD_DIM)
    pt = page_table.reshape(-1)

    def src(u):
        return pl.BlockSpec((PAGE * NSA_KV, HEAD_DIM), lambda b, c, pt_ref: (pt_ref[b * n_pages + c * P + u], 0))

    wspec = pl.BlockSpec((CMP_STRIDE, HEAD_DIM, 2 * HEAD_DIM), lambda b, c, pt_ref: (0, 0, 0))
    out = jax.ShapeDtypeStruct((B, n_pages * sub, 4 * HEAD_DIM), F32)
    ospec = pl.BlockSpec((None, P * sub, 4 * HEAD_DIM), lambda b, c, pt_ref: (b, c, 0))
    return pl.pallas_call(
        functools.partial(_cmp_partial_paged_kernel, n_pages=P),
        out_shape=(out, out),
        grid_spec=pltpu.PrefetchScalarGridSpec(
            num_scalar_prefetch=1,
            grid=(B, n_pages // P),
            in_specs=[src(u) for u in range(P)] * 2 + [wspec, wspec],
            out_specs=(ospec, ospec),
        ),
        compiler_params=_cparams(("arbitrary", "arbitrary")),
        name="cmp_partial_paged",
    )(pt, *([vk] * P), *([vv] * P), _cmp_cat(w1k), _cmp_cat(w1v))


def _cmp_mlp_kernel(pk_ref, pv_ref, b1k_ref, b1v_ref, w2k_ref, w2v_ref, ok_ref, ov_ref, *, ns):
    for p_ref, b_ref, w_ref, o_ref in ((pk_ref, b1k_ref, w2k_ref, ok_ref), (pv_ref, b1v_ref, w2v_ref, ov_ref)):
        for kv in range(NSA_KV):
            p0 = p_ref[:, kv * HEAD_DIM:(kv + 1) * HEAD_DIM]
            p1 = p_ref[:, (NSA_KV + kv) * HEAD_DIM:(NSA_KV + kv + 1) * HEAD_DIM]
            nxt = pltpu.roll(p1, ns - 1, axis=0)
            h = _silu(p0 + nxt + b_ref[...])
            o_ref[:, kv * HEAD_DIM:(kv + 1) * HEAD_DIM] = jnp.dot(h.astype(BF16), w_ref[...],
                                                                  preferred_element_type=F32).astype(BF16)


def _cmp_mlp(pk, pv, b1k, b1v, w2k, w2v):
    B, ns, _ = pk.shape
    pspec = pl.BlockSpec((None, ns, 4 * HEAD_DIM), lambda b: (b, 0, 0))
    bspec = pl.BlockSpec((1, HEAD_DIM), lambda b: (0, 0))
    wspec = pl.BlockSpec((HEAD_DIM, HEAD_DIM), lambda b: (0, 0))
    out = jax.ShapeDtypeStruct((B, ns, NSA_KV * HEAD_DIM), BF16)
    ospec = pl.BlockSpec((None, ns, NSA_KV * HEAD_DIM), lambda b: (b, 0, 0))
    return pl.pallas_call(
        functools.partial(_cmp_mlp_kernel, ns=ns),
        out_shape=(out, out),
        grid=(B,),
        in_specs=[pspec, pspec, bspec, bspec, wspec, wspec],
        out_specs=(ospec, ospec),
        compiler_params=_cparams(("arbitrary",)),
        name="cmp_mlp",
    )(pk, pv, b1k.reshape(1, -1), b1v.reshape(1, -1), w2k.astype(BF16), w2v.astype(BF16))


def _cmp_attn_kernel(q_ref, kc_ref, vc_ref, ov_ref, o_ref, sel_ref, score_ref, *, tq, ns, nb, q_base, widths):
    qi = pl.program_id(1)
    q_pos = q_base + qi * tq + lax.broadcasted_iota(jnp.int32, (tq, 1), 0)
    n_idx = lax.broadcasted_iota(jnp.int32, (1, ns), 1)
    cmp_end = n_idx * CMP_STRIDE + (CMP_BLOCK - 1)
    mask = (cmp_end <= q_pos) & (n_idx < ns - 1)
    overlap = ov_ref[...]
    blk = lax.broadcasted_iota(jnp.int32, (tq, nb), 1)
    blkf = blk.astype(F32)
    cur = q_pos >> SLC_SHIFT
    forced = (blk == 0) | (blk == cur) | (blk == cur - 1)
    valid = blk <= cur
    n_pick = SLC_TOPN - jnp.where(cur >= nb, 1, 0)

    def attend(w):
        for g in range(NSA_KV):
            kb = kc_ref[:w, g * HEAD_DIM:(g + 1) * HEAD_DIM]
            vb = vc_ref[:w, g * HEAD_DIM:(g + 1) * HEAD_DIM]
            imp = jnp.zeros((tq, w), F32)
            for h in range(GROUP):
                c = (g * GROUP + h) * HEAD_DIM
                qh = (q_ref[:, c:c + HEAD_DIM] * (SCALE * LOG2E)).astype(BF16)
                s = lax.dot_general(qh, kb, (((1,), (1,)), ((), ())), preferred_element_type=F32)
                s = jnp.where(mask[:, :w], s, NEG_BIG)
                m = jnp.max(s, axis=-1, keepdims=True)
                e = jnp.exp2(s - m)
                any_visible = jnp.where(m > 0.5 * NEG_BIG, 1.0, 0.0)
                p = e * (any_visible / jnp.maximum(jnp.sum(e, axis=-1, keepdims=True), 1e-30))
                o_ref[:, c:c + HEAD_DIM] = jnp.dot(p.astype(BF16), vb, preferred_element_type=F32)
                imp = imp + p
            hi = imp.astype(BF16)
            r1 = imp - hi.astype(F32)
            mid = r1.astype(BF16)
            lo = (r1 - mid.astype(F32)).astype(BF16)
            ov = overlap[:w]
            imp_slc = (jnp.dot(hi, ov, preferred_element_type=F32) + jnp.dot(mid, ov, preferred_element_type=F32)
                       + jnp.dot(lo, ov, preferred_element_type=F32))
            score_ref[g] = jnp.where(valid, imp_slc + FORCE_BONUS * forced.astype(F32), NEG_BIG)

    n_vis = (q_base + (qi + 1) * tq - CMP_BLOCK) // CMP_STRIDE + 1
    lo_w = -(2 ** 30)
    for w in widths:
        pl.when((n_vis > lo_w) & ((n_vis <= w) | (w == ns)))(functools.partial(attend, w))
        lo_w = w
    scores = [score_ref[g] for g in range(NSA_KV)]

    def pick(it, carry):
        out = []
        for score, chosen in carry:
            best = jnp.max(score, axis=-1, keepdims=True)
            first = jnp.min(jnp.where(score == best, blkf, float(nb)), axis=-1, keepdims=True)
            hit = blkf == first
            out.append((jnp.where(hit, -jnp.inf, score), jnp.where(hit & (it < n_pick), 1.0, chosen)))
        return tuple(out)

    picked = lax.fori_loop(0, SLC_TOPN, pick, tuple((sc, jnp.zeros((tq, nb), F32)) for sc in scores))
    for g in range(NSA_KV):
        bias = jnp.where((picked[g][1] > 0.5) & valid, 0.0, SEL_OFF).astype(BF16)
        for half in range(nb // LANES):
            sel_ref[g, half] = bias[:, half * LANES:(half + 1) * LANES]


def _cmp_attn(q_arr, q_col, B, n_qblk, tq, kc, vc, q_base, nb):
    ns = kc.shape[1]
    per = SLC_BLOCK // CMP_STRIDE
    ci = np.arange(ns)[:, None]
    cb = np.arange(nb)[None, :]
    overlap = jnp.asarray((ci >= per * cb - 1) & (ci <= per * cb + per - 1), BF16)
    kspec = pl.BlockSpec((None, ns, NSA_KV * HEAD_DIM), lambda b, i: (b, 0, 0))
    quarter = ns // 4
    widths = (quarter, 2 * quarter, 3 * quarter, ns) if (n_qblk > 1 and quarter % LANES == 0) else (ns,)
    return pl.pallas_call(
        functools.partial(_cmp_attn_kernel, tq=tq, ns=ns, nb=nb, q_base=q_base, widths=widths),
        out_shape=(jax.ShapeDtypeStruct((B * n_qblk * tq, NSA_HEADS * HEAD_DIM), F32),
                   jax.ShapeDtypeStruct((B, NSA_KV, nb // LANES, n_qblk * tq, LANES), BF16)),
        grid=(B, n_qblk),
        in_specs=[pl.BlockSpec((tq, NSA_HEADS * HEAD_DIM), lambda b, i: (b * n_qblk + i, q_col)), kspec, kspec,
                  pl.BlockSpec((ns, nb), lambda b, i: (0, 0))],
        out_specs=(pl.BlockSpec((tq, NSA_HEADS * HEAD_DIM), lambda b, i: (b * n_qblk + i, 0)),
                   pl.BlockSpec((None, NSA_KV, nb // LANES, tq, LANES), lambda b, i: (b, 0, 0, i, 0))),
        scratch_shapes=[pltpu.VMEM((NSA_KV, tq, nb), F32)],
        compiler_params=_cparams(("arbitrary", "arbitrary")),
        name="cmp_attn_select",
    )(q_arr, kc, vc, overlap)


def _mix_out_kernel(x_ref, of_ref, oc_ref, os_ref, ow_ref, sm_ref, fz_ref, nz_ref, w_ref, h_ref):
    mix_f = of_ref[...] * _silu(fz_ref[...])
    sm = sm_ref[...]
    parts = []
    for h in range(NSA_HEADS):
        sl = slice(h * HEAD_DIM, (h + 1) * HEAD_DIM)
        c = FOX_HEADS + h * N_BRANCH
        parts.append(sm[:, c:c + 1] * oc_ref[:, sl] + sm[:, c + 1:c + 2] * os_ref[:, sl]
                     + sm[:, c + 2:c + 3] * ow_ref[:, sl])
    mix_n = jnp.concatenate(parts, axis=1) * _silu(nz_ref[...])
    mix = jnp.concatenate([mix_f, mix_n], axis=1).astype(BF16)
    h_ref[...] = x_ref[...] + jnp.dot(mix, w_ref[...], preferred_element_type=F32)


def _mix_out(x, o_f, o_c, o_s, o_w, big, w_out, tm):
    R, D = x.shape
    row = lambda i: (i, 0)
    wide = pl.BlockSpec((tm, 1024), row)
    return pl.pallas_call(
        _mix_out_kernel,
        out_shape=jax.ShapeDtypeStruct((R, D), F32),
        grid=(R // tm,),
        in_specs=[pl.BlockSpec((tm, D), row), wide, wide, wide, wide,
                  pl.BlockSpec((tm, LANES), lambda i: (i, C_SM * 2)),
                  pl.BlockSpec((tm, 1024), lambda i: (i, C_FZ // 4)),
                  pl.BlockSpec((tm, 1024), lambda i: (i, C_NZ // 4)),
                  pl.BlockSpec(w_out.shape, lambda i: (0, 0))],
        out_specs=pl.BlockSpec((tm, D), row),
        compiler_params=_cparams(("arbitrary",)),
        name="mix_out",
    )(x, o_f, o_c, o_s, o_w, big, big, big, w_out)


def _ple_kernel(h_ref, p_ref, g_ref, wg_ref, wp_ref, y_ref):
    h = h_ref[...]
    ms = jnp.mean(h * h, axis=-1, keepdims=True)
    hn = (h * lax.rsqrt(ms + EPS) * g_ref[...]).astype(BF16)
    gate = _sigmoid(jnp.dot(hn, wg_ref[...], preferred_element_type=F32))
    y_ref[...] = h + gate * jnp.dot(p_ref[...].astype(BF16), wp_ref[...], preferred_element_type=F32)


def _ple(h, p, ple_norm, w_gate, w_ple, tm):
    R, D = h.shape
    row = lambda i: (i, 0)
    return pl.pallas_call(
        _ple_kernel,
        out_shape=jax.ShapeDtypeStruct((R, D), F32),
        grid=(R // tm,),
        in_specs=[pl.BlockSpec((tm, D), row), pl.BlockSpec((tm, p.shape[1]), row),
                  pl.BlockSpec((1, D), lambda i: (0, 0)),
                  pl.BlockSpec(w_gate.shape, lambda i: (0, 0)),
                  pl.BlockSpec(w_ple.shape, lambda i: (0, 0))],
        out_specs=pl.BlockSpec((tm, D), row),
        compiler_params=_cparams(("arbitrary",)),
        name="ple_gate",
    )(h, p, ple_norm.reshape(1, D), w_gate, w_ple)


STK = GROUP * 4
STK_SHIFT = 4


def _decode_kernel(pt_ref, *refs, P, n_tok, has_bias, has_sel):
    q_ref = refs[0]
    k_pages, v_pages = refs[1:1 + P], refs[1 + P:1 + 2 * P]
    pos = 1 + 2 * P
    cq_ref = cs_ref = cst_ref = sel_ref = None
    if has_bias:
        cq_ref, cs_ref, cst_ref = refs[pos:pos + 3]
        pos += 3
    if has_sel:
        sel_ref = refs[pos]
        pos += 1
    il_ref, kt_ref, vt_ref, o_ref, m_ref, l_ref, acc_ref = refs[pos:pos + 7]
    step = pl.program_id(1)
    n_keys = P * PAGE
    n_rows = n_keys * NSA_KV
    rows = NSA_KV * STK
    row_group = lax.broadcasted_iota(jnp.int32, (rows, 1), 0) >> STK_SHIFT

    @pl.when(step == 0)
    def _():
        m_ref[...] = jnp.full(m_ref.shape, NEG_BIG, F32)
        l_ref[...] = jnp.zeros(l_ref.shape, F32)
        acc_ref[...] = jnp.zeros(acc_ref.shape, F32)

    def update(s, pv):
        m_prev = m_ref[...]
        m_new = jnp.maximum(m_prev, jnp.max(s, axis=-1, keepdims=True))
        alpha = jnp.exp(m_prev - m_new)
        p = jnp.exp(s - m_new)
        l_ref[...] = alpha * l_ref[...] + jnp.sum(p, axis=-1, keepdims=True)
        acc_ref[...] = alpha * acc_ref[...] + pv(p.astype(BF16))
        m_ref[...] = m_new

    qb = (q_ref[...] * SCALE).astype(BF16)
    kb = jnp.concatenate([r[...] for r in k_pages], axis=0).astype(BF16)
    vb = jnp.concatenate([r[...] for r in v_pages], axis=0).astype(BF16)
    s = lax.dot_general(qb, kb, (((1,), (1,)), ((), ())), preferred_element_type=F32)
    col = lax.broadcasted_iota(jnp.int32, (1, n_rows), 1)
    if has_bias:
        spread = il_ref[...]
        parts = []
        for u in range(P):
            c = cs_ref[:, u * PAGE:(u + 1) * PAGE]
            hi = c.astype(BF16)
            r1 = c - hi.astype(F32)
            mid = r1.astype(BF16)
            lo = (r1 - mid.astype(F32)).astype(BF16)
            parts.append(jnp.dot(hi, spread, preferred_element_type=F32) + jnp.dot(mid, spread, preferred_element_type=F32)
                         + jnp.dot(lo, spread, preferred_element_type=F32))
        s = s + (cq_ref[...] - jnp.concatenate(parts, axis=1))
    if has_sel:
        blk = ((step * n_keys + (col >> 1)) >> SLC_SHIFT) & (LANES - 1)
        onehot = (lax.broadcasted_iota(jnp.int32, (LANES, n_rows), 0) == blk).astype(BF16)
        s = s + jnp.dot(sel_ref[...], onehot, preferred_element_type=F32)
    s = jnp.where((col & 1) == row_group, s, NEG_BIG)
    update(s, lambda p: jnp.dot(p, vb, preferred_element_type=F32))

    @pl.when(step == pl.num_programs(1) - 1)
    def _():
        row_tok = lax.broadcasted_iota(jnp.int32, (rows, DEC_PAD), 0) & 3
        key_tok = lax.broadcasted_iota(jnp.int32, (rows, DEC_PAD), 1)
        mask = (key_tok <= row_tok) & (key_tok < n_tok)
        by_group = lambda a0, a1: jnp.where(row_group == 0, a0, a1)
        kt = [kt_ref[:, g * HEAD_DIM:(g + 1) * HEAD_DIM].astype(BF16) for g in range(NSA_KV)]
        vt = [vt_ref[:, g * HEAD_DIM:(g + 1) * HEAD_DIM].astype(BF16) for g in range(NSA_KV)]
        st = by_group(*[lax.dot_general(qb, k, (((1,), (1,)), ((), ())), preferred_element_type=F32) for k in kt])
        if has_bias:
            st = st + (cq_ref[...] - cst_ref[...])
        update(jnp.where(mask, st, NEG_BIG),
               lambda p: by_group(*[jnp.dot(p, v, preferred_element_type=F32) for v in vt]))
        o_ref[...] = acc_ref[...] / jnp.maximum(l_ref[...], 1e-30)


def _decode_attn(q_st, cache_k, cache_v, page_table, big, k_col, v_col, P, n_tok, *, bias=None, sel=None, name):
    B, n_pages = page_table.shape
    n_pool = cache_k.shape[0]
    rows = NSA_KV * STK
    k2 = cache_k.reshape(n_pool * PAGE * NSA_KV, HEAD_DIM)
    v2 = cache_v.reshape(n_pool * PAGE * NSA_KV, HEAD_DIM)
    pt = page_table.reshape(-1)
    n_keys = P * PAGE
    whole = lambda b, s, pt_ref: (b, 0, 0)

    def page(u):
        return pl.BlockSpec((PAGE * NSA_KV, HEAD_DIM), lambda b, s, pt_ref: (pt_ref[b * n_pages + s * P + u], 0))

    in_specs = [pl.BlockSpec((None, rows, HEAD_DIM), whole)]
    in_specs += [page(u) for u in range(P)] * 2
    args = [q_st.reshape(B, rows, HEAD_DIM)] + [k2] * P + [v2] * P
    if bias is not None:
        cq, cs, cs_tail = bias
        in_specs += [pl.BlockSpec((None, rows, 1), whole),
                     pl.BlockSpec((None, rows, n_keys), lambda b, s, pt_ref: (b, 0, s)),
                     pl.BlockSpec((None, rows, DEC_PAD), whole)]
        args += [cq.reshape(B, rows, 1), cs.reshape(B, rows, -1), cs_tail.reshape(B, rows, DEC_PAD)]
    if sel is not None:
        per_step = n_keys // SLC_BLOCK
        n_half = sel.shape[2]
        in_specs.append(pl.BlockSpec((None, None, rows, LANES), lambda b, s, pt_ref: (b, (s * per_step) // LANES, 0, 0)))
        args.append(sel.transpose(0, 2, 1, 3, 4).reshape(B, n_half, rows, LANES))
    spread = jnp.asarray(np.arange(PAGE)[:, None] == (np.arange(PAGE * NSA_KV)[None, :] >> 1), BF16)
    in_specs += [pl.BlockSpec((PAGE, PAGE * NSA_KV), lambda b, s, pt_ref: (0, 0)),
                 pl.BlockSpec((DEC_PAD, CH), lambda b, s, pt_ref: (b, k_col)),
                 pl.BlockSpec((DEC_PAD, CH), lambda b, s, pt_ref: (b, v_col))]
    args += [spread, big, big]
    out = pl.pallas_call(
        functools.partial(_decode_kernel, P=P, n_tok=n_tok, has_bias=bias is not None, has_sel=sel is not None),
        out_shape=jax.ShapeDtypeStruct((B, rows, HEAD_DIM), F32),
        grid_spec=pltpu.PrefetchScalarGridSpec(
            num_scalar_prefetch=1,
            grid=(B, n_pages // P),
            in_specs=in_specs,
            out_specs=pl.BlockSpec((None, rows, HEAD_DIM), whole),
            scratch_shapes=[pltpu.VMEM((rows, 1), F32), pltpu.VMEM((rows, 1), F32), pltpu.VMEM((rows, HEAD_DIM), F32)],
        ),
        compiler_params=_cparams(("arbitrary", "arbitrary")),
        name=name,
    )(pt, *args)
    return out.reshape(B, NSA_KV, STK, HEAD_DIM)


def _stack_rows(a, n_tok):
    B = a.shape[0]
    x = a.shape[-1] // (NSA_KV * GROUP)
    return a[:, :n_tok].reshape(B, n_tok, NSA_KV, GROUP, x).transpose(0, 2, 3, 1, 4).reshape(B, NSA_KV, STK, x)


def _unstack_rows(o_st, n_tok):
    B = o_st.shape[0]
    o = o_st.reshape(B, NSA_KV, GROUP, n_tok, HEAD_DIM).transpose(0, 3, 1, 2, 4).reshape(B, n_tok, -1)
    return jnp.pad(o, ((0, 0), (0, DEC_PAD - n_tok), (0, 0))).reshape(B * DEC_PAD, -1)


def _pick(n, cands):
    for c in cands:
        if n % c == 0:
            return c
    raise ValueError(f"no tile in {cands} divides {n}")


def _col(big, c, width=CH):
    return big[:, c * CH:c * CH + width]


def _prompt_layer(x, p_i, prm):
    T, D = x.shape
    big = _inproj(x, jnp.arange(T, dtype=jnp.int32), prm["attn_norm"], prm["w_perm"], prm["gain"], prm["bf_pad"],
                  _pick(T, (1024, 512, 256, 128)))
    logf = big[:, C_SM * CH:C_SM * CH + FOX_HEADS]
    nr = T // LANES
    c4, _ = _cumsum(logf.reshape(1, nr, LANES, FOX_HEADS).transpose(0, 1, 3, 2),
                    jnp.zeros((1, FOX_HEADS, DEC_PAD), F32))
    c_rows = c4.transpose(0, 1, 3, 2).reshape(T, FOX_HEADS)

    tq, tk = _pick(T, (1024, 512)), 512
    qm, fa, kf, vf, ks, vs, kw, vw, nrm = _pack(big, c_rows, tk)
    tabs = _prompt_tables(T, tq, tk)
    o_f = _pflash(qm, 0, kf, vf, tabs, tq, tk, aug=fa, aug_mode="head",
                  skip=_fox_skip(tabs, nrm, c_rows, T, tq, tk), name="fox_prompt")

    pk, pv = _cmp_partial_prompt(big, T, prm["w1k"], prm["w1v"], _pick(T // CMP_STRIDE, (256, 128, 64)))
    kc, vc = _cmp_mlp(pk[None], pv[None], prm["b1k"], prm["b1v"], prm["w2k"], prm["w2v"])
    nb = -(-(T // SLC_BLOCK) // LANES) * LANES
    tqc = _pick(T, (256, 128))
    o_c, sel = _cmp_attn(big, C_NQ // 4, 1, T // tqc, tqc, kc, vc, 0, nb)
    o_s = _pflash(qm, 1, ks, vs, tabs, tq, tk, aug=sel[0], aug_mode="group", name="slc_prompt")
    o_w = _pflash(qm, 1, kw, vw, _prompt_tables(T, tk, tk, window=True), tk, tk, name="win_prompt")

    tm = _pick(T, (256, 128))
    h = _mix_out(x, o_f, o_c, o_s, o_w, big, prm["w_out"], tm)
    y = _ple(h, p_i, prm["ple_norm"], prm["w_gate"], prm["w_ple"], tm)
    n_win = min(WINDOW, T)
    kv5 = lambda c: _col(big, c).reshape(1, 1, T, 2, HEAD_DIM)
    state = (kv5(C_FK), kv5(C_FV), logf.reshape(1, 1, T, FOX_HEADS), kv5(C_CK), kv5(C_CV), kv5(C_SK), kv5(C_SV),
             kv5(C_WK)[:, :, T - n_win:], kv5(C_WV)[:, :, T - n_win:])
    return y.reshape(1, T, D), state


def _sample_layer(x, p_i, caches, page_table, prm):
    c_fk, c_fv, c_flogf, c_ck, c_cv, c_sk, c_sv, c_wk, c_wv = caches
    B, Tn, D = x.shape
    n_pages = page_table.shape[1]
    past = n_pages * PAGE
    R = B * DEC_PAD
    xp = jnp.pad(x, ((0, 0), (0, DEC_PAD - Tn), (0, 0))).reshape(R, D)
    pos = jnp.tile(past + jnp.arange(DEC_PAD, dtype=jnp.int32), B)
    big = _inproj(xp, pos, prm["attn_norm"], prm["w_perm"], prm["gain"], prm["bf_pad"], _pick(R, (256, 128, 16)))
    logf = big[:, C_SM * CH:C_SM * CH + FOX_HEADS]

    assert Tn * GROUP == STK, "decode kernel stacks 4 heads x 4 new tokens per KV group"
    P = _pick(n_pages, (16, 8, 4, 2, 1))
    lf_pages = _gather_logf_pages(c_flogf.transpose(0, 2, 1), page_table, _pick(n_pages, (32, 16, 8, 4, 2, 1)))
    c4, c_new = _cumsum(lf_pages, logf.reshape(B, DEC_PAD, FOX_HEADS).transpose(0, 2, 1))
    rep = lambda a: jnp.repeat(a, Tn, axis=2)
    cs_past = rep(c4.transpose(0, 2, 1, 3).reshape(B, FOX_KV, GROUP, past))
    cs_tail = rep(c_new.reshape(B, FOX_KV, GROUP, DEC_PAD))
    cq = c_new[:, :, :Tn].reshape(B, FOX_KV, STK, 1)

    q3 = lambda c: big[:, c * CH:c * CH + 1024].reshape(B, DEC_PAD, 1024)
    o_f = _unstack_rows(
        _decode_attn(_stack_rows(q3(C_FQ), Tn), c_fk, c_fv, page_table, big, C_FK, C_FV, P, Tn,
                     bias=(cq, cs_past, cs_tail), name="fox_decode"), Tn)

    pk, pv = _cmp_partial_paged(c_ck, c_cv, page_table, prm["w1k"], prm["w1v"], _pick(n_pages, (32, 16, 8, 4, 2, 1)))
    kc, vc = _cmp_mlp(pk, pv, prm["b1k"], prm["b1v"], prm["w2k"], prm["w2v"])
    nb = -(-(past // SLC_BLOCK) // LANES) * LANES
    o_c, sel = _cmp_attn(big, C_NQ // 4, B, 1, DEC_PAD, kc, vc, past, nb)
    sel_st = jnp.tile(sel[:, :, :, None, :Tn], (1, 1, 1, GROUP, 1, 1)).reshape(B, NSA_KV, nb // LANES, STK, LANES)
    o_s = _unstack_rows(
        _decode_attn(_stack_rows(q3(C_NQ), Tn), c_sk, c_sv, page_table, big, C_SK, C_SV, P, Tn,
                     sel=sel_st, name="slc_decode"), Tn)
    n_buf = c_wk.shape[1]
    wspec = [pl.BlockSpec((n_buf, HEAD_DIM), lambda b, g, s, *_: (b, g))]
    wflat = lambda c: c.reshape(B * n_buf, NSA_KV * HEAD_DIM)
    o_w = _flash(big, C_NQ // 2, DEC_PAD, 1, B, _linear_tables(1), wspec, wspec, [wflat(c_wk), wflat(c_wv)], n_buf,
                 tail=(big, C_WK * 2, big, C_WV * 2), windowed=True,
                 q_base=past, k_base=past - n_buf, tail_base=past, name="win_decode")

    tm = _pick(R, (256, 128, 16))
    pp = jnp.pad(p_i, ((0, 0), (0, DEC_PAD - Tn), (0, 0))).reshape(R, -1)
    h = _mix_out(xp, o_f, o_c, o_s, o_w, big, prm["w_out"], tm)
    y = _ple(h, pp, prm["ple_norm"], prm["w_gate"], prm["w_ple"], tm)
    y = y.reshape(B, DEC_PAD, D)[:, :Tn]
    new = lambda c: _col(big, c).reshape(B, DEC_PAD, 2, HEAD_DIM)[:, :Tn]
    wk_new, wv_new = new(C_WK), new(C_WV)
    kw = jnp.concatenate([c_wk, wk_new], axis=1)[:, -n_buf:]
    vw = jnp.concatenate([c_wv, wv_new], axis=1)[:, -n_buf:]
    state = (new(C_FK), new(C_FV), logf.reshape(B, DEC_PAD, FOX_HEADS)[:, :Tn], new(C_CK), new(C_CV), new(C_SK),
             new(C_SV), kw, vw)
    return y, tuple(s[None] for s in state)


def kernel(x_prompt, x_sample, cache_fox_k, cache_fox_v, cache_fox_logf, cache_cmp_k, cache_cmp_v, cache_slc_k,
           cache_slc_v, cache_win_k, cache_win_v, page_table, p_prompt, p_sample, attn_norm, w_in, b_forget,
           fox_q_norm, fox_k_norm, nsa_q_norm, nsa_k_norm, cmp_k_w1, cmp_k_b1, cmp_k_w2, cmp_v_w1, cmp_v_b1,
           cmp_v_w2, w_out, ple_norm, w_ple, w_ple_gate):
    assert x_prompt.shape[0] == 1 and w_in.shape[0] == 1, "one prompt sequence, one layer"
    w_perm, gain, bf_pad = _prep_inproj_params(w_in[0], b_forget[0], fox_q_norm[0], fox_k_norm[0], nsa_q_norm[0],
                                               nsa_k_norm[0])
    prm = dict(attn_norm=attn_norm[0], w_perm=w_perm, gain=gain, bf_pad=bf_pad,
               w1k=cmp_k_w1[0].astype(BF16), w1v=cmp_v_w1[0].astype(BF16), b1k=cmp_k_b1[0], b1v=cmp_v_b1[0],
               w2k=cmp_k_w2[0], w2v=cmp_v_w2[0], w_out=w_out[0].astype(BF16), ple_norm=ple_norm[0],
               w_gate=w_ple_gate[0].astype(BF16), w_ple=w_ple[0].astype(BF16))
    y_p, st_p = _prompt_layer(x_prompt[0], p_prompt[0, 0], prm)
    caches = (cache_fox_k[0], cache_fox_v[0], cache_fox_logf[0], cache_cmp_k[0], cache_cmp_v[0], cache_slc_k[0],
              cache_slc_v[0], cache_win_k[0], cache_win_v[0])
    y_s, st_s = _sample_layer(x_sample, p_sample[0], caches, page_table, prm)
    return (y_p, y_s) + tuple(st_p) + tuple(st_s)
```

```python
import functools

import numpy as np
import jax
import jax.numpy as jnp
from jax import lax
from jax.experimental import pallas as pl
from jax.experimental.pallas import tpu as pltpu

F32 = jnp.float32
BF16 = jnp.bfloat16

HEAD_DIM = 128
FOX_HEADS = 8
FOX_KV = 2
NSA_HEADS = 8
NSA_KV = 2
GROUP = 4
N_BRANCH = 3
ROPE_THETA = 500000.0
ROT_DIM = HEAD_DIM // 4
ROT_HALF = ROT_DIM // 2
CMP_STRIDE = 16
CMP_BLOCK = 2 * CMP_STRIDE
SLC_BLOCK = 64
SLC_SHIFT = 6
SLC_TOPN = 16
WINDOW = 512
EPS = 1e-6
NEG_BIG = -1e30
FORCE_BONUS = 1e4
SEL_OFF = -30000.0
PAGE = 128
SCALE = HEAD_DIM ** -0.5

SPLIT_SIZES = (1024, 256, 256, 8, 1024, 1024, 256, 256, 256, 256, 256, 256, 24, 1024)

CH = 256
C_FQ, C_NQ, C_FZ, C_NZ = 0, 4, 8, 12
C_FK, C_CK, C_SK, C_WK, C_FV, C_CV, C_SV, C_WV, C_SM = 16, 17, 18, 19, 20, 21, 22, 23, 24
N_CH = 25
NP = N_CH * CH
LANES = 128
DEC_PAD = 16

VMEM_LIMIT = 56 * 1024 * 1024


def _cparams(sem):
    return pltpu.CompilerParams(dimension_semantics=sem, vmem_limit_bytes=VMEM_LIMIT)


def _sigmoid(x):
    return 1.0 / (1.0 + jnp.exp(-x))


def _silu(x):
    return x * _sigmoid(x)


def _inproj_kernel(x_ref, g_ref, w_ref, gain_ref, bf_ref, rc_ref, rs1_ref, rs2_ref, o_ref, xn_ref, y_ref):
    j = pl.program_id(1)
    c = j - 1
    cur, prev = j & 1, c & 1

    def matmul():
        y_ref[cur] = jnp.dot(xn_ref[...], w_ref[...], preferred_element_type=F32)

    @pl.when(j == 0)
    def _():
        x = x_ref[...]
        ms = jnp.mean(x * x, axis=-1, keepdims=True)
        xn_ref[...] = (x * lax.rsqrt(ms + EPS) * g_ref[...]).astype(BF16)
        matmul()

    is_q_rope = (c >= C_NQ) & (c < C_FZ)
    is_k_rope = (c >= C_CK) & (c <= C_WK)
    is_norm_only = ((c >= 0) & (c < C_NQ)) | (c == C_FK)
    is_rope = is_q_rope | is_k_rope
    is_raw = ((c >= C_FZ) & (c < C_FK)) | ((c >= C_FV) & (c < C_SM))

    def normed(h):
        yh = y_ref[prev, :, h * LANES:(h + 1) * LANES]
        ms = jnp.mean(yh * yh, axis=-1, keepdims=True)
        return yh * lax.rsqrt(ms + EPS) * gain_ref[:, h * LANES:(h + 1) * LANES]

    @pl.when(is_norm_only)
    def _():
        for h in range(CH // LANES):
            o_ref[:, h * LANES:(h + 1) * LANES] = normed(h)
        matmul()

    @pl.when(is_rope)
    def _():
        for h in range(CH // LANES):
            yn = normed(h)
            lo = pltpu.roll(yn, LANES - ROT_HALF, axis=1)
            hi = pltpu.roll(yn, ROT_HALF, axis=1)
            o_ref[:, h * LANES:(h + 1) * LANES] = yn * rc_ref[...] + lo * rs1_ref[...] + hi * rs2_ref[...]
        matmul()

    @pl.when(is_raw)
    def _():
        o_ref[...] = y_ref[prev]
        matmul()

    @pl.when(c == C_SM)
    def _():
        t = y_ref[prev, :, :LANES] + bf_ref[...]
        lane = lax.broadcasted_iota(jnp.int32, t.shape, 1)
        e = jnp.exp(-jnp.abs(t))
        logsig = jnp.minimum(t, 0.0) - jnp.log(1.0 + e)
        o_ref[:, :LANES] = jnp.where(lane < FOX_HEADS, logsig, _sigmoid(t))
        o_ref[:, LANES:] = jnp.zeros_like(t)


def _inproj(x, pos, attn_norm, w_perm, gain, bf_pad, tm):
    R, D = x.shape
    inv = ROPE_THETA ** (-(2.0 / ROT_DIM) * jnp.arange(ROT_HALF, dtype=F32))
    ang = pos.astype(F32)[:, None] * inv[None, :]
    cos, sin = jnp.cos(ang), jnp.sin(ang)
    z = jnp.zeros((R, LANES - ROT_DIM), F32)
    zh = jnp.zeros((R, ROT_HALF), F32)
    rc = jnp.concatenate([cos, cos, z + 1.0], axis=1)
    rs1 = jnp.concatenate([-sin, zh, z], axis=1)
    rs2 = jnp.concatenate([zh, sin, z], axis=1)
    row = lambda i, j: (i, 0)
    return pl.pallas_call(
        _inproj_kernel,
        out_shape=jax.ShapeDtypeStruct((R, NP), F32),
        grid=(R // tm, N_CH + 1),
        in_specs=[
            pl.BlockSpec((tm, D), row),
            pl.BlockSpec((1, D), lambda i, j: (0, 0)),
            pl.BlockSpec((D, CH), lambda i, j: (0, jnp.minimum(j, N_CH - 1))),
            pl.BlockSpec((1, CH), lambda i, j: (0, jnp.maximum(j - 1, 0))),
            pl.BlockSpec((1, LANES), lambda i, j: (0, 0)),
            pl.BlockSpec((tm, LANES), row),
            pl.BlockSpec((tm, LANES), row),
            pl.BlockSpec((tm, LANES), row),
        ],
        out_specs=pl.BlockSpec((tm, CH), lambda i, j: (i, jnp.maximum(j - 1, 0))),
        scratch_shapes=[pltpu.VMEM((tm, D), BF16), pltpu.VMEM((2, tm, CH), F32)],
        compiler_params=_cparams(("arbitrary", "arbitrary")),
        name="inproj",
    )(x, attn_norm.reshape(1, D), w_perm, gain, bf_pad, rc, rs1, rs2)


def _prep_inproj_params(w_in, b_forget, fox_q_norm, fox_k_norm, nsa_q_norm, nsa_k_norm):
    D = w_in.shape[0]
    offs = np.concatenate([[0], np.cumsum(SPLIT_SIZES)]).tolist()
    seg = [w_in[:, offs[i]:offs[i + 1]] for i in range(len(SPLIT_SIZES))]
    fq, fk, fv, flg, fz, nq, ck, cv, sk, sv, wk, wv, ng, nz = seg
    small = jnp.concatenate([flg, ng, jnp.zeros((D, CH - FOX_HEADS - NSA_HEADS * N_BRANCH), F32)], axis=1)
    w_perm = jnp.concatenate([fq, nq, fz, nz, fk, ck, sk, wk, fv, cv, sv, wv, small], axis=1).astype(BF16)
    zeros = lambda n: jnp.zeros((n,), F32)
    gain = jnp.concatenate([
        jnp.tile(fox_q_norm, FOX_HEADS), jnp.tile(nsa_q_norm, NSA_HEADS), zeros(2048),
        jnp.tile(fox_k_norm, FOX_KV), jnp.tile(nsa_k_norm[0], NSA_KV), jnp.tile(nsa_k_norm[1], NSA_KV),
        jnp.tile(nsa_k_norm[2], NSA_KV), zeros(4 * CH + CH)]).reshape(1, NP)
    bf_pad = jnp.concatenate([b_forget, zeros(LANES - FOX_HEADS)]).reshape(1, LANES)
    return w_perm, gain, bf_pad


def _gather_pages_kernel(pt_ref, *refs):
    n = len(refs) - 1
    o_ref = refs[n]
    for u in range(n):
        o_ref[u] = refs[u][...]


def _gather_logf_pages(cache_t, page_table, pages_per_step):
    B, n_pages = page_table.shape
    P = pages_per_step
    pt = page_table.reshape(-1)

    def src_map(u):
        return lambda b, c, pt_ref: (pt_ref[b * n_pages + c * P + u], 0, 0)

    return pl.pallas_call(
        _gather_pages_kernel,
        out_shape=jax.ShapeDtypeStruct((B, n_pages, FOX_HEADS, PAGE), F32),
        grid_spec=pltpu.PrefetchScalarGridSpec(
            num_scalar_prefetch=1,
            grid=(B, n_pages // P),
            in_specs=[pl.BlockSpec((None, FOX_HEADS, PAGE), src_map(u)) for u in range(P)],
            out_specs=pl.BlockSpec((None, P, FOX_HEADS, PAGE), lambda b, c, pt_ref: (b, c, 0, 0)),
        ),
        compiler_params=_cparams(("arbitrary", "arbitrary")),
        name="gather_logf_pages",
    )(pt, *([cache_t] * P))


def _cumsum_kernel(x_ref, new_ref, before_ref, o_ref, onew_ref, *, nr):
    hp = lax.Precision.HIGHEST
    x = x_ref[...].reshape(nr * FOX_HEADS, LANES)
    ci = lax.broadcasted_iota(jnp.int32, (LANES, LANES), 0)
    cj = lax.broadcasted_iota(jnp.int32, (LANES, LANES), 1)
    upper = (ci <= cj).astype(F32)
    lastcol = (ci == LANES - 1).astype(F32)
    within = jnp.dot(x, upper, precision=hp, preferred_element_type=F32)
    rowtot = jnp.dot(within, lastcol, precision=hp, preferred_element_type=F32)
    n = nr * FOX_HEADS
    before = before_ref[...]
    hi = rowtot.astype(BF16)
    r1 = rowtot - hi.astype(F32)
    mid = r1.astype(BF16)
    lo = (r1 - mid.astype(F32)).astype(BF16)
    prefix = (jnp.dot(before, hi, preferred_element_type=F32) + jnp.dot(before, mid, preferred_element_type=F32)
              + jnp.dot(before, lo, preferred_element_type=F32))
    o_ref[...] = (within + prefix).reshape(nr, FOX_HEADS, LANES)
    total = (prefix + rowtot)[n - FOX_HEADS:, :]
    nw = new_ref[...]
    ti = lax.broadcasted_iota(jnp.int32, (DEC_PAD, DEC_PAD), 0)
    tj = lax.broadcasted_iota(jnp.int32, (DEC_PAD, DEC_PAD), 1)
    onew_ref[...] = total[:, :DEC_PAD] + jnp.dot(nw, (ti <= tj).astype(F32), precision=hp,
                                                  preferred_element_type=F32)


def _cumsum(x4, new):
    B, nr = x4.shape[:2]
    n = nr * FOX_HEADS
    r = np.arange(n)
    before = jnp.asarray((r[:, None] % FOX_HEADS == r[None, :] % FOX_HEADS) & (r[None, :] < r[:, None]), BF16)
    return pl.pallas_call(
        functools.partial(_cumsum_kernel, nr=nr),
        out_shape=(jax.ShapeDtypeStruct(x4.shape, F32), jax.ShapeDtypeStruct(new.shape, F32)),
        grid=(B,),
        in_specs=[pl.BlockSpec((None, nr, FOX_HEADS, LANES), lambda b: (b, 0, 0, 0)),
                  pl.BlockSpec((None, FOX_HEADS, DEC_PAD), lambda b: (b, 0, 0)),
                  pl.BlockSpec((n, n), lambda b: (0, 0))],
        out_specs=(pl.BlockSpec((None, nr, FOX_HEADS, LANES), lambda b: (b, 0, 0, 0)),
                   pl.BlockSpec((None, FOX_HEADS, DEC_PAD), lambda b: (b, 0, 0))),
        compiler_params=_cparams(("arbitrary",)),
        name="cumsum_logf",
    )(x4, new, before)


def _flash_kernel(*refs, n_sp, n_src, tq, n_keys, has_bias, has_sel, has_tail, windowed,
                  q_base, k_base, tail_base):
    qi_tab, kj_tab, first_tab, last_tab = refs[:4]
    refs = refs[n_sp:]
    q_ref = refs[0]
    k_srcs = refs[1:1 + n_src]
    v_srcs = refs[1 + n_src:1 + 2 * n_src]
    pos = 1 + 2 * n_src
    cq_ref = cs_ref = sel_ref = kt_ref = vt_ref = cst_ref = None
    if has_bias:
        cq_ref, cs_ref = refs[pos], refs[pos + 1]
        pos += 2
    if has_sel:
        sel_ref = refs[pos]
        pos += 1
    if has_tail:
        kt_ref, vt_ref = refs[pos], refs[pos + 1]
        pos += 2
        if has_bias:
            cst_ref = refs[pos]
            pos += 1
    o_ref = refs[pos]
    qs_ref, m_ref, l_ref, acc_ref = refs[pos + 1:pos + 5]

    step = pl.program_id(2)
    g = pl.program_id(1)
    qi = qi_tab[step]
    kj = kj_tab[step]

    @pl.when(first_tab[step] == 1)
    def _():
        qs_ref[...] = (q_ref[...] * SCALE).astype(BF16)
        m_ref[...] = jnp.full(m_ref.shape, NEG_BIG, F32)
        l_ref[...] = jnp.zeros(l_ref.shape, F32)
        acc_ref[...] = jnp.zeros(acc_ref.shape, F32)

    q_pos = q_base + qi * tq + lax.broadcasted_iota(jnp.int32, (tq, 1), 0)

    def attend(k, v, k_pos, cs, sel_bias):
        kb = k.astype(BF16)
        vb = v.astype(BF16)
        mask = k_pos <= q_pos
        if windowed:
            mask = mask & ((q_pos - k_pos) < WINDOW)
        for h in range(GROUP):
            s = lax.dot_general(qs_ref[:, h * HEAD_DIM:(h + 1) * HEAD_DIM], kb, (((1,), (1,)), ((), ())),
                                preferred_element_type=F32)
            if cs is not None:
                hh = g * GROUP + h
                cq = jnp.sum(jnp.where(lax.broadcasted_iota(jnp.int32, cq_ref.shape, 1) == hh, cq_ref[...], 0.0),
                             axis=1, keepdims=True)
                s = s + (cq - cs[h:h + 1, :])
            if sel_bias is not None:
                s = s + sel_bias
            s = jnp.where(mask, s, NEG_BIG)
            m_prev = m_ref[h]
            m_new = jnp.maximum(m_prev, jnp.max(s, axis=-1, keepdims=True))
            alpha = jnp.exp(m_prev - m_new)
            p = jnp.exp(s - m_new)
            l_ref[h] = alpha * l_ref[h] + jnp.sum(p, axis=-1, keepdims=True)
            acc_ref[h] = alpha * acc_ref[h] + jnp.dot(p.astype(BF16), vb, preferred_element_type=F32)
            m_ref[h] = m_new

    if n_src == 1:
        k, v = k_srcs[0][...], v_srcs[0][...]
    else:
        k = jnp.concatenate([r[...] for r in k_srcs], axis=0)
        v = jnp.concatenate([r[...] for r in v_srcs], axis=0)
    k_pos = k_base + kj * n_keys + lax.broadcasted_iota(jnp.int32, (1, n_keys), 1)
    sel_bias = None
    if has_sel:
        blk = ((kj * n_keys + lax.broadcasted_iota(jnp.int32, (1, n_keys), 1)) >> SLC_SHIFT) & (LANES - 1)
        onehot = (lax.broadcasted_iota(jnp.int32, (LANES, n_keys), 0) == blk).astype(BF16)
        sel_bias = jnp.dot(sel_ref[...], onehot, preferred_element_type=F32)
    attend(k, v, k_pos, cs_ref[...] if has_bias else None, sel_bias)

    @pl.when(last_tab[step] == 1)
    def _():
        if has_tail:
            t_pos = tail_base + lax.broadcasted_iota(jnp.int32, (1, DEC_PAD), 1)
            attend(kt_ref[...], vt_ref[...], t_pos, cst_ref[...] if has_bias else None, None)
        for h in range(GROUP):
            o_ref[:, h * HEAD_DIM:(h + 1) * HEAD_DIM] = acc_ref[h] / jnp.maximum(l_ref[h], 1e-30)


def _flash(q_arr, q_col, tq, n_qblk, B, tables, k_specs, v_specs, kv_args, n_keys, *, prefetch=(),
           bias=None, sel=None, tail=None, windowed=False, q_base=0, k_base=0, tail_base=0, name="flash"):
    n_sp = len(tables) + len(prefetch)
    n_steps = tables[0].shape[0]
    n_src = len(k_specs)
    qmap = lambda b, g, s, qi, *_: (b * n_qblk + qi[s], q_col + g)
    in_specs = [pl.BlockSpec((tq, GROUP * HEAD_DIM), qmap)] + list(k_specs) + list(v_specs)
    args = [q_arr] + list(kv_args)
    if bias is not None:
        in_specs.append(pl.BlockSpec((tq, FOX_HEADS), lambda b, g, s, qi, *_: (b * n_qblk + qi[s], 0)))
        in_specs.append(pl.BlockSpec((None, None, GROUP, n_keys), lambda b, g, s, qi, kj, *_: (b, g, 0, kj[s])))
        args += [bias[0], bias[1]]
    if sel is not None:
        blocks_per_step = n_keys // SLC_BLOCK
        in_specs.append(pl.BlockSpec(
            (None, None, None, tq, LANES),
            lambda b, g, s, qi, kj, *_: (b, g, (kj[s] * blocks_per_step) // LANES, qi[s], 0)))
        args.append(sel)
    if tail is not None:
        k_arr, k_col, v_arr, v_col = tail
        in_specs.append(pl.BlockSpec((DEC_PAD, HEAD_DIM), lambda b, g, s, *_: (b, k_col + g)))
        in_specs.append(pl.BlockSpec((DEC_PAD, HEAD_DIM), lambda b, g, s, *_: (b, v_col + g)))
        args += [k_arr, v_arr]
        if bias is not None:
            in_specs.append(pl.BlockSpec((None, None, GROUP, DEC_PAD), lambda b, g, s, *_: (b, g, 0, 0)))
            args.append(bias[2])
    kern = functools.partial(
        _flash_kernel, n_sp=n_sp, n_src=n_src, tq=tq, n_keys=n_keys, has_bias=bias is not None,
        has_sel=sel is not None, has_tail=tail is not None, windowed=windowed,
        q_base=q_base, k_base=k_base, tail_base=tail_base)
    return pl.pallas_call(
        kern,
        out_shape=jax.ShapeDtypeStruct((B * n_qblk * tq, 2 * GROUP * HEAD_DIM), F32),
        grid_spec=pltpu.PrefetchScalarGridSpec(
            num_scalar_prefetch=n_sp,
            grid=(B, 2, n_steps),
            in_specs=in_specs,
            out_specs=pl.BlockSpec((tq, GROUP * HEAD_DIM), lambda b, g, s, qi, *_: (b * n_qblk + qi[s], g)),
            scratch_shapes=[pltpu.VMEM((tq, GROUP * HEAD_DIM), BF16),
                            pltpu.VMEM((GROUP, tq, 1), F32),
                            pltpu.VMEM((GROUP, tq, 1), F32),
                            pltpu.VMEM((GROUP, tq, HEAD_DIM), F32)],
        ),
        compiler_params=_cparams(("arbitrary", "arbitrary", "arbitrary")),
        name=name,
    )(*tables, *prefetch, *args)


def _causal_tables(n_blk, lookback=None):
    qi, kj, first, last = [], [], [], []
    for i in range(n_blk):
        lo = 0 if lookback is None else max(0, i - lookback)
        for j in range(lo, i + 1):
            qi.append(i)
            kj.append(j)
            first.append(int(j == lo))
            last.append(int(j == i))
    return tuple(jnp.asarray(a, jnp.int32) for a in (qi, kj, first, last))


def _linear_tables(n_steps):
    z = np.zeros((n_steps,), np.int32)
    first, last = z.copy(), z.copy()
    first[0], last[-1] = 1, 1
    return tuple(jnp.asarray(a, jnp.int32) for a in (z, np.arange(n_steps, dtype=np.int32), first, last))


LOG2E = 1.4426950408889634
AUG = 128
N_SPLIT = 3
AUG_C0 = N_SPLIT
NRM_ROWS = 16
UNDERFLOW_LOG2 = -160.0


def _split_part(x, part):
    hi = x.astype(BF16)
    r1 = x - hi.astype(F32)
    mid = r1.astype(BF16)
    lo = (r1 - mid.astype(F32)).astype(BF16)
    return jnp.where(part == 0, hi, jnp.where(part == 1, mid, lo))


def _pack_kernel(big_q, big_kv, c_ref, qm_ref, fa_ref, kf_ref, vf_ref, ks_ref, vs_ref, kw_ref, vw_ref, nrm_ref, *, tr):
    i = pl.program_id(0)
    lane = lax.broadcasted_iota(jnp.int32, (tr, AUG), 1)
    in_c = (lane >= AUG_C0) & (lane < AUG_C0 + GROUP * N_SPLIT)
    slot = jnp.zeros_like(lane)
    for j in range(1, GROUP):
        slot = slot + jnp.where(lane >= AUG_C0 + j * N_SPLIT, 1, 0)
    part = jnp.where(lane < AUG_C0, lane, lane - AUG_C0 - N_SPLIT * slot)
    c2 = c_ref[...] * LOG2E
    ones_lane = jnp.where(lane == 0, 1.0, 0.0).astype(BF16)
    zeros = jnp.zeros((tr, AUG), BF16)

    def max_sq_norm(x):
        xf = x.astype(F32)
        n2 = jnp.max(jnp.sum(xf * xf, axis=-1, keepdims=True), axis=0, keepdims=True)
        return jnp.broadcast_to(n2, (1, LANES))

    qmb = (big_q[...] * (SCALE * LOG2E)).astype(BF16)
    qm_ref[...] = qmb
    nrm_ref[...] = jnp.zeros(nrm_ref.shape, F32)
    for h in range(FOX_HEADS):
        nrm_ref[h:h + 1, :] = max_sq_norm(qmb[:, h * HEAD_DIM:(h + 1) * HEAD_DIM])
        cq = jnp.where(lane < AUG_C0, c2[:, h:h + 1], 0.0)
        own = in_c & (slot == h % GROUP)
        fa_ref[:, h * AUG:(h + 1) * AUG] = jnp.where(own, -1.0, _split_part(cq, part).astype(F32)).astype(BF16)

    tok = i * tr + lax.broadcasted_iota(jnp.int32, (tr, AUG), 0)
    blk_onehot = jnp.where(lane == ((tok >> SLC_SHIFT) & (AUG - 1)), 1.0, 0.0).astype(BF16)
    for g in range(NSA_KV):
        cs = jnp.zeros((tr, AUG), F32)
        for j in range(GROUP):
            cs = jnp.where(in_c & (slot == j), c2[:, g * GROUP + j:g * GROUP + j + 1], cs)
        k_aug = jnp.where(lane < AUG_C0, 1.0, _split_part(cs, part).astype(F32)).astype(BF16)
        col = lambda c: big_kv[:, (c - C_FK) * CH + g * HEAD_DIM:(c - C_FK) * CH + (g + 1) * HEAD_DIM].astype(BF16)
        lo, hi = g * 2 * HEAD_DIM, g * 2 * HEAD_DIM + HEAD_DIM
        for k_out, v_out, kc, vc, aug in ((kf_ref, vf_ref, C_FK, C_FV, k_aug), (ks_ref, vs_ref, C_SK, C_SV, blk_onehot),
                                          (kw_ref, vw_ref, C_WK, C_WV, zeros)):
            k_out[:, lo:hi] = col(kc)
            if kc == C_FK:
                nrm_ref[FOX_HEADS + g:FOX_HEADS + g + 1, :] = max_sq_norm(col(kc))
            k_out[:, hi:hi + AUG] = aug
            v_out[:, lo:hi] = col(vc)
            v_out[:, hi:hi + AUG] = ones_lane


def _pack(big, c_rows, tr):
    T = big.shape[0]
    row = lambda i: (i, 0)
    kv = jax.ShapeDtypeStruct((T, NSA_KV * (HEAD_DIM + AUG)), BF16)
    kvspec = pl.BlockSpec((tr, NSA_KV * (HEAD_DIM + AUG)), row)
    return pl.pallas_call(
        functools.partial(_pack_kernel, tr=tr),
        out_shape=(jax.ShapeDtypeStruct((T, 2048), BF16), jax.ShapeDtypeStruct((T, FOX_HEADS * AUG), BF16)) + (kv,) * 6
        + (jax.ShapeDtypeStruct((T // tr, NRM_ROWS, LANES), F32),),
        grid=(T // tr,),
        in_specs=[pl.BlockSpec((tr, 2048), row),
                  pl.BlockSpec((tr, 2048), lambda i: (i, C_FK // 8)),
                  pl.BlockSpec((tr, FOX_HEADS), row)],
        out_specs=(pl.BlockSpec((tr, 2048), row), pl.BlockSpec((tr, FOX_HEADS * AUG), row)) + (kvspec,) * 6
        + (pl.BlockSpec((None, NRM_ROWS, LANES), lambda i: (i, 0, 0)),),
        compiler_params=_cparams(("arbitrary",)),
        name="pack_qkv",
    )(big, big, c_rows)


def _pflash_kernel(qi_tab, kj_tab, first_tab, last_tab, kind_tab, aug_tab, skip_tab, kjd_tab, *refs, tq, tk,
                   aug_mode):
    qm_ref, k_ref, v_ref = refs[:3]
    pos = 3
    a_ref = None
    if aug_mode != "none":
        a_ref = refs[pos]
        pos += 1
    o_ref, qs_ref, m_ref, acc_ref = refs[pos:pos + 4]
    step = pl.program_id(1)
    active = skip_tab[pl.program_id(0) * pl.num_programs(1) + step] == 0

    @pl.when(first_tab[step] == 1)
    def _():
        for h in range(GROUP):
            qs_ref[h, :, :HEAD_DIM] = qm_ref[:, h * HEAD_DIM:(h + 1) * HEAD_DIM]
            if aug_mode == "head":
                qs_ref[h, :, HEAD_DIM:] = a_ref[:, h * AUG:(h + 1) * AUG]
            elif aug_mode == "none":
                qs_ref[h, :, HEAD_DIM:] = jnp.zeros((tq, AUG), BF16)
        m_ref[...] = jnp.full(m_ref.shape, NEG_BIG, F32)
        acc_ref[...] = jnp.zeros(acc_ref.shape, F32)

    if aug_mode == "group":
        @pl.when(aug_tab[step] == 1)
        def _():
            for h in range(GROUP):
                qs_ref[h, :, HEAD_DIM:] = a_ref[...]

    def attend(kind):
        kb = k_ref[...]
        vb = v_ref[...]
        if kind:
            ahead = (qi_tab[step] * tq - kj_tab[step] * tk + lax.broadcasted_iota(jnp.int32, (tq, tk), 0)
                     - lax.broadcasted_iota(jnp.int32, (tq, tk), 1))
            keep = (ahead >= 0) if kind == 1 else ((ahead >= 0) & (ahead < WINDOW))
        for h in range(GROUP):
            s = lax.dot_general(qs_ref[h], kb, (((1,), (1,)), ((), ())), preferred_element_type=F32)
            if kind:
                s = jnp.where(keep, s, NEG_BIG)
            m_prev = m_ref[h]
            m_tile = s[:, :LANES]
            for c in range(1, tk // LANES):
                m_tile = jnp.maximum(m_tile, s[:, c * LANES:(c + 1) * LANES])
            m_new = jnp.maximum(m_prev, jnp.max(m_tile, axis=-1, keepdims=True))
            p = jnp.exp2(s - jnp.tile(m_new, (1, tk // LANES))).astype(BF16)
            alpha = jnp.exp2(m_prev - m_new)
            acc_ref[h] = jnp.tile(alpha, (1, 2)) * acc_ref[h] + jnp.dot(p, vb, preferred_element_type=F32)
            m_ref[h] = m_new

    for kind in range(3):
        pl.when((kind_tab[step] == kind) & active)(functools.partial(attend, kind))

    @pl.when(last_tab[step] == 1)
    def _():
        for h in range(GROUP):
            acc = acc_ref[h]
            o_ref[:, h * HEAD_DIM:(h + 1) * HEAD_DIM] = acc[:, :HEAD_DIM] / jnp.maximum(
                acc[:, HEAD_DIM:HEAD_DIM + 1], 1e-30)


def _pflash(qm, q_col, k_arr, v_arr, tables, tq, tk, *, aug=None, aug_mode="none", skip=None, name):
    T = qm.shape[0]
    n_steps = tables[0].shape[0]
    kvw = HEAD_DIM + AUG
    in_specs = [pl.BlockSpec((tq, GROUP * HEAD_DIM), lambda g, s, qi, *_: (qi[s], q_col * 2 + g)),
                pl.BlockSpec((tk, kvw), lambda g, s, *tabs: (tabs[7][g * n_steps + s], g)),
                pl.BlockSpec((tk, kvw), lambda g, s, *tabs: (tabs[7][g * n_steps + s], g))]
    args = [qm, k_arr, v_arr]
    if aug_mode == "head":
        in_specs.append(pl.BlockSpec((tq, GROUP * AUG), lambda g, s, qi, *_: (qi[s], g)))
        args.append(aug)
    elif aug_mode == "group":
        per_step = tk // SLC_BLOCK
        in_specs.append(pl.BlockSpec((None, None, tq, AUG),
                                     lambda g, s, qi, kj, *_: (g, (kj[s] * per_step) // AUG, qi[s], 0)))
        args.append(aug)
    return pl.pallas_call(
        functools.partial(_pflash_kernel, tq=tq, tk=tk, aug_mode=aug_mode),
        out_shape=jax.ShapeDtypeStruct((T, 2 * GROUP * HEAD_DIM), F32),
        grid_spec=pltpu.PrefetchScalarGridSpec(
            num_scalar_prefetch=8,
            grid=(2, n_steps),
            in_specs=in_specs,
            out_specs=pl.BlockSpec((tq, GROUP * HEAD_DIM), lambda g, s, qi, *_: (qi[s], g)),
            scratch_shapes=[pltpu.VMEM((GROUP, tq, kvw), BF16),
                            pltpu.VMEM((GROUP, tq, LANES), F32),
                            pltpu.VMEM((GROUP, tq, kvw), F32)],
        ),
        compiler_params=_cparams(("arbitrary", "arbitrary")),
        name=name,
    )(*(jnp.asarray(t) for t in tables), *(skip if skip is not None else _no_skip(tables)), *args)


def _fox_skip(tables, nrm, c_rows, T, tq, tk):
    qi, kj, last = np.asarray(tables[0]), np.asarray(tables[1]), np.asarray(tables[3])
    nq, nt, r = T // tq, T // tk, tq // tk
    qn = jnp.sqrt(nrm[:, :FOX_HEADS, 0].reshape(nq, r, FOX_HEADS).max(axis=1))
    kn = jnp.sqrt(nrm[:, FOX_HEADS:FOX_HEADS + FOX_KV, 0])
    kn_own = kn.reshape(nq, r, FOX_KV).max(axis=1)
    c2 = c_rows * LOG2E
    c_first_q, c_last_k = c2[::tq], c2[tk - 1::tk]
    rep = lambda a: jnp.repeat(a, GROUP, axis=-1)
    logit_max = qn[qi] * rep(kn[kj]) + (c_first_q[qi] - c_last_k[kj])
    max_lb = -qn[qi] * rep(kn_own[qi])
    before = jnp.asarray((kj + 1) * tk - 1 < qi * tq)[:, None]
    dead = (before & (logit_max < max_lb + UNDERFLOW_LOG2)).reshape(-1, FOX_KV, GROUP).all(axis=-1)
    last_of_block = np.zeros_like(kj)
    nxt = 0
    for s in range(len(kj) - 1, -1, -1):
        if last[s] == 1:
            nxt = kj[s]
        last_of_block[s] = nxt
    skip = dead.T.astype(jnp.int32).reshape(-1)
    kjd = jnp.where(dead.T, jnp.asarray(last_of_block)[None, :], jnp.asarray(kj)[None, :]).astype(jnp.int32)
    return skip, kjd.reshape(-1)


def _no_skip(tables):
    n = tables[1].shape[0]
    return jnp.zeros((2 * n,), jnp.int32), jnp.asarray(np.tile(tables[1], 2))


def _prompt_tables(T, tq, tk, window=False):
    per_half = AUG * SLC_BLOCK // tk
    rows = []
    for i in range(T // tq):
        q_lo, q_hi = i * tq, (i + 1) * tq - 1
        lo = max(0, q_lo - WINDOW + 1) // tk if window else 0
        hi = q_hi // tk
        for j in range(lo, hi + 1):
            crosses = (j + 1) * tk - 1 > q_lo
            kind = 2 if window else int(crosses)
            rows.append((i, j, int(j == lo), int(j == hi), kind, int(j == lo or j % per_half == 0)))
    return tuple(np.asarray(a, np.int32) for a in zip(*rows))


def _cmp_cat(w1):
    return jnp.concatenate([w1[:CMP_STRIDE], w1[CMP_STRIDE:]], axis=-1)


def _cmp_partial_prompt_kernel(xk0_ref, xk1_ref, xv0_ref, xv1_ref, wk_ref, wv_ref, ok_ref, ov_ref, *, tn):
    for x_refs, w_ref, o_ref in (((xk0_ref, xk1_ref), wk_ref, ok_ref), ((xv0_ref, xv1_ref), wv_ref, ov_ref)):
        for kv in range(NSA_KV):
            acc = None
            for r in range(CMP_STRIDE):
                x = x_refs[kv][pl.ds(r, tn, stride=CMP_STRIDE), :]
                d = jnp.dot(x.astype(BF16), w_ref[r], preferred_element_type=F32)
                acc = d if acc is None else acc + d
            o_ref[:, kv * HEAD_DIM:(kv + 1) * HEAD_DIM] = acc[:, :HEAD_DIM]
            o_ref[:, (NSA_KV + kv) * HEAD_DIM:(NSA_KV + kv + 1) * HEAD_DIM] = acc[:, HEAD_DIM:]


def _cmp_partial_prompt(big, T, w1k, w1v, tn):
    ns = T // CMP_STRIDE
    wspec = pl.BlockSpec((CMP_STRIDE, HEAD_DIM, 2 * HEAD_DIM), lambda i: (0, 0, 0))
    out = jax.ShapeDtypeStruct((ns, 4 * HEAD_DIM), F32)
    return pl.pallas_call(
        functools.partial(_cmp_partial_prompt_kernel, tn=tn),
        out_shape=(out, out),
        grid=(ns // tn,),
        in_specs=[pl.BlockSpec((tn * CMP_STRIDE, HEAD_DIM), (lambda c: (lambda i: (i, c)))(c))
                  for c in (2 * C_CK, 2 * C_CK + 1, 2 * C_CV, 2 * C_CV + 1)] + [wspec, wspec],
        out_specs=(pl.BlockSpec((tn, 4 * HEAD_DIM), lambda i: (i, 0)),) * 2,
        compiler_params=_cparams(("arbitrary",)),
        name="cmp_partial_prompt",
    )(big, big, big, big, _cmp_cat(w1k), _cmp_cat(w1v))


def _cmp_partial_paged_kernel(pt_ref, *refs, n_pages):
    xk, xv = refs[:n_pages], refs[n_pages:2 * n_pages]
    wk_ref, wv_ref, ok_ref, ov_ref = refs[2 * n_pages:]
    sub = PAGE // CMP_STRIDE
    rows_per_sub = CMP_STRIDE * NSA_KV
    for xs, w_ref, o_ref in ((xk, wk_ref, ok_ref), (xv, wv_ref, ov_ref)):
        for kv in range(NSA_KV):
            acc = None
            for r in range(CMP_STRIDE):
                x = jnp.concatenate([p[pl.ds(NSA_KV * r + kv, sub, stride=rows_per_sub), :] for p in xs], axis=0)
                d = jnp.dot(x.astype(BF16), w_ref[r], preferred_element_type=F32)
                acc = d if acc is None else acc + d
            o_ref[:, kv * HEAD_DIM:(kv + 1) * HEAD_DIM] = acc[:, :HEAD_DIM]
            o_ref[:, (NSA_KV + kv) * HEAD_DIM:(NSA_KV + kv + 1) * HEAD_DIM] = acc[:, HEAD_DIM:]


def _cmp_partial_paged(cache_k, cache_v, page_table, w1k, w1v, pages_per_step):
    B, n_pages = page_table.shape
    P = pages_per_step
    sub = PAGE // CMP_STRIDE
    n_pool = cache_k.shape[0]
    vk = cache_k.reshape(n_pool * PAGE * NSA_KV, HEAD_DIM)
    vv = cache_v.reshape(n_pool * PAGE * NSA_KV, HEAD_DIM)
    pt = page_table.reshape(-1)

    def src(u):
        return pl.BlockSpec((PAGE * NSA_KV, HEAD_DIM), lambda b, c, pt_ref: (pt_ref[b * n_pages + c * P + u], 0))

    wspec = pl.BlockSpec((CMP_STRIDE, HEAD_DIM, 2 * HEAD_DIM), lambda b, c, pt_ref: (0, 0, 0))
    out = jax.ShapeDtypeStruct((B, n_pages * sub, 4 * HEAD_DIM), F32)
    ospec = pl.BlockSpec((None, P * sub, 4 * HEAD_DIM), lambda b, c, pt_ref: (b, c, 0))
    return pl.pallas_call(
        functools.partial(_cmp_partial_paged_kernel, n_pages=P),
        out_shape=(out, out),
        grid_spec=pltpu.PrefetchScalarGridSpec(
            num_scalar_prefetch=1,
            grid=(B, n_pages // P),
            in_specs=[src(u) for u in range(P)] * 2 + [wspec, wspec],
            out_specs=(ospec, ospec),
        ),
        compiler_params=_cparams(("arbitrary", "arbitrary")),
        name="cmp_partial_paged",
    )(pt, *([vk] * P), *([vv] * P), _cmp_cat(w1k), _cmp_cat(w1v))


def _cmp_mlp_kernel(pk_ref, pv_ref, b1k_ref, b1v_ref, w2k_ref, w2v_ref, ok_ref, ov_ref, *, ns):
    for p_ref, b_ref, w_ref, o_ref in ((pk_ref, b1k_ref, w2k_ref, ok_ref), (pv_ref, b1v_ref, w2v_ref, ov_ref)):
        for kv in range(NSA_KV):
            p0 = p_ref[:, kv * HEAD_DIM:(kv + 1) * HEAD_DIM]
            p1 = p_ref[:, (NSA_KV + kv) * HEAD_DIM:(NSA_KV + kv + 1) * HEAD_DIM]
            nxt = pltpu.roll(p1, ns - 1, axis=0)
            h = _silu(p0 + nxt + b_ref[...])
            o_ref[:, kv * HEAD_DIM:(kv + 1) * HEAD_DIM] = jnp.dot(h.astype(BF16), w_ref[...],
                                                                  preferred_element_type=F32).astype(BF16)


def _cmp_mlp(pk, pv, b1k, b1v, w2k, w2v):
    B, ns, _ = pk.shape
    pspec = pl.BlockSpec((None, ns, 4 * HEAD_DIM), lambda b: (b, 0, 0))
    bspec = pl.BlockSpec((1, HEAD_DIM), lambda b: (0, 0))
    wspec = pl.BlockSpec((HEAD_DIM, HEAD_DIM), lambda b: (0, 0))
    out = jax.ShapeDtypeStruct((B, ns, NSA_KV * HEAD_DIM), BF16)
    ospec = pl.BlockSpec((None, ns, NSA_KV * HEAD_DIM), lambda b: (b, 0, 0))
    return pl.pallas_call(
        functools.partial(_cmp_mlp_kernel, ns=ns),
        out_shape=(out, out),
        grid=(B,),
        in_specs=[pspec, pspec, bspec, bspec, wspec, wspec],
        out_specs=(ospec, ospec),
        compiler_params=_cparams(("arbitrary",)),
        name="cmp_mlp",
    )(pk, pv, b1k.reshape(1, -1), b1v.reshape(1, -1), w2k.astype(BF16), w2v.astype(BF16))


def _cmp_attn_kernel(q_ref, kc_ref, vc_ref, ov_ref, o_ref, sel_ref, score_ref, *, tq, nbat, ns, nb, q_base, widths):
    qi = pl.program_id(1)
    rows = nbat * tq
    q_pos = q_base + qi * tq + (lax.broadcasted_iota(jnp.int32, (rows, 1), 0) & (tq - 1))
    n_idx = lax.broadcasted_iota(jnp.int32, (1, ns), 1)
    cmp_end = n_idx * CMP_STRIDE + (CMP_BLOCK - 1)
    mask = (cmp_end <= q_pos) & (n_idx < ns - 1)
    overlap = ov_ref[...]
    blk = lax.broadcasted_iota(jnp.int32, (rows, nb), 1)
    blkf = blk.astype(F32)
    cur = q_pos >> SLC_SHIFT
    forced = (blk == 0) | (blk == cur) | (blk == cur - 1)
    valid = blk <= cur
    n_pick = SLC_TOPN - jnp.where(cur >= nb, 1, 0)

    def attend(w):
        for b in range(nbat):
            rs = slice(b * tq, (b + 1) * tq)
            for g in range(NSA_KV):
                kb = kc_ref[b, :w, g * HEAD_DIM:(g + 1) * HEAD_DIM]
                vb = vc_ref[b, :w, g * HEAD_DIM:(g + 1) * HEAD_DIM]
                imp = jnp.zeros((tq, w), F32)
                for h in range(GROUP):
                    c = (g * GROUP + h) * HEAD_DIM
                    qh = (q_ref[rs, c:c + HEAD_DIM] * (SCALE * LOG2E)).astype(BF16)
                    s = lax.dot_general(qh, kb, (((1,), (1,)), ((), ())), preferred_element_type=F32)
                    s = jnp.where(mask[rs, :w], s, NEG_BIG)
                    m = jnp.max(s, axis=-1, keepdims=True)
                    e = jnp.exp2(s - m)
                    any_visible = jnp.where(m > 0.5 * NEG_BIG, 1.0, 0.0)
                    p = e * (any_visible / jnp.maximum(jnp.sum(e, axis=-1, keepdims=True), 1e-30))
                    o_ref[rs, c:c + HEAD_DIM] = jnp.dot(p.astype(BF16), vb, preferred_element_type=F32)
                    imp = imp + p
                hi = imp.astype(BF16)
                r1 = imp - hi.astype(F32)
                mid = r1.astype(BF16)
                lo = (r1 - mid.astype(F32)).astype(BF16)
                ov = overlap[:w]
                imp_slc = (jnp.dot(hi, ov, preferred_element_type=F32) + jnp.dot(mid, ov, preferred_element_type=F32)
                           + jnp.dot(lo, ov, preferred_element_type=F32))
                score_ref[g, rs] = jnp.where(valid[rs], imp_slc + FORCE_BONUS * forced[rs].astype(F32), NEG_BIG)

    n_vis = (q_base + (qi + 1) * tq - CMP_BLOCK) // CMP_STRIDE + 1
    lo_w = -(2 ** 30)
    for w in widths:
        pl.when((n_vis > lo_w) & ((n_vis <= w) | (w == ns)))(functools.partial(attend, w))
        lo_w = w
    scores = [score_ref[g] for g in range(NSA_KV)]

    def pick(it, carry):
        out = []
        for score, chosen in carry:
            best = jnp.max(score, axis=-1, keepdims=True)
            first = jnp.min(jnp.where(score == best, blkf, float(nb)), axis=-1, keepdims=True)
            hit = blkf == first
            out.append((jnp.where(hit, -jnp.inf, score), jnp.where(hit & (it < n_pick), 1.0, chosen)))
        return tuple(out)

    picked = lax.fori_loop(0, SLC_TOPN, pick, tuple((sc, jnp.zeros((rows, nb), F32)) for sc in scores))
    for g in range(NSA_KV):
        bias = jnp.where((picked[g][1] > 0.5) & valid, 0.0, SEL_OFF).astype(BF16)
        for b in range(nbat):
            for half in range(nb // LANES):
                sel_ref[b, g, half] = bias[b * tq:(b + 1) * tq, half * LANES:(half + 1) * LANES]


def _cmp_attn(q_arr, q_col, B, n_qblk, tq, kc, vc, q_base, nb):
    ns = kc.shape[1]
    per = SLC_BLOCK // CMP_STRIDE
    ci = np.arange(ns)[:, None]
    cb = np.arange(nb)[None, :]
    overlap = jnp.asarray((ci >= per * cb - 1) & (ci <= per * cb + per - 1), BF16)
    assert tq & (tq - 1) == 0
    nbat = _pick(B, (4, 2, 1)) if n_qblk == 1 else 1
    kspec = pl.BlockSpec((nbat, ns, NSA_KV * HEAD_DIM), lambda b, i: (b, 0, 0))
    quarter = ns // 4
    widths = (quarter, 2 * quarter, 3 * quarter, ns) if (n_qblk > 1 and quarter % LANES == 0) else (ns,)
    return pl.pallas_call(
        functools.partial(_cmp_attn_kernel, tq=tq, nbat=nbat, ns=ns, nb=nb, q_base=q_base, widths=widths),
        out_shape=(jax.ShapeDtypeStruct((B * n_qblk * tq, NSA_HEADS * HEAD_DIM), F32),
                   jax.ShapeDtypeStruct((B, NSA_KV, nb // LANES, n_qblk * tq, LANES), BF16)),
        grid=(B // nbat, n_qblk),
        in_specs=[pl.BlockSpec((nbat * tq, NSA_HEADS * HEAD_DIM), lambda b, i: (b * n_qblk + i, q_col)), kspec, kspec,
                  pl.BlockSpec((ns, nb), lambda b, i: (0, 0))],
        out_specs=(pl.BlockSpec((nbat * tq, NSA_HEADS * HEAD_DIM), lambda b, i: (b * n_qblk + i, 0)),
                   pl.BlockSpec((nbat, NSA_KV, nb // LANES, tq, LANES), lambda b, i: (b, 0, 0, i, 0))),
        scratch_shapes=[pltpu.VMEM((NSA_KV, nbat * tq, nb), F32)],
        compiler_params=_cparams(("arbitrary", "arbitrary")),
        name="cmp_attn_select",
    )(q_arr, kc, vc, overlap)


def _mix_out_kernel(x_ref, of_ref, oc_ref, os_ref, ow_ref, sm_ref, fz_ref, nz_ref, w_ref, h_ref):
    mix_f = of_ref[...] * _silu(fz_ref[...])
    sm = sm_ref[...]
    parts = []
    for h in range(NSA_HEADS):
        sl = slice(h * HEAD_DIM, (h + 1) * HEAD_DIM)
        c = FOX_HEADS + h * N_BRANCH
        parts.append(sm[:, c:c + 1] * oc_ref[:, sl] + sm[:, c + 1:c + 2] * os_ref[:, sl]
                     + sm[:, c + 2:c + 3] * ow_ref[:, sl])
    mix_n = jnp.concatenate(parts, axis=1) * _silu(nz_ref[...])
    mix = jnp.concatenate([mix_f, mix_n], axis=1).astype(BF16)
    h_ref[...] = x_ref[...] + jnp.dot(mix, w_ref[...], preferred_element_type=F32)


def _mix_out(x, o_f, o_c, o_s, o_w, big, w_out, tm):
    R, D = x.shape
    row = lambda i: (i, 0)
    wide = pl.BlockSpec((tm, 1024), row)
    return pl.pallas_call(
        _mix_out_kernel,
        out_shape=jax.ShapeDtypeStruct((R, D), F32),
        grid=(R // tm,),
        in_specs=[pl.BlockSpec((tm, D), row), wide, wide, wide, wide,
                  pl.BlockSpec((tm, LANES), lambda i: (i, C_SM * 2)),
                  pl.BlockSpec((tm, 1024), lambda i: (i, C_FZ // 4)),
                  pl.BlockSpec((tm, 1024), lambda i: (i, C_NZ // 4)),
                  pl.BlockSpec(w_out.shape, lambda i: (0, 0))],
        out_specs=pl.BlockSpec((tm, D), row),
        compiler_params=_cparams(("arbitrary",)),
        name="mix_out",
    )(x, o_f, o_c, o_s, o_w, big, big, big, w_out)


def _ple_kernel(h_ref, p_ref, g_ref, wg_ref, wp_ref, y_ref):
    h = h_ref[...]
    ms = jnp.mean(h * h, axis=-1, keepdims=True)
    hn = (h * lax.rsqrt(ms + EPS) * g_ref[...]).astype(BF16)
    gate = _sigmoid(jnp.dot(hn, wg_ref[...], preferred_element_type=F32))
    y_ref[...] = h + gate * jnp.dot(p_ref[...].astype(BF16), wp_ref[...], preferred_element_type=F32)


def _ple(h, p, ple_norm, w_gate, w_ple, tm):
    R, D = h.shape
    row = lambda i: (i, 0)
    return pl.pallas_call(
        _ple_kernel,
        out_shape=jax.ShapeDtypeStruct((R, D), F32),
        grid=(R // tm,),
        in_specs=[pl.BlockSpec((tm, D), row), pl.BlockSpec((tm, p.shape[1]), row),
                  pl.BlockSpec((1, D), lambda i: (0, 0)),
                  pl.BlockSpec(w_gate.shape, lambda i: (0, 0)),
                  pl.BlockSpec(w_ple.shape, lambda i: (0, 0))],
        out_specs=pl.BlockSpec((tm, D), row),
        compiler_params=_cparams(("arbitrary",)),
        name="ple_gate",
    )(h, p, ple_norm.reshape(1, D), w_gate, w_ple)


STK = GROUP * 4
STK_SHIFT = 4


def _decode_kernel(pt_ref, *refs, P, n_tok, has_bias, has_sel):
    q_ref = refs[0]
    k_pages, v_pages = refs[1:1 + P], refs[1 + P:1 + 2 * P]
    pos = 1 + 2 * P
    cq_ref = cs_ref = cst_ref = sel_ref = None
    if has_bias:
        cq_ref, cs_ref, cst_ref = refs[pos:pos + 3]
        pos += 3
    if has_sel:
        sel_ref = refs[pos]
        pos += 1
    il_ref, kt_ref, vt_ref, o_ref, m_ref, l_ref, acc_ref = refs[pos:pos + 7]
    step = pl.program_id(1)
    n_keys = P * PAGE
    n_rows = n_keys * NSA_KV
    rows = NSA_KV * STK
    row_group = lax.broadcasted_iota(jnp.int32, (rows, 1), 0) >> STK_SHIFT

    @pl.when(step == 0)
    def _():
        m_ref[...] = jnp.full(m_ref.shape, NEG_BIG, F32)
        l_ref[...] = jnp.zeros(l_ref.shape, F32)
        acc_ref[...] = jnp.zeros(acc_ref.shape, F32)

    def update(s, pv):
        m_prev = m_ref[...]
        m_new = jnp.maximum(m_prev, jnp.max(s, axis=-1, keepdims=True))
        alpha = jnp.exp(m_prev - m_new)
        p = jnp.exp(s - m_new)
        l_ref[...] = alpha * l_ref[...] + jnp.sum(p, axis=-1, keepdims=True)
        acc_ref[...] = alpha * acc_ref[...] + pv(p.astype(BF16))
        m_ref[...] = m_new

    qb = (q_ref[...] * SCALE).astype(BF16)
    kb = jnp.concatenate([r[...] for r in k_pages], axis=0).astype(BF16)
    vb = jnp.concatenate([r[...] for r in v_pages], axis=0).astype(BF16)
    s = lax.dot_general(qb, kb, (((1,), (1,)), ((), ())), preferred_element_type=F32)
    col = lax.broadcasted_iota(jnp.int32, (1, n_rows), 1)
    if has_bias:
        spread = il_ref[...]
        parts = []
        for u in range(P):
            c = cs_ref[:, u * PAGE:(u + 1) * PAGE]
            hi = c.astype(BF16)
            r1 = c - hi.astype(F32)
            mid = r1.astype(BF16)
            lo = (r1 - mid.astype(F32)).astype(BF16)
            parts.append(jnp.dot(hi, spread, preferred_element_type=F32) + jnp.dot(mid, spread, preferred_element_type=F32)
                         + jnp.dot(lo, spread, preferred_element_type=F32))
        s = s + (cq_ref[...] - jnp.concatenate(parts, axis=1))
    if has_sel:
        blk = ((step * n_keys + (col >> 1)) >> SLC_SHIFT) & (LANES - 1)
        onehot = (lax.broadcasted_iota(jnp.int32, (LANES, n_rows), 0) == blk).astype(BF16)
        s = s + jnp.dot(sel_ref[...], onehot, preferred_element_type=F32)
    s = jnp.where((col & 1) == row_group, s, NEG_BIG)
    update(s, lambda p: jnp.dot(p, vb, preferred_element_type=F32))

    @pl.when(step == pl.num_programs(1) - 1)
    def _():
        row_tok = lax.broadcasted_iota(jnp.int32, (rows, DEC_PAD), 0) & 3
        key_tok = lax.broadcasted_iota(jnp.int32, (rows, DEC_PAD), 1)
        mask = (key_tok <= row_tok) & (key_tok < n_tok)
        by_group = lambda a0, a1: jnp.where(row_group == 0, a0, a1)
        kt = [kt_ref[:, g * HEAD_DIM:(g + 1) * HEAD_DIM].astype(BF16) for g in range(NSA_KV)]
        vt = [vt_ref[:, g * HEAD_DIM:(g + 1) * HEAD_DIM].astype(BF16) for g in range(NSA_KV)]
        st = by_group(*[lax.dot_general(qb, k, (((1,), (1,)), ((), ())), preferred_element_type=F32) for k in kt])
        if has_bias:
            st = st + (cq_ref[...] - cst_ref[...])
        update(jnp.where(mask, st, NEG_BIG),
               lambda p: by_group(*[jnp.dot(p, v, preferred_element_type=F32) for v in vt]))
        o_ref[...] = acc_ref[...] / jnp.maximum(l_ref[...], 1e-30)


def _decode_attn(q_st, cache_k, cache_v, page_table, big, k_col, v_col, P, n_tok, *, bias=None, sel=None, name):
    B, n_pages = page_table.shape
    n_pool = cache_k.shape[0]
    rows = NSA_KV * STK
    k2 = cache_k.reshape(n_pool * PAGE * NSA_KV, HEAD_DIM)
    v2 = cache_v.reshape(n_pool * PAGE * NSA_KV, HEAD_DIM)
    pt = page_table.reshape(-1)
    n_keys = P * PAGE
    whole = lambda b, s, pt_ref: (b, 0, 0)

    def page(u):
        return pl.BlockSpec((PAGE * NSA_KV, HEAD_DIM), lambda b, s, pt_ref: (pt_ref[b * n_pages + s * P + u], 0))

    in_specs = [pl.BlockSpec((None, rows, HEAD_DIM), whole)]
    in_specs += [page(u) for u in range(P)] * 2
    args = [q_st.reshape(B, rows, HEAD_DIM)] + [k2] * P + [v2] * P
    if bias is not None:
        cq, cs, cs_tail = bias
        in_specs += [pl.BlockSpec((None, rows, 1), whole),
                     pl.BlockSpec((None, rows, n_keys), lambda b, s, pt_ref: (b, 0, s)),
                     pl.BlockSpec((None, rows, DEC_PAD), whole)]
        args += [cq.reshape(B, rows, 1), cs.reshape(B, rows, -1), cs_tail.reshape(B, rows, DEC_PAD)]
    if sel is not None:
        per_step = n_keys // SLC_BLOCK
        n_half = sel.shape[2]
        in_specs.append(pl.BlockSpec((None, None, rows, LANES), lambda b, s, pt_ref: (b, (s * per_step) // LANES, 0, 0)))
        args.append(sel.transpose(0, 2, 1, 3, 4).reshape(B, n_half, rows, LANES))
    spread = jnp.asarray(np.arange(PAGE)[:, None] == (np.arange(PAGE * NSA_KV)[None, :] >> 1), BF16)
    in_specs += [pl.BlockSpec((PAGE, PAGE * NSA_KV), lambda b, s, pt_ref: (0, 0)),
                 pl.BlockSpec((DEC_PAD, CH), lambda b, s, pt_ref: (b, k_col)),
                 pl.BlockSpec((DEC_PAD, CH), lambda b, s, pt_ref: (b, v_col))]
    args += [spread, big, big]
    out = pl.pallas_call(
        functools.partial(_decode_kernel, P=P, n_tok=n_tok, has_bias=bias is not None, has_sel=sel is not None),
        out_shape=jax.ShapeDtypeStruct((B, rows, HEAD_DIM), F32),
        grid_spec=pltpu.PrefetchScalarGridSpec(
            num_scalar_prefetch=1,
            grid=(B, n_pages // P),
            in_specs=in_specs,
            out_specs=pl.BlockSpec((None, rows, HEAD_DIM), whole),
            scratch_shapes=[pltpu.VMEM((rows, 1), F32), pltpu.VMEM((rows, 1), F32), pltpu.VMEM((rows, HEAD_DIM), F32)],
        ),
        compiler_params=_cparams(("arbitrary", "arbitrary")),
        name=name,
    )(pt, *args)
    return out.reshape(B, NSA_KV, STK, HEAD_DIM)


def _stack_rows(a, n_tok):
    B = a.shape[0]
    x = a.shape[-1] // (NSA_KV * GROUP)
    return a[:, :n_tok].reshape(B, n_tok, NSA_KV, GROUP, x).transpose(0, 2, 3, 1, 4).reshape(B, NSA_KV, STK, x)


def _unstack_rows(o_st, n_tok):
    B = o_st.shape[0]
    o = o_st.reshape(B, NSA_KV, GROUP, n_tok, HEAD_DIM).transpose(0, 3, 1, 2, 4).reshape(B, n_tok, -1)
    return jnp.pad(o, ((0, 0), (0, DEC_PAD - n_tok), (0, 0))).reshape(B * DEC_PAD, -1)


def _pick(n, cands):
    for c in cands:
        if n % c == 0:
            return c
    raise ValueError(f"no tile in {cands} divides {n}")


def _col(big, c, width=CH):
    return big[:, c * CH:c * CH + width]


def _prompt_layer(x, p_i, prm):
    T, D = x.shape
    big = _inproj(x, jnp.arange(T, dtype=jnp.int32), prm["attn_norm"], prm["w_perm"], prm["gain"], prm["bf_pad"],
                  _pick(T, (1024, 512, 256, 128)))
    logf = big[:, C_SM * CH:C_SM * CH + FOX_HEADS]
    nr = T // LANES
    c4, _ = _cumsum(logf.reshape(1, nr, LANES, FOX_HEADS).transpose(0, 1, 3, 2),
                    jnp.zeros((1, FOX_HEADS, DEC_PAD), F32))
    c_rows = c4.transpose(0, 1, 3, 2).reshape(T, FOX_HEADS)

    tq, tk = _pick(T, (1024, 512)), 512
    qm, fa, kf, vf, ks, vs, kw, vw, nrm = _pack(big, c_rows, tk)
    tabs = _prompt_tables(T, tq, tk)
    o_f = _pflash(qm, 0, kf, vf, tabs, tq, tk, aug=fa, aug_mode="head",
                  skip=_fox_skip(tabs, nrm, c_rows, T, tq, tk), name="fox_prompt")

    pk, pv = _cmp_partial_prompt(big, T, prm["w1k"], prm["w1v"], _pick(T // CMP_STRIDE, (256, 128, 64)))
    kc, vc = _cmp_mlp(pk[None], pv[None], prm["b1k"], prm["b1v"], prm["w2k"], prm["w2v"])
    nb = -(-(T // SLC_BLOCK) // LANES) * LANES
    tqc = _pick(T, (256, 128))
    o_c, sel = _cmp_attn(big, C_NQ // 4, 1, T // tqc, tqc, kc, vc, 0, nb)
    o_s = _pflash(qm, 1, ks, vs, tabs, tq, tk, aug=sel[0], aug_mode="group", name="slc_prompt")
    o_w = _pflash(qm, 1, kw, vw, _prompt_tables(T, tk, tk, window=True), tk, tk, name="win_prompt")

    tm = _pick(T, (256, 128))
    h = _mix_out(x, o_f, o_c, o_s, o_w, big, prm["w_out"], tm)
    y = _ple(h, p_i, prm["ple_norm"], prm["w_gate"], prm["w_ple"], tm)
    n_win = min(WINDOW, T)
    kv5 = lambda c: _col(big, c).reshape(1, 1, T, 2, HEAD_DIM)
    state = (kv5(C_FK), kv5(C_FV), logf.reshape(1, 1, T, FOX_HEADS), kv5(C_CK), kv5(C_CV), kv5(C_SK), kv5(C_SV),
             kv5(C_WK)[:, :, T - n_win:], kv5(C_WV)[:, :, T - n_win:])
    return y.reshape(1, T, D), state


def _sample_layer(x, p_i, caches, page_table, prm):
    c_fk, c_fv, c_flogf, c_ck, c_cv, c_sk, c_sv, c_wk, c_wv = caches
    B, Tn, D = x.shape
    n_pages = page_table.shape[1]
    past = n_pages * PAGE
    R = B * DEC_PAD
    xp = jnp.pad(x, ((0, 0), (0, DEC_PAD - Tn), (0, 0))).reshape(R, D)
    pos = jnp.tile(past + jnp.arange(DEC_PAD, dtype=jnp.int32), B)
    big = _inproj(xp, pos, prm["attn_norm"], prm["w_perm"], prm["gain"], prm["bf_pad"], _pick(R, (256, 128, 16)))
    logf = big[:, C_SM * CH:C_SM * CH + FOX_HEADS]

    assert Tn * GROUP == STK, "decode kernel stacks 4 heads x 4 new tokens per KV group"
    P = _pick(n_pages, (16, 8, 4, 2, 1))
    lf_pages = _gather_logf_pages(c_flogf.transpose(0, 2, 1), page_table, _pick(n_pages, (32, 16, 8, 4, 2, 1)))
    c4, c_new = _cumsum(lf_pages, logf.reshape(B, DEC_PAD, FOX_HEADS).transpose(0, 2, 1))
    rep = lambda a: jnp.repeat(a, Tn, axis=2)
    cs_past = rep(c4.transpose(0, 2, 1, 3).reshape(B, FOX_KV, GROUP, past))
    cs_tail = rep(c_new.reshape(B, FOX_KV, GROUP, DEC_PAD))
    cq = c_new[:, :, :Tn].reshape(B, FOX_KV, STK, 1)

    q3 = lambda c: big[:, c * CH:c * CH + 1024].reshape(B, DEC_PAD, 1024)
    o_f = _unstack_rows(
        _decode_attn(_stack_rows(q3(C_FQ), Tn), c_fk, c_fv, page_table, big, C_FK, C_FV, P, Tn,
                     bias=(cq, cs_past, cs_tail), name="fox_decode"), Tn)

    pk, pv = _cmp_partial_paged(c_ck, c_cv, page_table, prm["w1k"], prm["w1v"], _pick(n_pages, (32, 16, 8, 4, 2, 1)))
    kc, vc = _cmp_mlp(pk, pv, prm["b1k"], prm["b1v"], prm["w2k"], prm["w2v"])
    nb = -(-(past // SLC_BLOCK) // LANES) * LANES
    o_c, sel = _cmp_attn(big, C_NQ // 4, B, 1, DEC_PAD, kc, vc, past, nb)
    sel_st = jnp.tile(sel[:, :, :, None, :Tn], (1, 1, 1, GROUP, 1, 1)).reshape(B, NSA_KV, nb // LANES, STK, LANES)
    o_s = _unstack_rows(
        _decode_attn(_stack_rows(q3(C_NQ), Tn), c_sk, c_sv, page_table, big, C_SK, C_SV, P, Tn,
                     sel=sel_st, name="slc_decode"), Tn)
    n_buf = c_wk.shape[1]
    wspec = [pl.BlockSpec((n_buf, HEAD_DIM), lambda b, g, s, *_: (b, g))]
    wflat = lambda c: c.reshape(B * n_buf, NSA_KV * HEAD_DIM)
    o_w = _flash(big, C_NQ // 2, DEC_PAD, 1, B, _linear_tables(1), wspec, wspec, [wflat(c_wk), wflat(c_wv)], n_buf,
                 tail=(big, C_WK * 2, big, C_WV * 2), windowed=True,
                 q_base=past, k_base=past - n_buf, tail_base=past, name="win_decode")

    tm = _pick(R, (256, 128, 16))
    pp = jnp.pad(p_i, ((0, 0), (0, DEC_PAD - Tn), (0, 0))).reshape(R, -1)
    h = _mix_out(xp, o_f, o_c, o_s, o_w, big, prm["w_out"], tm)
    y = _ple(h, pp, prm["ple_norm"], prm["w_gate"], prm["w_ple"], tm)
    y = y.reshape(B, DEC_PAD, D)[:, :Tn]
    new = lambda c: _col(big, c).reshape(B, DEC_PAD, 2, HEAD_DIM)[:, :Tn]
    wk_new, wv_new = new(C_WK), new(C_WV)
    kw = jnp.concatenate([c_wk, wk_new], axis=1)[:, -n_buf:]
    vw = jnp.concatenate([c_wv, wv_new], axis=1)[:, -n_buf:]
    state = (new(C_FK), new(C_FV), logf.reshape(B, DEC_PAD, FOX_HEADS)[:, :Tn], new(C_CK), new(C_CV), new(C_SK),
             new(C_SV), kw, vw)
    return y, tuple(s[None] for s in state)


def kernel(x_prompt, x_sample, cache_fox_k, cache_fox_v, cache_fox_logf, cache_cmp_k, cache_cmp_v, cache_slc_k,
           cache_slc_v, cache_win_k, cache_win_v, page_table, p_prompt, p_sample, attn_norm, w_in, b_forget,
           fox_q_norm, fox_k_norm, nsa_q_norm, nsa_k_norm, cmp_k_w1, cmp_k_b1, cmp_k_w2, cmp_v_w1, cmp_v_b1,
           cmp_v_w2, w_out, ple_norm, w_ple, w_ple_gate):
    assert x_prompt.shape[0] == 1 and w_in.shape[0] == 1, "one prompt sequence, one layer"
    w_perm, gain, bf_pad = _prep_inproj_params(w_in[0], b_forget[0], fox_q_norm[0], fox_k_norm[0], nsa_q_norm[0],
                                               nsa_k_norm[0])
    prm = dict(attn_norm=attn_norm[0], w_perm=w_perm, gain=gain, bf_pad=bf_pad,
               w1k=cmp_k_w1[0].astype(BF16), w1v=cmp_v_w1[0].astype(BF16), b1k=cmp_k_b1[0], b1v=cmp_v_b1[0],
               w2k=cmp_k_w2[0], w2v=cmp_v_w2[0], w_out=w_out[0].astype(BF16), ple_norm=ple_norm[0],
               w_gate=w_ple_gate[0].astype(BF16), w_ple=w_ple[0].astype(BF16))
    y_p, st_p = _prompt_layer(x_prompt[0], p_prompt[0, 0], prm)
    caches = (cache_fox_k[0], cache_fox_v[0], cache_fox_logf[0], cache_cmp_k[0], cache_cmp_v[0], cache_slc_k[0],
              cache_slc_v[0], cache_win_k[0], cache_win_v[0])
    y_s, st_s = _sample_layer(x_sample, p_sample[0], caches, page_table, prm)
    return (y_p, y_s) + tuple(st_p) + tuple(st_s)
```

```python
import functools

import numpy as np
import jax
import jax.numpy as jnp
from jax import lax
from jax.experimental import pallas as pl
from jax.experimental.pallas import tpu as pltpu

F32 = jnp.float32
BF16 = jnp.bfloat16

HEAD_DIM = 128
FOX_HEADS = 8
FOX_KV = 2
NSA_HEADS = 8
NSA_KV = 2
GROUP = 4
N_BRANCH = 3
ROPE_THETA = 500000.0
ROT_DIM = HEAD_DIM // 4
ROT_HALF = ROT_DIM // 2
CMP_STRIDE = 16
CMP_BLOCK = 2 * CMP_STRIDE
SLC_BLOCK = 64
SLC_SHIFT = 6
SLC_TOPN = 16
WINDOW = 512
EPS = 1e-6
NEG_BIG = -1e30
FORCE_BONUS = 1e4
SEL_OFF = -30000.0
PAGE = 128
SCALE = HEAD_DIM ** -0.5

SPLIT_SIZES = (1024, 256, 256, 8, 1024, 1024, 256, 256, 256, 256, 256, 256, 24, 1024)

CH = 256
C_FQ, C_NQ, C_FZ, C_NZ = 0, 4, 8, 12
C_FK, C_CK, C_SK, C_WK, C_FV, C_CV, C_SV, C_WV, C_SM = 16, 17, 18, 19, 20, 21, 22, 23, 24
N_CH = 25
NP = N_CH * CH
LANES = 128
DEC_PAD = 16

VMEM_LIMIT = 56 * 1024 * 1024


def _cparams(sem):
    return pltpu.CompilerParams(dimension_semantics=sem, vmem_limit_bytes=VMEM_LIMIT)


def _sigmoid(x):
    return 1.0 / (1.0 + jnp.exp(-x))


def _silu(x):
    return x * _sigmoid(x)


def _inproj_kernel(x_ref, g_ref, w_ref, gain_ref, bf_ref, rc_ref, rs1_ref, rs2_ref, o_ref, xn_ref, y_ref):
    j = pl.program_id(1)
    c = j - 1
    cur, prev = j & 1, c & 1

    def matmul():
        y_ref[cur] = jnp.dot(xn_ref[...], w_ref[...], preferred_element_type=F32)

    @pl.when(j == 0)
    def _():
        x = x_ref[...]
        ms = jnp.mean(x * x, axis=-1, keepdims=True)
        xn_ref[...] = (x * lax.rsqrt(ms + EPS) * g_ref[...]).astype(BF16)
        matmul()

    is_q_rope = (c >= C_NQ) & (c < C_FZ)
    is_k_rope = (c >= C_CK) & (c <= C_WK)
    is_norm_only = ((c >= 0) & (c < C_NQ)) | (c == C_FK)
    is_rope = is_q_rope | is_k_rope
    is_raw = ((c >= C_FZ) & (c < C_FK)) | ((c >= C_FV) & (c < C_SM))

    def normed(h):
        yh = y_ref[prev, :, h * LANES:(h + 1) * LANES]
        ms = jnp.mean(yh * yh, axis=-1, keepdims=True)
        return yh * lax.rsqrt(ms + EPS) * gain_ref[:, h * LANES:(h + 1) * LANES]

    @pl.when(is_norm_only)
    def _():
        for h in range(CH // LANES):
            o_ref[:, h * LANES:(h + 1) * LANES] = normed(h)
        matmul()

    @pl.when(is_rope)
    def _():
        for h in range(CH // LANES):
            yn = normed(h)
            lo = pltpu.roll(yn, LANES - ROT_HALF, axis=1)
            hi = pltpu.roll(yn, ROT_HALF, axis=1)
            o_ref[:, h * LANES:(h + 1) * LANES] = yn * rc_ref[...] + lo * rs1_ref[...] + hi * rs2_ref[...]
        matmul()

    @pl.when(is_raw)
    def _():
        o_ref[...] = y_ref[prev]
        matmul()

    @pl.when(c == C_SM)
    def _():
        t = y_ref[prev, :, :LANES] + bf_ref[...]
        lane = lax.broadcasted_iota(jnp.int32, t.shape, 1)
        e = jnp.exp(-jnp.abs(t))
        logsig = jnp.minimum(t, 0.0) - jnp.log(1.0 + e)
        o_ref[:, :LANES] = jnp.where(lane < FOX_HEADS, logsig, _sigmoid(t))
        o_ref[:, LANES:] = jnp.zeros_like(t)


def _inproj(x, pos, attn_norm, w_perm, gain, bf_pad, tm):
    R, D = x.shape
    inv = ROPE_THETA ** (-(2.0 / ROT_DIM) * jnp.arange(ROT_HALF, dtype=F32))
    ang = pos.astype(F32)[:, None] * inv[None, :]
    cos, sin = jnp.cos(ang), jnp.sin(ang)
    z = jnp.zeros((R, LANES - ROT_DIM), F32)
    zh = jnp.zeros((R, ROT_HALF), F32)
    rc = jnp.concatenate([cos, cos, z + 1.0], axis=1)
    rs1 = jnp.concatenate([-sin, zh, z], axis=1)
    rs2 = jnp.concatenate([zh, sin, z], axis=1)
    row = lambda i, j: (i, 0)
    return pl.pallas_call(
        _inproj_kernel,
        out_shape=jax.ShapeDtypeStruct((R, NP), F32),
        grid=(R // tm, N_CH + 1),
        in_specs=[
            pl.BlockSpec((tm, D), row),
            pl.BlockSpec((1, D), lambda i, j: (0, 0)),
            pl.BlockSpec((D, CH), lambda i, j: (0, jnp.minimum(j, N_CH - 1))),
            pl.BlockSpec((1, CH), lambda i, j: (0, jnp.maximum(j - 1, 0))),
            pl.BlockSpec((1, LANES), lambda i, j: (0, 0)),
            pl.BlockSpec((tm, LANES), row),
            pl.BlockSpec((tm, LANES), row),
            pl.BlockSpec((tm, LANES), row),
        ],
        out_specs=pl.BlockSpec((tm, CH), lambda i, j: (i, jnp.maximum(j - 1, 0))),
        scratch_shapes=[pltpu.VMEM((tm, D), BF16), pltpu.VMEM((2, tm, CH), F32)],
        compiler_params=_cparams(("arbitrary", "arbitrary")),
        name="inproj",
    )(x, attn_norm.reshape(1, D), w_perm, gain, bf_pad, rc, rs1, rs2)


def _prep_inproj_params(w_in, b_forget, fox_q_norm, fox_k_norm, nsa_q_norm, nsa_k_norm):
    D = w_in.shape[0]
    offs = np.concatenate([[0], np.cumsum(SPLIT_SIZES)]).tolist()
    seg = [w_in[:, offs[i]:offs[i + 1]] for i in range(len(SPLIT_SIZES))]
    fq, fk, fv, flg, fz, nq, ck, cv, sk, sv, wk, wv, ng, nz = seg
    small = jnp.concatenate([flg, ng, jnp.zeros((D, CH - FOX_HEADS - NSA_HEADS * N_BRANCH), F32)], axis=1)
    w_perm = jnp.concatenate([fq, nq, fz, nz, fk, ck, sk, wk, fv, cv, sv, wv, small], axis=1).astype(BF16)
    zeros = lambda n: jnp.zeros((n,), F32)
    gain = jnp.concatenate([
        jnp.tile(fox_q_norm, FOX_HEADS), jnp.tile(nsa_q_norm, NSA_HEADS), zeros(2048),
        jnp.tile(fox_k_norm, FOX_KV), jnp.tile(nsa_k_norm[0], NSA_KV), jnp.tile(nsa_k_norm[1], NSA_KV),
        jnp.tile(nsa_k_norm[2], NSA_KV), zeros(4 * CH + CH)]).reshape(1, NP)
    bf_pad = jnp.concatenate([b_forget, zeros(LANES - FOX_HEADS)]).reshape(1, LANES)
    return w_perm, gain, bf_pad


def _gather_pages_kernel(pt_ref, *refs):
    n = len(refs) - 1
    o_ref = refs[n]
    for u in range(n):
        o_ref[u] = refs[u][...]


def _gather_logf_pages(cache_t, page_table, pages_per_step):
    B, n_pages = page_table.shape
    P = pages_per_step
    pt = page_table.reshape(-1)

    def src_map(u):
        return lambda b, c, pt_ref: (pt_ref[b * n_pages + c * P + u], 0, 0)

    return pl.pallas_call(
        _gather_pages_kernel,
        out_shape=jax.ShapeDtypeStruct((B, n_pages, FOX_HEADS, PAGE), F32),
        grid_spec=pltpu.PrefetchScalarGridSpec(
            num_scalar_prefetch=1,
            grid=(B, n_pages // P),
            in_specs=[pl.BlockSpec((None, FOX_HEADS, PAGE), src_map(u)) for u in range(P)],
            out_specs=pl.BlockSpec((None, P, FOX_HEADS, PAGE), lambda b, c, pt_ref: (b, c, 0, 0)),
        ),
        compiler_params=_cparams(("arbitrary", "arbitrary")),
        name="gather_logf_pages",
    )(pt, *([cache_t] * P))


def _cumsum_kernel(x_ref, new_ref, before_ref, o_ref, onew_ref, *, nr):
    hp = lax.Precision.HIGHEST
    x = x_ref[...].reshape(nr * FOX_HEADS, LANES)
    ci = lax.broadcasted_iota(jnp.int32, (LANES, LANES), 0)
    cj = lax.broadcasted_iota(jnp.int32, (LANES, LANES), 1)
    upper = (ci <= cj).astype(F32)
    lastcol = (ci == LANES - 1).astype(F32)
    within = jnp.dot(x, upper, precision=hp, preferred_element_type=F32)
    rowtot = jnp.dot(within, lastcol, precision=hp, preferred_element_type=F32)
    n = nr * FOX_HEADS
    before = before_ref[...]
    hi = rowtot.astype(BF16)
    r1 = rowtot - hi.astype(F32)
    mid = r1.astype(BF16)
    lo = (r1 - mid.astype(F32)).astype(BF16)
    prefix = (jnp.dot(before, hi, preferred_element_type=F32) + jnp.dot(before, mid, preferred_element_type=F32)
              + jnp.dot(before, lo, preferred_element_type=F32))
    o_ref[...] = (within + prefix).reshape(nr, FOX_HEADS, LANES)
    total = (prefix + rowtot)[n - FOX_HEADS:, :]
    nw = new_ref[...]
    ti = lax.broadcasted_iota(jnp.int32, (DEC_PAD, DEC_PAD), 0)
    tj = lax.broadcasted_iota(jnp.int32, (DEC_PAD, DEC_PAD), 1)
    onew_ref[...] = total[:, :DEC_PAD] + jnp.dot(nw, (ti <= tj).astype(F32), precision=hp,
                                                  preferred_element_type=F32)


def _cumsum(x4, new):
    B, nr = x4.shape[:2]
    n = nr * FOX_HEADS
    r = np.arange(n)
    before = jnp.asarray((r[:, None] % FOX_HEADS == r[None, :] % FOX_HEADS) & (r[None, :] < r[:, None]), BF16)
    return pl.pallas_call(
        functools.partial(_cumsum_kernel, nr=nr),
        out_shape=(jax.ShapeDtypeStruct(x4.shape, F32), jax.ShapeDtypeStruct(new.shape, F32)),
        grid=(B,),
        in_specs=[pl.BlockSpec((None, nr, FOX_HEADS, LANES), lambda b: (b, 0, 0, 0)),
                  pl.BlockSpec((None, FOX_HEADS, DEC_PAD), lambda b: (b, 0, 0)),
                  pl.BlockSpec((n, n), lambda b: (0, 0))],
        out_specs=(pl.BlockSpec((None, nr, FOX_HEADS, LANES), lambda b: (b, 0, 0, 0)),
                   pl.BlockSpec((None, FOX_HEADS, DEC_PAD), lambda b: (b, 0, 0))),
        compiler_params=_cparams(("arbitrary",)),
        name="cumsum_logf",
    )(x4, new, before)


def _flash_kernel(*refs, n_sp, n_src, tq, n_keys, has_bias, has_sel, has_tail, windowed,
                  q_base, k_base, tail_base):
    qi_tab, kj_tab, first_tab, last_tab = refs[:4]
    refs = refs[n_sp:]
    q_ref = refs[0]
    k_srcs = refs[1:1 + n_src]
    v_srcs = refs[1 + n_src:1 + 2 * n_src]
    pos = 1 + 2 * n_src
    cq_ref = cs_ref = sel_ref = kt_ref = vt_ref = cst_ref = None
    if has_bias:
        cq_ref, cs_ref = refs[pos], refs[pos + 1]
        pos += 2
    if has_sel:
        sel_ref = refs[pos]
        pos += 1
    if has_tail:
        kt_ref, vt_ref = refs[pos], refs[pos + 1]
        pos += 2
        if has_bias:
            cst_ref = refs[pos]
            pos += 1
    o_ref = refs[pos]
    qs_ref, m_ref, l_ref, acc_ref = refs[pos + 1:pos + 5]

    step = pl.program_id(2)
    g = pl.program_id(1)
    qi = qi_tab[step]
    kj = kj_tab[step]

    @pl.when(first_tab[step] == 1)
    def _():
        qs_ref[...] = (q_ref[...] * SCALE).astype(BF16)
        m_ref[...] = jnp.full(m_ref.shape, NEG_BIG, F32)
        l_ref[...] = jnp.zeros(l_ref.shape, F32)
        acc_ref[...] = jnp.zeros(acc_ref.shape, F32)

    q_pos = q_base + qi * tq + lax.broadcasted_iota(jnp.int32, (tq, 1), 0)

    def attend(k, v, k_pos, cs, sel_bias):
        kb = k.astype(BF16)
        vb = v.astype(BF16)
        mask = k_pos <= q_pos
        if windowed:
            mask = mask & ((q_pos - k_pos) < WINDOW)
        for h in range(GROUP):
            s = lax.dot_general(qs_ref[:, h * HEAD_DIM:(h + 1) * HEAD_DIM], kb, (((1,), (1,)), ((), ())),
                                preferred_element_type=F32)
            if cs is not None:
                hh = g * GROUP + h
                cq = jnp.sum(jnp.where(lax.broadcasted_iota(jnp.int32, cq_ref.shape, 1) == hh, cq_ref[...], 0.0),
                             axis=1, keepdims=True)
                s = s + (cq - cs[h:h + 1, :])
            if sel_bias is not None:
                s = s + sel_bias
            s = jnp.where(mask, s, NEG_BIG)
            m_prev = m_ref[h]
            m_new = jnp.maximum(m_prev, jnp.max(s, axis=-1, keepdims=True))
            alpha = jnp.exp(m_prev - m_new)
            p = jnp.exp(s - m_new)
            l_ref[h] = alpha * l_ref[h] + jnp.sum(p, axis=-1, keepdims=True)
            acc_ref[h] = alpha * acc_ref[h] + jnp.dot(p.astype(BF16), vb, preferred_element_type=F32)
            m_ref[h] = m_new

    if n_src == 1:
        k, v = k_srcs[0][...], v_srcs[0][...]
    else:
        k = jnp.concatenate([r[...] for r in k_srcs], axis=0)
        v = jnp.concatenate([r[...] for r in v_srcs], axis=0)
    k_pos = k_base + kj * n_keys + lax.broadcasted_iota(jnp.int32, (1, n_keys), 1)
    sel_bias = None
    if has_sel:
        blk = ((kj * n_keys + lax.broadcasted_iota(jnp.int32, (1, n_keys), 1)) >> SLC_SHIFT) & (LANES - 1)
        onehot = (lax.broadcasted_iota(jnp.int32, (LANES, n_keys), 0) == blk).astype(BF16)
        sel_bias = jnp.dot(sel_ref[...], onehot, preferred_element_type=F32)
    attend(k, v, k_pos, cs_ref[...] if has_bias else None, sel_bias)

    @pl.when(last_tab[step] == 1)
    def _():
        if has_tail:
            t_pos = tail_base + lax.broadcasted_iota(jnp.int32, (1, DEC_PAD), 1)
            attend(kt_ref[...], vt_ref[...], t_pos, cst_ref[...] if has_bias else None, None)
        for h in range(GROUP):
            o_ref[:, h * HEAD_DIM:(h + 1) * HEAD_DIM] = acc_ref[h] / jnp.maximum(l_ref[h], 1e-30)


def _flash(q_arr, q_col, tq, n_qblk, B, tables, k_specs, v_specs, kv_args, n_keys, *, prefetch=(),
           bias=None, sel=None, tail=None, windowed=False, q_base=0, k_base=0, tail_base=0, name="flash"):
    n_sp = len(tables) + len(prefetch)
    n_steps = tables[0].shape[0]
    n_src = len(k_specs)
    qmap = lambda b, g, s, qi, *_: (b * n_qblk + qi[s], q_col + g)
    in_specs = [pl.BlockSpec((tq, GROUP * HEAD_DIM), qmap)] + list(k_specs) + list(v_specs)
    args = [q_arr] + list(kv_args)
    if bias is not None:
        in_specs.append(pl.BlockSpec((tq, FOX_HEADS), lambda b, g, s, qi, *_: (b * n_qblk + qi[s], 0)))
        in_specs.append(pl.BlockSpec((None, None, GROUP, n_keys), lambda b, g, s, qi, kj, *_: (b, g, 0, kj[s])))
        args += [bias[0], bias[1]]
    if sel is not None:
        blocks_per_step = n_keys // SLC_BLOCK
        in_specs.append(pl.BlockSpec(
            (None, None, None, tq, LANES),
            lambda b, g, s, qi, kj, *_: (b, g, (kj[s] * blocks_per_step) // LANES, qi[s], 0)))
        args.append(sel)
    if tail is not None:
        k_arr, k_col, v_arr, v_col = tail
        in_specs.append(pl.BlockSpec((DEC_PAD, HEAD_DIM), lambda b, g, s, *_: (b, k_col + g)))
        in_specs.append(pl.BlockSpec((DEC_PAD, HEAD_DIM), lambda b, g, s, *_: (b, v_col + g)))
        args += [k_arr, v_arr]
        if bias is not None:
            in_specs.append(pl.BlockSpec((None, None, GROUP, DEC_PAD), lambda b, g, s, *_: (b, g, 0, 0)))
            args.append(bias[2])
    kern = functools.partial(
        _flash_kernel, n_sp=n_sp, n_src=n_src, tq=tq, n_keys=n_keys, has_bias=bias is not None,
        has_sel=sel is not None, has_tail=tail is not None, windowed=windowed,
        q_base=q_base, k_base=k_base, tail_base=tail_base)
    return pl.pallas_call(
        kern,
        out_shape=jax.ShapeDtypeStruct((B * n_qblk * tq, 2 * GROUP * HEAD_DIM), F32),
        grid_spec=pltpu.PrefetchScalarGridSpec(
            num_scalar_prefetch=n_sp,
            grid=(B, 2, n_steps),
            in_specs=in_specs,
            out_specs=pl.BlockSpec((tq, GROUP * HEAD_DIM), lambda b, g, s, qi, *_: (b * n_qblk + qi[s], g)),
            scratch_shapes=[pltpu.VMEM((tq, GROUP * HEAD_DIM), BF16),
                            pltpu.VMEM((GROUP, tq, 1), F32),
                            pltpu.VMEM((GROUP, tq, 1), F32),
                            pltpu.VMEM((GROUP, tq, HEAD_DIM), F32)],
        ),
        compiler_params=_cparams(("arbitrary", "arbitrary", "arbitrary")),
        name=name,
    )(*tables, *prefetch, *args)


def _causal_tables(n_blk, lookback=None):
    qi, kj, first, last = [], [], [], []
    for i in range(n_blk):
        lo = 0 if lookback is None else max(0, i - lookback)
        for j in range(lo, i + 1):
            qi.append(i)
            kj.append(j)
            first.append(int(j == lo))
            last.append(int(j == i))
    return tuple(jnp.asarray(a, jnp.int32) for a in (qi, kj, first, last))


def _linear_tables(n_steps):
    z = np.zeros((n_steps,), np.int32)
    first, last = z.copy(), z.copy()
    first[0], last[-1] = 1, 1
    return tuple(jnp.asarray(a, jnp.int32) for a in (z, np.arange(n_steps, dtype=np.int32), first, last))


LOG2E = 1.4426950408889634
AUG = 128
N_SPLIT = 3
AUG_C0 = N_SPLIT
NRM_ROWS = 16
UNDERFLOW_LOG2 = -160.0


def _split_part(x, part):
    hi = x.astype(BF16)
    r1 = x - hi.astype(F32)
    mid = r1.astype(BF16)
    lo = (r1 - mid.astype(F32)).astype(BF16)
    return jnp.where(part == 0, hi, jnp.where(part == 1, mid, lo))


def _pack_kernel(big_q, big_kv, c_ref, qm_ref, fa_ref, kf_ref, vf_ref, ks_ref, vs_ref, kw_ref, vw_ref, nrm_ref, *, tr):
    i = pl.program_id(0)
    lane = lax.broadcasted_iota(jnp.int32, (tr, AUG), 1)
    in_c = (lane >= AUG_C0) & (lane < AUG_C0 + GROUP * N_SPLIT)
    slot = jnp.zeros_like(lane)
    for j in range(1, GROUP):
        slot = slot + jnp.where(lane >= AUG_C0 + j * N_SPLIT, 1, 0)
    part = jnp.where(lane < AUG_C0, lane, lane - AUG_C0 - N_SPLIT * slot)
    c2 = c_ref[...] * LOG2E
    ones_lane = jnp.where(lane == 0, 1.0, 0.0).astype(BF16)
    zeros = jnp.zeros((tr, AUG), BF16)

    def max_sq_norm(x):
        xf = x.astype(F32)
        n2 = jnp.max(jnp.sum(xf * xf, axis=-1, keepdims=True), axis=0, keepdims=True)
        return jnp.broadcast_to(n2, (1, LANES))

    qmb = (big_q[...] * (SCALE * LOG2E)).astype(BF16)
    qm_ref[...] = qmb
    nrm_ref[...] = jnp.zeros(nrm_ref.shape, F32)
    for h in range(FOX_HEADS):
        nrm_ref[h:h + 1, :] = max_sq_norm(qmb[:, h * HEAD_DIM:(h + 1) * HEAD_DIM])
        cq = jnp.where(lane < AUG_C0, c2[:, h:h + 1], 0.0)
        own = in_c & (slot == h % GROUP)
        fa_ref[:, h * AUG:(h + 1) * AUG] = jnp.where(own, -1.0, _split_part(cq, part).astype(F32)).astype(BF16)

    tok = i * tr + lax.broadcasted_iota(jnp.int32, (tr, AUG), 0)
    blk_onehot = jnp.where(lane == ((tok >> SLC_SHIFT) & (AUG - 1)), 1.0, 0.0).astype(BF16)
    for g in range(NSA_KV):
        cs = jnp.zeros((tr, AUG), F32)
        for j in range(GROUP):
            cs = jnp.where(in_c & (slot == j), c2[:, g * GROUP + j:g * GROUP + j + 1], cs)
        k_aug = jnp.where(lane < AUG_C0, 1.0, _split_part(cs, part).astype(F32)).astype(BF16)
        col = lambda c: big_kv[:, (c - C_FK) * CH + g * HEAD_DIM:(c - C_FK) * CH + (g + 1) * HEAD_DIM].astype(BF16)
        lo, hi = g * 2 * HEAD_DIM, g * 2 * HEAD_DIM + HEAD_DIM
        for k_out, v_out, kc, vc, aug in ((kf_ref, vf_ref, C_FK, C_FV, k_aug), (ks_ref, vs_ref, C_SK, C_SV, blk_onehot),
                                          (kw_ref, vw_ref, C_WK, C_WV, zeros)):
            k_out[:, lo:hi] = col(kc)
            if kc == C_FK:
                nrm_ref[FOX_HEADS + g:FOX_HEADS + g + 1, :] = max_sq_norm(col(kc))
            k_out[:, hi:hi + AUG] = aug
            v_out[:, lo:hi] = col(vc)
            v_out[:, hi:hi + AUG] = ones_lane


def _pack(big, c_rows, tr):
    T = big.shape[0]
    row = lambda i: (i, 0)
    kv = jax.ShapeDtypeStruct((T, NSA_KV * (HEAD_DIM + AUG)), BF16)
    kvspec = pl.BlockSpec((tr, NSA_KV * (HEAD_DIM + AUG)), row)
    return pl.pallas_call(
        functools.partial(_pack_kernel, tr=tr),
        out_shape=(jax.ShapeDtypeStruct((T, 2048), BF16), jax.ShapeDtypeStruct((T, FOX_HEADS * AUG), BF16)) + (kv,) * 6
        + (jax.ShapeDtypeStruct((T // tr, NRM_ROWS, LANES), F32),),
        grid=(T // tr,),
        in_specs=[pl.BlockSpec((tr, 2048), row),
                  pl.BlockSpec((tr, 2048), lambda i: (i, C_FK // 8)),
                  pl.BlockSpec((tr, FOX_HEADS), row)],
        out_specs=(pl.BlockSpec((tr, 2048), row), pl.BlockSpec((tr, FOX_HEADS * AUG), row)) + (kvspec,) * 6
        + (pl.BlockSpec((None, NRM_ROWS, LANES), lambda i: (i, 0, 0)),),
        compiler_params=_cparams(("arbitrary",)),
        name="pack_qkv",
    )(big, big, c_rows)


def _pflash_kernel(qi_tab, kj_tab, first_tab, last_tab, kind_tab, aug_tab, skip_tab, kjd_tab, *refs, tq, tk,
                   aug_mode):
    qm_ref, k_ref, v_ref = refs[:3]
    pos = 3
    a_ref = None
    if aug_mode != "none":
        a_ref = refs[pos]
        pos += 1
    o_ref, qs_ref, m_ref, acc_ref = refs[pos:pos + 4]
    step = pl.program_id(1)
    active = skip_tab[pl.program_id(0) * pl.num_programs(1) + step] == 0

    @pl.when(first_tab[step] == 1)
    def _():
        for h in range(GROUP):
            qs_ref[h, :, :HEAD_DIM] = qm_ref[:, h * HEAD_DIM:(h + 1) * HEAD_DIM]
            if aug_mode == "head":
                qs_ref[h, :, HEAD_DIM:] = a_ref[:, h * AUG:(h + 1) * AUG]
            elif aug_mode == "none":
                qs_ref[h, :, HEAD_DIM:] = jnp.zeros((tq, AUG), BF16)
        m_ref[...] = jnp.full(m_ref.shape, NEG_BIG, F32)
        acc_ref[...] = jnp.zeros(acc_ref.shape, F32)

    if aug_mode == "group":
        @pl.when(aug_tab[step] == 1)
        def _():
            for h in range(GROUP):
                qs_ref[h, :, HEAD_DIM:] = a_ref[...]

    def attend(kind):
        kb = k_ref[...]
        vb = v_ref[...]
        if kind:
            ahead = (qi_tab[step] * tq - kj_tab[step] * tk + lax.broadcasted_iota(jnp.int32, (tq, tk), 0)
                     - lax.broadcasted_iota(jnp.int32, (tq, tk), 1))
            keep = (ahead >= 0) if kind == 1 else ((ahead >= 0) & (ahead < WINDOW))
        for h in range(GROUP):
            s = lax.dot_general(qs_ref[h], kb, (((1,), (1,)), ((), ())), preferred_element_type=F32)
            if kind:
                s = jnp.where(keep, s, NEG_BIG)
            m_prev = m_ref[h]
            m_tile = s[:, :LANES]
            for c in range(1, tk // LANES):
                m_tile = jnp.maximum(m_tile, s[:, c * LANES:(c + 1) * LANES])
            m_new = jnp.maximum(m_prev, jnp.max(m_tile, axis=-1, keepdims=True))
            p = jnp.exp2(s - jnp.tile(m_new, (1, tk // LANES))).astype(BF16)
            alpha = jnp.exp2(m_prev - m_new)
            acc_ref[h] = jnp.tile(alpha, (1, 2)) * acc_ref[h] + jnp.dot(p, vb, preferred_element_type=F32)
            m_ref[h] = m_new

    for kind in range(3):
        pl.when((kind_tab[step] == kind) & active)(functools.partial(attend, kind))

    @pl.when(last_tab[step] == 1)
    def _():
        for h in range(GROUP):
            acc = acc_ref[h]
            o_ref[:, h * HEAD_DIM:(h + 1) * HEAD_DIM] = acc[:, :HEAD_DIM] / jnp.maximum(
                acc[:, HEAD_DIM:HEAD_DIM + 1], 1e-30)


def _pflash(qm, q_col, k_arr, v_arr, tables, tq, tk, *, aug=None, aug_mode="none", skip=None, name):
    T = qm.shape[0]
    n_steps = tables[0].shape[0]
    kvw = HEAD_DIM + AUG
    in_specs = [pl.BlockSpec((tq, GROUP * HEAD_DIM), lambda g, s, qi, *_: (qi[s], q_col * 2 + g)),
                pl.BlockSpec((tk, kvw), lambda g, s, *tabs: (tabs[7][g * n_steps + s], g)),
                pl.BlockSpec((tk, kvw), lambda g, s, *tabs: (tabs[7][g * n_steps + s], g))]
    args = [qm, k_arr, v_arr]
    if aug_mode == "head":
        in_specs.append(pl.BlockSpec((tq, GROUP * AUG), lambda g, s, qi, *_: (qi[s], g)))
        args.append(aug)
    elif aug_mode == "group":
        per_step = tk // SLC_BLOCK
        in_specs.append(pl.BlockSpec((None, None, tq, AUG),
                                     lambda g, s, qi, kj, *_: (g, (kj[s] * per_step) // AUG, qi[s], 0)))
        args.append(aug)
    return pl.pallas_call(
        functools.partial(_pflash_kernel, tq=tq, tk=tk, aug_mode=aug_mode),
        out_shape=jax.ShapeDtypeStruct((T, 2 * GROUP * HEAD_DIM), F32),
        grid_spec=pltpu.PrefetchScalarGridSpec(
            num_scalar_prefetch=8,
            grid=(2, n_steps),
            in_specs=in_specs,
            out_specs=pl.BlockSpec((tq, GROUP * HEAD_DIM), lambda g, s, qi, *_: (qi[s], g)),
            scratch_shapes=[pltpu.VMEM((GROUP, tq, kvw), BF16),
                            pltpu.VMEM((GROUP, tq, LANES), F32),
                            pltpu.VMEM((GROUP, tq, kvw), F32)],
        ),
        compiler_params=_cparams(("arbitrary", "arbitrary")),
        name=name,
    )(*(jnp.asarray(t) for t in tables), *(skip if skip is not None else _no_skip(tables)), *args)


def _fox_skip(tables, nrm, c_rows, T, tq, tk):
    qi, kj, last = np.asarray(tables[0]), np.asarray(tables[1]), np.asarray(tables[3])
    nq, nt, r = T // tq, T // tk, tq // tk
    qn = jnp.sqrt(nrm[:, :FOX_HEADS, 0].reshape(nq, r, FOX_HEADS).max(axis=1))
    kn = jnp.sqrt(nrm[:, FOX_HEADS:FOX_HEADS + FOX_KV, 0])
    kn_own = kn.reshape(nq, r, FOX_KV).max(axis=1)
    c2 = c_rows * LOG2E
    c_first_q, c_last_k = c2[::tq], c2[tk - 1::tk]
    rep = lambda a: jnp.repeat(a, GROUP, axis=-1)
    logit_max = qn[qi] * rep(kn[kj]) + (c_first_q[qi] - c_last_k[kj])
    max_lb = -qn[qi] * rep(kn_own[qi])
    before = jnp.asarray((kj + 1) * tk - 1 < qi * tq)[:, None]
    dead = (before & (logit_max < max_lb + UNDERFLOW_LOG2)).reshape(-1, FOX_KV, GROUP).all(axis=-1)
    last_of_block = np.zeros_like(kj)
    nxt = 0
    for s in range(len(kj) - 1, -1, -1):
        if last[s] == 1:
            nxt = kj[s]
        last_of_block[s] = nxt
    skip = dead.T.astype(jnp.int32).reshape(-1)
    kjd = jnp.where(dead.T, jnp.asarray(last_of_block)[None, :], jnp.asarray(kj)[None, :]).astype(jnp.int32)
    return skip, kjd.reshape(-1)


def _no_skip(tables):
    n = tables[1].shape[0]
    return jnp.zeros((2 * n,), jnp.int32), jnp.asarray(np.tile(tables[1], 2))


def _prompt_tables(T, tq, tk, window=False):
    per_half = AUG * SLC_BLOCK // tk
    rows = []
    for i in range(T // tq):
        q_lo, q_hi = i * tq, (i + 1) * tq - 1
        lo = max(0, q_lo - WINDOW + 1) // tk if window else 0
        hi = q_hi // tk
        for j in range(lo, hi + 1):
            crosses = (j + 1) * tk - 1 > q_lo
            kind = 2 if window else int(crosses)
            rows.append((i, j, int(j == lo), int(j == hi), kind, int(j == lo or j % per_half == 0)))
    return tuple(np.asarray(a, np.int32) for a in zip(*rows))


def _cmp_cat(w1):
    return jnp.concatenate([w1[:CMP_STRIDE], w1[CMP_STRIDE:]], axis=-1)


def _cmp_partial_prompt_kernel(xk0_ref, xk1_ref, xv0_ref, xv1_ref, wk_ref, wv_ref, ok_ref, ov_ref, *, tn):
    for x_refs, w_ref, o_ref in (((xk0_ref, xk1_ref), wk_ref, ok_ref), ((xv0_ref, xv1_ref), wv_ref, ov_ref)):
        for kv in range(NSA_KV):
            acc = None
            for r in range(CMP_STRIDE):
                x = x_refs[kv][pl.ds(r, tn, stride=CMP_STRIDE), :]
                d = jnp.dot(x.astype(BF16), w_ref[r], preferred_element_type=F32)
                acc = d if acc is None else acc + d
            o_ref[:, kv * HEAD_DIM:(kv + 1) * HEAD_DIM] = acc[:, :HEAD_DIM]
            o_ref[:, (NSA_KV + kv) * HEAD_DIM:(NSA_KV + kv + 1) * HEAD_DIM] = acc[:, HEAD_DIM:]


def _cmp_partial_prompt(big, T, w1k, w1v, tn):
    ns = T // CMP_STRIDE
    wspec = pl.BlockSpec((CMP_STRIDE, HEAD_DIM, 2 * HEAD_DIM), lambda i: (0, 0, 0))
    out = jax.ShapeDtypeStruct((ns, 4 * HEAD_DIM), F32)
    return pl.pallas_call(
        functools.partial(_cmp_partial_prompt_kernel, tn=tn),
        out_shape=(out, out),
        grid=(ns // tn,),
        in_specs=[pl.BlockSpec((tn * CMP_STRIDE, HEAD_DIM), (lambda c: (lambda i: (i, c)))(c))
                  for c in (2 * C_CK, 2 * C_CK + 1, 2 * C_CV, 2 * C_CV + 1)] + [wspec, wspec],
        out_specs=(pl.BlockSpec((tn, 4 * HEAD_DIM), lambda i: (i, 0)),) * 2,
        compiler_params=_cparams(("arbitrary",)),
        name="cmp_partial_prompt",
    )(big, big, big, big, _cmp_cat(w1k), _cmp_cat(w1v))


def _cmp_partial_paged_kernel(pt_ref, *refs, n_pages):
    xk, xv = refs[:n_pages], refs[n_pages:2 * n_pages]
    wk_ref, wv_ref, ok_ref, ov_ref = refs[2 * n_pages:]
    sub = PAGE // CMP_STRIDE
    rows_per_sub = CMP_STRIDE * NSA_KV
    for xs, w_ref, o_ref in ((xk, wk_ref, ok_ref), (xv, wv_ref, ov_ref)):
        x_all = jnp.concatenate([p[...] for p in xs], axis=0).reshape(n_pages * sub, rows_per_sub, HEAD_DIM)
        by_row = pltpu.einshape("msd->smd", x_all).astype(BF16)
        for kv in range(NSA_KV):
            acc = None
            for r in range(CMP_STRIDE):
                d = jnp.dot(by_row[NSA_KV * r + kv], w_ref[r], preferred_element_type=F32)
                acc = d if acc is None else acc + d
            o_ref[:, kv * HEAD_DIM:(kv + 1) * HEAD_DIM] = acc[:, :HEAD_DIM]
            o_ref[:, (NSA_KV + kv) * HEAD_DIM:(NSA_KV + kv + 1) * HEAD_DIM] = acc[:, HEAD_DIM:]


def _cmp_partial_paged(cache_k, cache_v, page_table, w1k, w1v, pages_per_step):
    B, n_pages = page_table.shape
    P = pages_per_step
    sub = PAGE // CMP_STRIDE
    n_pool = cache_k.shape[0]
    vk = cache_k.reshape(n_pool * PAGE * NSA_KV, HEAD_DIM)
    vv = cache_v.reshape(n_pool * PAGE * NSA_KV, HEAD_DIM)
    pt = page_table.reshape(-1)

    def src(u):
        return pl.BlockSpec((PAGE * NSA_KV, HEAD_DIM), lambda b, c, pt_ref: (pt_ref[b * n_pages + c * P + u], 0))

    wspec = pl.BlockSpec((CMP_STRIDE, HEAD_DIM, 2 * HEAD_DIM), lambda b, c, pt_ref: (0, 0, 0))
    out = jax.ShapeDtypeStruct((B, n_pages * sub, 4 * HEAD_DIM), F32)
    ospec = pl.BlockSpec((None, P * sub, 4 * HEAD_DIM), lambda b, c, pt_ref: (b, c, 0))
    return pl.pallas_call(
        functools.partial(_cmp_partial_paged_kernel, n_pages=P),
        out_shape=(out, out),
        grid_spec=pltpu.PrefetchScalarGridSpec(
            num_scalar_prefetch=1,
            grid=(B, n_pages // P),
            in_specs=[src(u) for u in range(P)] * 2 + [wspec, wspec],
            out_specs=(ospec, ospec),
        ),
        compiler_params=_cparams(("arbitrary", "arbitrary")),
        name="cmp_partial_paged",
    )(pt, *([vk] * P), *([vv] * P), _cmp_cat(w1k), _cmp_cat(w1v))


def _cmp_mlp_kernel(pk_ref, pv_ref, b1k_ref, b1v_ref, w2k_ref, w2v_ref, ok_ref, ov_ref, *, ns):
    for p_ref, b_ref, w_ref, o_ref in ((pk_ref, b1k_ref, w2k_ref, ok_ref), (pv_ref, b1v_ref, w2v_ref, ov_ref)):
        for kv in range(NSA_KV):
            p0 = p_ref[:, kv * HEAD_DIM:(kv + 1) * HEAD_DIM]
            p1 = p_ref[:, (NSA_KV + kv) * HEAD_DIM:(NSA_KV + kv + 1) * HEAD_DIM]
            nxt = pltpu.roll(p1, ns - 1, axis=0)
            h = _silu(p0 + nxt + b_ref[...])
            o_ref[:, kv * HEAD_DIM:(kv + 1) * HEAD_DIM] = jnp.dot(h.astype(BF16), w_ref[...],
                                                                  preferred_element_type=F32).astype(BF16)


def _cmp_mlp(pk, pv, b1k, b1v, w2k, w2v):
    B, ns, _ = pk.shape
    pspec = pl.BlockSpec((None, ns, 4 * HEAD_DIM), lambda b: (b, 0, 0))
    bspec = pl.BlockSpec((1, HEAD_DIM), lambda b: (0, 0))
    wspec = pl.BlockSpec((HEAD_DIM, HEAD_DIM), lambda b: (0, 0))
    out = jax.ShapeDtypeStruct((B, ns, NSA_KV * HEAD_DIM), BF16)
    ospec = pl.BlockSpec((None, ns, NSA_KV * HEAD_DIM), lambda b: (b, 0, 0))
    return pl.pallas_call(
        functools.partial(_cmp_mlp_kernel, ns=ns),
        out_shape=(out, out),
        grid=(B,),
        in_specs=[pspec, pspec, bspec, bspec, wspec, wspec],
        out_specs=(ospec, ospec),
        compiler_params=_cparams(("arbitrary",)),
        name="cmp_mlp",
    )(pk, pv, b1k.reshape(1, -1), b1v.reshape(1, -1), w2k.astype(BF16), w2v.astype(BF16))


def _cmp_attn_kernel(q_ref, kc_ref, vc_ref, ov_ref, o_ref, sel_ref, score_ref, *, tq, nbat, ns, nb, q_base, widths):
    qi = pl.program_id(1)
    rows = nbat * tq
    q_pos = q_base + qi * tq + (lax.broadcasted_iota(jnp.int32, (rows, 1), 0) & (tq - 1))
    n_idx = lax.broadcasted_iota(jnp.int32, (1, ns), 1)
    cmp_end = n_idx * CMP_STRIDE + (CMP_BLOCK - 1)
    mask = (cmp_end <= q_pos) & (n_idx < ns - 1)
    overlap = ov_ref[...]
    blk = lax.broadcasted_iota(jnp.int32, (rows, nb), 1)
    blkf = blk.astype(F32)
    cur = q_pos >> SLC_SHIFT
    forced = (blk == 0) | (blk == cur) | (blk == cur - 1)
    valid = blk <= cur
    n_pick = SLC_TOPN - jnp.where(cur >= nb, 1, 0)

    def attend(w):
        for b in range(nbat):
            rs = slice(b * tq, (b + 1) * tq)
            for g in range(NSA_KV):
                kb = kc_ref[b, :w, g * HEAD_DIM:(g + 1) * HEAD_DIM]
                vb = vc_ref[b, :w, g * HEAD_DIM:(g + 1) * HEAD_DIM]
                imp = jnp.zeros((tq, w), F32)
                for h in range(GROUP):
                    c = (g * GROUP + h) * HEAD_DIM
                    qh = (q_ref[rs, c:c + HEAD_DIM] * (SCALE * LOG2E)).astype(BF16)
                    s = lax.dot_general(qh, kb, (((1,), (1,)), ((), ())), preferred_element_type=F32)
                    s = jnp.where(mask[rs, :w], s, NEG_BIG)
                    m = jnp.max(s, axis=-1, keepdims=True)
                    e = jnp.exp2(s - m)
                    any_visible = jnp.where(m > 0.5 * NEG_BIG, 1.0, 0.0)
                    p = e * (any_visible / jnp.maximum(jnp.sum(e, axis=-1, keepdims=True), 1e-30))
                    o_ref[rs, c:c + HEAD_DIM] = jnp.dot(p.astype(BF16), vb, preferred_element_type=F32)
                    imp = imp + p
                hi = imp.astype(BF16)
                r1 = imp - hi.astype(F32)
                mid = r1.astype(BF16)
                lo = (r1 - mid.astype(F32)).astype(BF16)
                ov = overlap[:w]
                imp_slc = (jnp.dot(hi, ov, preferred_element_type=F32) + jnp.dot(mid, ov, preferred_element_type=F32)
                           + jnp.dot(lo, ov, preferred_element_type=F32))
                score_ref[g, rs] = jnp.where(valid[rs], imp_slc + FORCE_BONUS * forced[rs].astype(F32), NEG_BIG)

    n_vis = (q_base + (qi + 1) * tq - CMP_BLOCK) // CMP_STRIDE + 1
    lo_w = -(2 ** 30)
    for w in widths:
        pl.when((n_vis > lo_w) & ((n_vis <= w) | (w == ns)))(functools.partial(attend, w))
        lo_w = w
    scores = [score_ref[g] for g in range(NSA_KV)]

    def pick_one(score, allowed):
        best = jnp.max(score, axis=-1, keepdims=True)
        first = jnp.min(jnp.where(score == best, blkf, float(nb)), axis=-1, keepdims=True)
        return jnp.where((blkf == first) & allowed, -jnp.inf, score)

    picked = lax.fori_loop(0, SLC_TOPN - 1, lambda it, sc: tuple(pick_one(s, True) for s in sc), tuple(scores))
    picked = [pick_one(s, n_pick >= SLC_TOPN) for s in picked]
    for g in range(NSA_KV):
        bias = jnp.where((picked[g] == -jnp.inf) & valid, 0.0, SEL_OFF).astype(BF16)
        for b in range(nbat):
            for half in range(nb // LANES):
                sel_ref[b, g, half] = bias[b * tq:(b + 1) * tq, half * LANES:(half + 1) * LANES]


def _cmp_attn(q_arr, q_col, B, n_qblk, tq, kc, vc, q_base, nb):
    ns = kc.shape[1]
    per = SLC_BLOCK // CMP_STRIDE
    ci = np.arange(ns)[:, None]
    cb = np.arange(nb)[None, :]
    overlap = jnp.asarray((ci >= per * cb - 1) & (ci <= per * cb + per - 1), BF16)
    assert tq & (tq - 1) == 0
    nbat = _pick(B, (4, 2, 1)) if n_qblk == 1 else 1
    kspec = pl.BlockSpec((nbat, ns, NSA_KV * HEAD_DIM), lambda b, i: (b, 0, 0))
    quarter = ns // 4
    widths = (quarter, 2 * quarter, 3 * quarter, ns) if (n_qblk > 1 and quarter % LANES == 0) else (ns,)
    return pl.pallas_call(
        functools.partial(_cmp_attn_kernel, tq=tq, nbat=nbat, ns=ns, nb=nb, q_base=q_base, widths=widths),
        out_shape=(jax.ShapeDtypeStruct((B * n_qblk * tq, NSA_HEADS * HEAD_DIM), F32),
                   jax.ShapeDtypeStruct((B, NSA_KV, nb // LANES, n_qblk * tq, LANES), BF16)),
        grid=(B // nbat, n_qblk),
        in_specs=[pl.BlockSpec((nbat * tq, NSA_HEADS * HEAD_DIM), lambda b, i: (b * n_qblk + i, q_col)), kspec, kspec,
                  pl.BlockSpec((ns, nb), lambda b, i: (0, 0))],
        out_specs=(pl.BlockSpec((nbat * tq, NSA_HEADS * HEAD_DIM), lambda b, i: (b * n_qblk + i, 0)),
                   pl.BlockSpec((nbat, NSA_KV, nb // LANES, tq, LANES), lambda b, i: (b, 0, 0, i, 0))),
        scratch_shapes=[pltpu.VMEM((NSA_KV, nbat * tq, nb), F32)],
        compiler_params=_cparams(("arbitrary", "arbitrary")),
        name="cmp_attn_select",
    )(q_arr, kc, vc, overlap)


def _mix_out_kernel(x_ref, of_ref, oc_ref, os_ref, ow_ref, sm_ref, fz_ref, nz_ref, w_ref, h_ref):
    mix_f = of_ref[...] * _silu(fz_ref[...])
    sm = sm_ref[...]
    parts = []
    for h in range(NSA_HEADS):
        sl = slice(h * HEAD_DIM, (h + 1) * HEAD_DIM)
        c = FOX_HEADS + h * N_BRANCH
        parts.append(sm[:, c:c + 1] * oc_ref[:, sl] + sm[:, c + 1:c + 2] * os_ref[:, sl]
                     + sm[:, c + 2:c + 3] * ow_ref[:, sl])
    mix_n = jnp.concatenate(parts, axis=1) * _silu(nz_ref[...])
    mix = jnp.concatenate([mix_f, mix_n], axis=1).astype(BF16)
    h_ref[...] = x_ref[...] + jnp.dot(mix, w_ref[...], preferred_element_type=F32)


def _mix_out(x, o_f, o_c, o_s, o_w, big, w_out, tm):
    R, D = x.shape
    row = lambda i: (i, 0)
    wide = pl.BlockSpec((tm, 1024), row)
    return pl.pallas_call(
        _mix_out_kernel,
        out_shape=jax.ShapeDtypeStruct((R, D), F32),
        grid=(R // tm,),
        in_specs=[pl.BlockSpec((tm, D), row), wide, wide, wide, wide,
                  pl.BlockSpec((tm, LANES), lambda i: (i, C_SM * 2)),
                  pl.BlockSpec((tm, 1024), lambda i: (i, C_FZ // 4)),
                  pl.BlockSpec((tm, 1024), lambda i: (i, C_NZ // 4)),
                  pl.BlockSpec(w_out.shape, lambda i: (0, 0))],
        out_specs=pl.BlockSpec((tm, D), row),
        compiler_params=_cparams(("arbitrary",)),
        name="mix_out",
    )(x, o_f, o_c, o_s, o_w, big, big, big, w_out)


def _ple_kernel(h_ref, p_ref, g_ref, wg_ref, wp_ref, y_ref):
    h = h_ref[...]
    ms = jnp.mean(h * h, axis=-1, keepdims=True)
    hn = (h * lax.rsqrt(ms + EPS) * g_ref[...]).astype(BF16)
    gate = _sigmoid(jnp.dot(hn, wg_ref[...], preferred_element_type=F32))
    y_ref[...] = h + gate * jnp.dot(p_ref[...].astype(BF16), wp_ref[...], preferred_element_type=F32)


def _ple(h, p, ple_norm, w_gate, w_ple, tm):
    R, D = h.shape
    row = lambda i: (i, 0)
    return pl.pallas_call(
        _ple_kernel,
        out_shape=jax.ShapeDtypeStruct((R, D), F32),
        grid=(R // tm,),
        in_specs=[pl.BlockSpec((tm, D), row), pl.BlockSpec((tm, p.shape[1]), row),
                  pl.BlockSpec((1, D), lambda i: (0, 0)),
                  pl.BlockSpec(w_gate.shape, lambda i: (0, 0)),
                  pl.BlockSpec(w_ple.shape, lambda i: (0, 0))],
        out_specs=pl.BlockSpec((tm, D), row),
        compiler_params=_cparams(("arbitrary",)),
        name="ple_gate",
    )(h, p, ple_norm.reshape(1, D), w_gate, w_ple)


STK = GROUP * 4
STK_SHIFT = 4


def _decode_kernel(pt_ref, *refs, P, n_tok, has_bias, has_sel):
    q_ref = refs[0]
    k_pages, v_pages = refs[1:1 + P], refs[1 + P:1 + 2 * P]
    pos = 1 + 2 * P
    cq_ref = cs_ref = cst_ref = sel_ref = None
    if has_bias:
        cq_ref, cs_ref, cst_ref = refs[pos:pos + 3]
        pos += 3
    if has_sel:
        sel_ref = refs[pos]
        pos += 1
    il_ref, kt_ref, vt_ref, o_ref, m_ref, l_ref, acc_ref = refs[pos:pos + 7]
    step = pl.program_id(1)
    n_keys = P * PAGE
    n_rows = n_keys * NSA_KV
    rows = NSA_KV * STK
    row_group = lax.broadcasted_iota(jnp.int32, (rows, 1), 0) >> STK_SHIFT

    @pl.when(step == 0)
    def _():
        m_ref[...] = jnp.full(m_ref.shape, NEG_BIG, F32)
        l_ref[...] = jnp.zeros(l_ref.shape, F32)
        acc_ref[...] = jnp.zeros(acc_ref.shape, F32)

    def update(s, pv):
        m_prev = m_ref[...]
        m_new = jnp.maximum(m_prev, jnp.max(s, axis=-1, keepdims=True))
        alpha = jnp.exp(m_prev - m_new)
        p = jnp.exp(s - m_new)
        l_ref[...] = alpha * l_ref[...] + jnp.sum(p, axis=-1, keepdims=True)
        acc_ref[...] = alpha * acc_ref[...] + pv(p.astype(BF16))
        m_ref[...] = m_new

    qb = (q_ref[...] * SCALE).astype(BF16)
    kb = jnp.concatenate([r[...] for r in k_pages], axis=0).astype(BF16)
    vb = jnp.concatenate([r[...] for r in v_pages], axis=0).astype(BF16)
    s = lax.dot_general(qb, kb, (((1,), (1,)), ((), ())), preferred_element_type=F32)
    col = lax.broadcasted_iota(jnp.int32, (1, n_rows), 1)
    if has_bias:
        spread = il_ref[...]
        parts = []
        for u in range(P):
            c = cs_ref[:, u * PAGE:(u + 1) * PAGE]
            hi = c.astype(BF16)
            r1 = c - hi.astype(F32)
            mid = r1.astype(BF16)
            lo = (r1 - mid.astype(F32)).astype(BF16)
            parts.append(jnp.dot(hi, spread, preferred_element_type=F32) + jnp.dot(mid, spread, preferred_element_type=F32)
                         + jnp.dot(lo, spread, preferred_element_type=F32))
        s = s + (cq_ref[...] - jnp.concatenate(parts, axis=1))
    if has_sel:
        blk = ((step * n_keys + (col >> 1)) >> SLC_SHIFT) & (LANES - 1)
        onehot = (lax.broadcasted_iota(jnp.int32, (LANES, n_rows), 0) == blk).astype(BF16)
        s = s + jnp.dot(sel_ref[...], onehot, preferred_element_type=F32)
    s = jnp.where((col & 1) == row_group, s, NEG_BIG)
    update(s, lambda p: jnp.dot(p, vb, preferred_element_type=F32))

    @pl.when(step == pl.num_programs(1) - 1)
    def _():
        row_tok = lax.broadcasted_iota(jnp.int32, (rows, DEC_PAD), 0) & 3
        key_tok = lax.broadcasted_iota(jnp.int32, (rows, DEC_PAD), 1)
        mask = (key_tok <= row_tok) & (key_tok < n_tok)
        by_group = lambda a0, a1: jnp.where(row_group == 0, a0, a1)
        kt = [kt_ref[:, g * HEAD_DIM:(g + 1) * HEAD_DIM].astype(BF16) for g in range(NSA_KV)]
        vt = [vt_ref[:, g * HEAD_DIM:(g + 1) * HEAD_DIM].astype(BF16) for g in range(NSA_KV)]
        st = by_group(*[lax.dot_general(qb, k, (((1,), (1,)), ((), ())), preferred_element_type=F32) for k in kt])
        if has_bias:
            st = st + (cq_ref[...] - cst_ref[...])
        update(jnp.where(mask, st, NEG_BIG),
               lambda p: by_group(*[jnp.dot(p, v, preferred_element_type=F32) for v in vt]))
        o_ref[...] = acc_ref[...] / jnp.maximum(l_ref[...], 1e-30)


def _decode_attn(q_st, cache_k, cache_v, page_table, big, k_col, v_col, P, n_tok, *, bias=None, sel=None, name):
    B, n_pages = page_table.shape
    n_pool = cache_k.shape[0]
    rows = NSA_KV * STK
    k2 = cache_k.reshape(n_pool * PAGE * NSA_KV, HEAD_DIM)
    v2 = cache_v.reshape(n_pool * PAGE * NSA_KV, HEAD_DIM)
    pt = page_table.reshape(-1)
    n_keys = P * PAGE
    whole = lambda b, s, pt_ref: (b, 0, 0)

    def page(u):
        return pl.BlockSpec((PAGE * NSA_KV, HEAD_DIM), lambda b, s, pt_ref: (pt_ref[b * n_pages + s * P + u], 0))

    in_specs = [pl.BlockSpec((None, rows, HEAD_DIM), whole)]
    in_specs += [page(u) for u in range(P)] * 2
    args = [q_st.reshape(B, rows, HEAD_DIM)] + [k2] * P + [v2] * P
    if bias is not None:
        cq, cs, cs_tail = bias
        in_specs += [pl.BlockSpec((None, rows, 1), whole),
                     pl.BlockSpec((None, rows, n_keys), lambda b, s, pt_ref: (b, 0, s)),
                     pl.BlockSpec((None, rows, DEC_PAD), whole)]
        args += [cq.reshape(B, rows, 1), cs.reshape(B, rows, -1), cs_tail.reshape(B, rows, DEC_PAD)]
    if sel is not None:
        per_step = n_keys // SLC_BLOCK
        n_half = sel.shape[2]
        in_specs.append(pl.BlockSpec((None, None, rows, LANES), lambda b, s, pt_ref: (b, (s * per_step) // LANES, 0, 0)))
        args.append(sel.transpose(0, 2, 1, 3, 4).reshape(B, n_half, rows, LANES))
    spread = jnp.asarray(np.arange(PAGE)[:, None] == (np.arange(PAGE * NSA_KV)[None, :] >> 1), BF16)
    in_specs += [pl.BlockSpec((PAGE, PAGE * NSA_KV), lambda b, s, pt_ref: (0, 0)),
                 pl.BlockSpec((DEC_PAD, CH), lambda b, s, pt_ref: (b, k_col)),
                 pl.BlockSpec((DEC_PAD, CH), lambda b, s, pt_ref: (b, v_col))]
    args += [spread, big, big]
    out = pl.pallas_call(
        functools.partial(_decode_kernel, P=P, n_tok=n_tok, has_bias=bias is not None, has_sel=sel is not None),
        out_shape=jax.ShapeDtypeStruct((B, rows, HEAD_DIM), F32),
        grid_spec=pltpu.PrefetchScalarGridSpec(
            num_scalar_prefetch=1,
            grid=(B, n_pages // P),
            in_specs=in_specs,
            out_specs=pl.BlockSpec((None, rows, HEAD_DIM), whole),
            scratch_shapes=[pltpu.VMEM((rows, 1), F32), pltpu.VMEM((rows, 1), F32), pltpu.VMEM((rows, HEAD_DIM), F32)],
        ),
        compiler_params=_cparams(("arbitrary", "arbitrary")),
        name=name,
    )(pt, *args)
    return out.reshape(B, NSA_KV, STK, HEAD_DIM)


def _stack_rows(a, n_tok):
    B = a.shape[0]
    x = a.shape[-1] // (NSA_KV * GROUP)
    return a[:, :n_tok].reshape(B, n_tok, NSA_KV, GROUP, x).transpose(0, 2, 3, 1, 4).reshape(B, NSA_KV, STK, x)


def _unstack_rows(o_st, n_tok):
    B = o_st.shape[0]
    o = o_st.reshape(B, NSA_KV, GROUP, n_tok, HEAD_DIM).transpose(0, 3, 1, 2, 4).reshape(B, n_tok, -1)
    return jnp.pad(o, ((0, 0), (0, DEC_PAD - n_tok), (0, 0))).reshape(B * DEC_PAD, -1)


def _pick(n, cands):
    for c in cands:
        if n % c == 0:
            return c
    raise ValueError(f"no tile in {cands} divides {n}")


def _col(big, c, width=CH):
    return big[:, c * CH:c * CH + width]


def _prompt_layer(x, p_i, prm):
    T, D = x.shape
    big = _inproj(x, jnp.arange(T, dtype=jnp.int32), prm["attn_norm"], prm["w_perm"], prm["gain"], prm["bf_pad"],
                  _pick(T, (1024, 512, 256, 128)))
    logf = big[:, C_SM * CH:C_SM * CH + FOX_HEADS]
    nr = T // LANES
    c4, _ = _cumsum(logf.reshape(1, nr, LANES, FOX_HEADS).transpose(0, 1, 3, 2),
                    jnp.zeros((1, FOX_HEADS, DEC_PAD), F32))
    c_rows = c4.transpose(0, 1, 3, 2).reshape(T, FOX_HEADS)

    tq, tk = _pick(T, (1024, 512)), 512
    qm, fa, kf, vf, ks, vs, kw, vw, nrm = _pack(big, c_rows, tk)
    tabs = _prompt_tables(T, tq, tk)
    o_f = _pflash(qm, 0, kf, vf, tabs, tq, tk, aug=fa, aug_mode="head",
                  skip=_fox_skip(tabs, nrm, c_rows, T, tq, tk), name="fox_prompt")

    pk, pv = _cmp_partial_prompt(big, T, prm["w1k"], prm["w1v"], _pick(T // CMP_STRIDE, (256, 128, 64)))
    kc, vc = _cmp_mlp(pk[None], pv[None], prm["b1k"], prm["b1v"], prm["w2k"], prm["w2v"])
    nb = -(-(T // SLC_BLOCK) // LANES) * LANES
    tqc = _pick(T, (256, 128))
    o_c, sel = _cmp_attn(big, C_NQ // 4, 1, T // tqc, tqc, kc, vc, 0, nb)
    o_s = _pflash(qm, 1, ks, vs, tabs, tq, tk, aug=sel[0], aug_mode="group", name="slc_prompt")
    o_w = _pflash(qm, 1, kw, vw, _prompt_tables(T, tk, tk, window=True), tk, tk, name="win_prompt")

    tm = _pick(T, (256, 128))
    h = _mix_out(x, o_f, o_c, o_s, o_w, big, prm["w_out"], tm)
    y = _ple(h, p_i, prm["ple_norm"], prm["w_gate"], prm["w_ple"], tm)
    n_win = min(WINDOW, T)
    kv5 = lambda c: _col(big, c).reshape(1, 1, T, 2, HEAD_DIM)
    state = (kv5(C_FK), kv5(C_FV), logf.reshape(1, 1, T, FOX_HEADS), kv5(C_CK), kv5(C_CV), kv5(C_SK), kv5(C_SV),
             kv5(C_WK)[:, :, T - n_win:], kv5(C_WV)[:, :, T - n_win:])
    return y.reshape(1, T, D), state


def _sample_layer(x, p_i, caches, page_table, prm):
    c_fk, c_fv, c_flogf, c_ck, c_cv, c_sk, c_sv, c_wk, c_wv = caches
    B, Tn, D = x.shape
    n_pages = page_table.shape[1]
    past = n_pages * PAGE
    R = B * DEC_PAD
    xp = jnp.pad(x, ((0, 0), (0, DEC_PAD - Tn), (0, 0))).reshape(R, D)
    pos = jnp.tile(past + jnp.arange(DEC_PAD, dtype=jnp.int32), B)
    big = _inproj(xp, pos, prm["attn_norm"], prm["w_perm"], prm["gain"], prm["bf_pad"], _pick(R, (256, 128, 16)))
    logf = big[:, C_SM * CH:C_SM * CH + FOX_HEADS]

    assert Tn * GROUP == STK, "decode kernel stacks 4 heads x 4 new tokens per KV group"
    P = _pick(n_pages, (16, 8, 4, 2, 1))
    lf_pages = _gather_logf_pages(c_flogf.transpose(0, 2, 1), page_table, _pick(n_pages, (32, 16, 8, 4, 2, 1)))
    c4, c_new = _cumsum(lf_pages, logf.reshape(B, DEC_PAD, FOX_HEADS).transpose(0, 2, 1))
    rep = lambda a: jnp.repeat(a, Tn, axis=2)
    cs_past = rep(c4.transpose(0, 2, 1, 3).reshape(B, FOX_KV, GROUP, past))
    cs_tail = rep(c_new.reshape(B, FOX_KV, GROUP, DEC_PAD))
    cq = c_new[:, :, :Tn].reshape(B, FOX_KV, STK, 1)

    q3 = lambda c: big[:, c * CH:c * CH + 1024].reshape(B, DEC_PAD, 1024)
    o_f = _unstack_rows(
        _decode_attn(_stack_rows(q3(C_FQ), Tn), c_fk, c_fv, page_table, big, C_FK, C_FV, P, Tn,
                     bias=(cq, cs_past, cs_tail), name="fox_decode"), Tn)

    pk, pv = _cmp_partial_paged(c_ck, c_cv, page_table, prm["w1k"], prm["w1v"], _pick(n_pages, (32, 16, 8, 4, 2, 1)))
    kc, vc = _cmp_mlp(pk, pv, prm["b1k"], prm["b1v"], prm["w2k"], prm["w2v"])
    nb = -(-(past // SLC_BLOCK) // LANES) * LANES
    o_c, sel = _cmp_attn(big, C_NQ // 4, B, 1, DEC_PAD, kc, vc, past, nb)
    sel_st = jnp.tile(sel[:, :, :, None, :Tn], (1, 1, 1, GROUP, 1, 1)).reshape(B, NSA_KV, nb // LANES, STK, LANES)
    o_s = _unstack_rows(
        _decode_attn(_stack_rows(q3(C_NQ), Tn), c_sk, c_sv, page_table, big, C_SK, C_SV, P, Tn,
                     sel=sel_st, name="slc_decode"), Tn)
    n_buf = c_wk.shape[1]
    wspec = [pl.BlockSpec((n_buf, HEAD_DIM), lambda b, g, s, *_: (b, g))]
    wflat = lambda c: c.reshape(B * n_buf, NSA_KV * HEAD_DIM)
    o_w = _flash(big, C_NQ // 2, DEC_PAD, 1, B, _linear_tables(1), wspec, wspec, [wflat(c_wk), wflat(c_wv)], n_buf,
                 tail=(big, C_WK * 2, big, C_WV * 2), windowed=True,
                 q_base=past, k_base=past - n_buf, tail_base=past, name="win_decode")

    tm = _pick(R, (256, 128, 16))
    pp = jnp.pad(p_i, ((0, 0), (0, DEC_PAD - Tn), (0, 0))).reshape(R, -1)
    h = _mix_out(xp, o_f, o_c, o_s, o_w, big, prm["w_out"], tm)
    y = _ple(h, pp, prm["ple_norm"], prm["w_gate"], prm["w_ple"], tm)
    y = y.reshape(B, DEC_PAD, D)[:, :Tn]
    new = lambda c: _col(big, c).reshape(B, DEC_PAD, 2, HEAD_DIM)[:, :Tn]
    wk_new, wv_new = new(C_WK), new(C_WV)
    kw = jnp.concatenate([c_wk, wk_new], axis=1)[:, -n_buf:]
    vw = jnp.concatenate([c_wv, wv_new], axis=1)[:, -n_buf:]
    state = (new(C_FK), new(C_FV), logf.reshape(B, DEC_PAD, FOX_HEADS)[:, :Tn], new(C_CK), new(C_CV), new(C_SK),
             new(C_SV), kw, vw)
    return y, tuple(s[None] for s in state)


def kernel(x_prompt, x_sample, cache_fox_k, cache_fox_v, cache_fox_logf, cache_cmp_k, cache_cmp_v, cache_slc_k,
           cache_slc_v, cache_win_k, cache_win_v, page_table, p_prompt, p_sample, attn_norm, w_in, b_forget,
           fox_q_norm, fox_k_norm, nsa_q_norm, nsa_k_norm, cmp_k_w1, cmp_k_b1, cmp_k_w2, cmp_v_w1, cmp_v_b1,
           cmp_v_w2, w_out, ple_norm, w_ple, w_ple_gate):
    assert x_prompt.shape[0] == 1 and w_in.shape[0] == 1, "one prompt sequence, one layer"
    w_perm, gain, bf_pad = _prep_inproj_params(w_in[0], b_forget[0], fox_q_norm[0], fox_k_norm[0], nsa_q_norm[0],
                                               nsa_k_norm[0])
    prm = dict(attn_norm=attn_norm[0], w_perm=w_perm, gain=gain, bf_pad=bf_pad,
               w1k=cmp_k_w1[0].astype(BF16), w1v=cmp_v_w1[0].astype(BF16), b1k=cmp_k_b1[0], b1v=cmp_v_b1[0],
               w2k=cmp_k_w2[0], w2v=cmp_v_w2[0], w_out=w_out[0].astype(BF16), ple_norm=ple_norm[0],
               w_gate=w_ple_gate[0].astype(BF16), w_ple=w_ple[0].astype(BF16))
    y_p, st_p = _prompt_layer(x_prompt[0], p_prompt[0, 0], prm)
    caches = (cache_fox_k[0], cache_fox_v[0], cache_fox_logf[0], cache_cmp_k[0], cache_cmp_v[0], cache_slc_k[0],
              cache_slc_v[0], cache_win_k[0], cache_win_v[0])
    y_s, st_s = _sample_layer(x_sample, p_sample[0], caches, page_table, prm)
    return (y_p, y_s) + tuple(st_p) + tuple(st_s)
```

```python
import functools

import numpy as np
import jax
import jax.numpy as jnp
from jax import lax
from jax.experimental import pallas as pl
from jax.experimental.pallas import tpu as pltpu

F32 = jnp.float32
BF16 = jnp.bfloat16

HEAD_DIM = 128
FOX_HEADS = 8
FOX_KV = 2
NSA_HEADS = 8
NSA_KV = 2
GROUP = 4
N_BRANCH = 3
ROPE_THETA = 500000.0
ROT_DIM = HEAD_DIM // 4
ROT_HALF = ROT_DIM // 2
CMP_STRIDE = 16
CMP_BLOCK = 2 * CMP_STRIDE
SLC_BLOCK = 64
SLC_SHIFT = 6
SLC_TOPN = 16
WINDOW = 512
EPS = 1e-6
NEG_BIG = -1e30
FORCE_BONUS = 1e4
SEL_OFF = -30000.0
PAGE = 128
SCALE = HEAD_DIM ** -0.5

SPLIT_SIZES = (1024, 256, 256, 8, 1024, 1024, 256, 256, 256, 256, 256, 256, 24, 1024)

CH = 256
C_FQ, C_NQ, C_FZ, C_NZ = 0, 4, 8, 12
C_FK, C_CK, C_SK, C_WK, C_FV, C_CV, C_SV, C_WV, C_SM = 16, 17, 18, 19, 20, 21, 22, 23, 24
N_CH = 25
NP = N_CH * CH
LANES = 128
DEC_PAD = 16

VMEM_LIMIT = 56 * 1024 * 1024


def _cparams(sem):
    return pltpu.CompilerParams(dimension_semantics=sem, vmem_limit_bytes=VMEM_LIMIT)


def _sigmoid(x):
    return 1.0 / (1.0 + jnp.exp(-x))


def _silu(x):
    return x * _sigmoid(x)


def _inproj_kernel(x_ref, g_ref, w_ref, gain_ref, bf_ref, rc_ref, rs1_ref, rs2_ref, o_ref, xn_ref, y_ref):
    j = pl.program_id(1)
    c = j - 1
    cur, prev = j & 1, c & 1

    def matmul():
        y_ref[cur] = jnp.dot(xn_ref[...], w_ref[...], preferred_element_type=F32)

    @pl.when(j == 0)
    def _():
        x = x_ref[...]
        ms = jnp.mean(x * x, axis=-1, keepdims=True)
        xn_ref[...] = (x * lax.rsqrt(ms + EPS) * g_ref[...]).astype(BF16)
        matmul()

    is_q_rope = (c >= C_NQ) & (c < C_FZ)
    is_k_rope = (c >= C_CK) & (c <= C_WK)
    is_norm_only = ((c >= 0) & (c < C_NQ)) | (c == C_FK)
    is_rope = is_q_rope | is_k_rope
    is_raw = ((c >= C_FZ) & (c < C_FK)) | ((c >= C_FV) & (c < C_SM))

    def normed(h):
        yh = y_ref[prev, :, h * LANES:(h + 1) * LANES]
        ms = jnp.mean(yh * yh, axis=-1, keepdims=True)
        return yh * lax.rsqrt(ms + EPS) * gain_ref[:, h * LANES:(h + 1) * LANES]

    @pl.when(is_norm_only)
    def _():
        for h in range(CH // LANES):
            o_ref[:, h * LANES:(h + 1) * LANES] = normed(h)
        matmul()

    @pl.when(is_rope)
    def _():
        for h in range(CH // LANES):
            yn = normed(h)
            lo = pltpu.roll(yn, LANES - ROT_HALF, axis=1)
            hi = pltpu.roll(yn, ROT_HALF, axis=1)
            o_ref[:, h * LANES:(h + 1) * LANES] = yn * rc_ref[...] + lo * rs1_ref[...] + hi * rs2_ref[...]
        matmul()

    @pl.when(is_raw)
    def _():
        o_ref[...] = y_ref[prev]
        matmul()

    @pl.when(c == C_SM)
    def _():
        t = y_ref[prev, :, :LANES] + bf_ref[...]
        lane = lax.broadcasted_iota(jnp.int32, t.shape, 1)
        e = jnp.exp(-jnp.abs(t))
        logsig = jnp.minimum(t, 0.0) - jnp.log(1.0 + e)
        o_ref[:, :LANES] = jnp.where(lane < FOX_HEADS, logsig, _sigmoid(t))
        o_ref[:, LANES:] = jnp.zeros_like(t)


def _inproj(x, pos, attn_norm, w_perm, gain, bf_pad, tm):
    R, D = x.shape
    inv = ROPE_THETA ** (-(2.0 / ROT_DIM) * jnp.arange(ROT_HALF, dtype=F32))
    ang = pos.astype(F32)[:, None] * inv[None, :]
    cos, sin = jnp.cos(ang), jnp.sin(ang)
    z = jnp.zeros((R, LANES - ROT_DIM), F32)
    zh = jnp.zeros((R, ROT_HALF), F32)
    rc = jnp.concatenate([cos, cos, z + 1.0], axis=1)
    rs1 = jnp.concatenate([-sin, zh, z], axis=1)
    rs2 = jnp.concatenate([zh, sin, z], axis=1)
    row = lambda i, j: (i, 0)
    return pl.pallas_call(
        _inproj_kernel,
        out_shape=jax.ShapeDtypeStruct((R, NP), F32),
        grid=(R // tm, N_CH + 1),
        in_specs=[
            pl.BlockSpec((tm, D), row),
            pl.BlockSpec((1, D), lambda i, j: (0, 0)),
            pl.BlockSpec((D, CH), lambda i, j: (0, jnp.minimum(j, N_CH - 1))),
            pl.BlockSpec((1, CH), lambda i, j: (0, jnp.maximum(j - 1, 0))),
            pl.BlockSpec((1, LANES), lambda i, j: (0, 0)),
            pl.BlockSpec((tm, LANES), row),
            pl.BlockSpec((tm, LANES), row),
            pl.BlockSpec((tm, LANES), row),
        ],
        out_specs=pl.BlockSpec((tm, CH), lambda i, j: (i, jnp.maximum(j - 1, 0))),
        scratch_shapes=[pltpu.VMEM((tm, D), BF16), pltpu.VMEM((2, tm, CH), F32)],
        compiler_params=_cparams(("arbitrary", "arbitrary")),
        name="inproj",
    )(x, attn_norm.reshape(1, D), w_perm, gain, bf_pad, rc, rs1, rs2)


def _prep_inproj_params(w_in, b_forget, fox_q_norm, fox_k_norm, nsa_q_norm, nsa_k_norm):
    D = w_in.shape[0]
    offs = np.concatenate([[0], np.cumsum(SPLIT_SIZES)]).tolist()
    seg = [w_in[:, offs[i]:offs[i + 1]] for i in range(len(SPLIT_SIZES))]
    fq, fk, fv, flg, fz, nq, ck, cv, sk, sv, wk, wv, ng, nz = seg
    small = jnp.concatenate([flg, ng, jnp.zeros((D, CH - FOX_HEADS - NSA_HEADS * N_BRANCH), F32)], axis=1)
    w_perm = jnp.concatenate([fq, nq, fz, nz, fk, ck, sk, wk, fv, cv, sv, wv, small], axis=1).astype(BF16)
    zeros = lambda n: jnp.zeros((n,), F32)
    gain = jnp.concatenate([
        jnp.tile(fox_q_norm, FOX_HEADS), jnp.tile(nsa_q_norm, NSA_HEADS), zeros(2048),
        jnp.tile(fox_k_norm, FOX_KV), jnp.tile(nsa_k_norm[0], NSA_KV), jnp.tile(nsa_k_norm[1], NSA_KV),
        jnp.tile(nsa_k_norm[2], NSA_KV), zeros(4 * CH + CH)]).reshape(1, NP)
    bf_pad = jnp.concatenate([b_forget, zeros(LANES - FOX_HEADS)]).reshape(1, LANES)
    return w_perm, gain, bf_pad


def _gather_pages_kernel(pt_ref, *refs):
    n = len(refs) - 1
    o_ref = refs[n]
    for u in range(n):
        o_ref[u] = refs[u][...]


def _gather_logf_pages(cache_t, page_table, pages_per_step):
    B, n_pages = page_table.shape
    P = pages_per_step
    pt = page_table.reshape(-1)

    def src_map(u):
        return lambda b, c, pt_ref: (pt_ref[b * n_pages + c * P + u], 0, 0)

    return pl.pallas_call(
        _gather_pages_kernel,
        out_shape=jax.ShapeDtypeStruct((B, n_pages, FOX_HEADS, PAGE), F32),
        grid_spec=pltpu.PrefetchScalarGridSpec(
            num_scalar_prefetch=1,
            grid=(B, n_pages // P),
            in_specs=[pl.BlockSpec((None, FOX_HEADS, PAGE), src_map(u)) for u in range(P)],
            out_specs=pl.BlockSpec((None, P, FOX_HEADS, PAGE), lambda b, c, pt_ref: (b, c, 0, 0)),
        ),
        compiler_params=_cparams(("arbitrary", "arbitrary")),
        name="gather_logf_pages",
    )(pt, *([cache_t] * P))


def _cumsum_kernel(x_ref, new_ref, before_ref, o_ref, onew_ref, *, nr):
    hp = lax.Precision.HIGHEST
    x = x_ref[...].reshape(nr * FOX_HEADS, LANES)
    ci = lax.broadcasted_iota(jnp.int32, (LANES, LANES), 0)
    cj = lax.broadcasted_iota(jnp.int32, (LANES, LANES), 1)
    upper = (ci <= cj).astype(F32)
    lastcol = (ci == LANES - 1).astype(F32)
    within = jnp.dot(x, upper, precision=hp, preferred_element_type=F32)
    rowtot = jnp.dot(within, lastcol, precision=hp, preferred_element_type=F32)
    n = nr * FOX_HEADS
    before = before_ref[...]
    hi = rowtot.astype(BF16)
    r1 = rowtot - hi.astype(F32)
    mid = r1.astype(BF16)
    lo = (r1 - mid.astype(F32)).astype(BF16)
    prefix = (jnp.dot(before, hi, preferred_element_type=F32) + jnp.dot(before, mid, preferred_element_type=F32)
              + jnp.dot(before, lo, preferred_element_type=F32))
    o_ref[...] = (within + prefix).reshape(nr, FOX_HEADS, LANES)
    total = (prefix + rowtot)[n - FOX_HEADS:, :]
    nw = new_ref[...]
    ti = lax.broadcasted_iota(jnp.int32, (DEC_PAD, DEC_PAD), 0)
    tj = lax.broadcasted_iota(jnp.int32, (DEC_PAD, DEC_PAD), 1)
    onew_ref[...] = total[:, :DEC_PAD] + jnp.dot(nw, (ti <= tj).astype(F32), precision=hp,
                                                  preferred_element_type=F32)


def _cumsum(x4, new):
    B, nr = x4.shape[:2]
    n = nr * FOX_HEADS
    r = np.arange(n)
    before = jnp.asarray((r[:, None] % FOX_HEADS == r[None, :] % FOX_HEADS) & (r[None, :] < r[:, None]), BF16)
    return pl.pallas_call(
        functools.partial(_cumsum_kernel, nr=nr),
        out_shape=(jax.ShapeDtypeStruct(x4.shape, F32), jax.ShapeDtypeStruct(new.shape, F32)),
        grid=(B,),
        in_specs=[pl.BlockSpec((None, nr, FOX_HEADS, LANES), lambda b: (b, 0, 0, 0)),
                  pl.BlockSpec((None, FOX_HEADS, DEC_PAD), lambda b: (b, 0, 0)),
                  pl.BlockSpec((n, n), lambda b: (0, 0))],
        out_specs=(pl.BlockSpec((None, nr, FOX_HEADS, LANES), lambda b: (b, 0, 0, 0)),
                   pl.BlockSpec((None, FOX_HEADS, DEC_PAD), lambda b: (b, 0, 0))),
        compiler_params=_cparams(("arbitrary",)),
        name="cumsum_logf",
    )(x4, new, before)


def _flash_kernel(*refs, n_sp, n_src, tq, n_keys, has_bias, has_sel, has_tail, windowed,
                  q_base, k_base, tail_base):
    qi_tab, kj_tab, first_tab, last_tab = refs[:4]
    refs = refs[n_sp:]
    q_ref = refs[0]
    k_srcs = refs[1:1 + n_src]
    v_srcs = refs[1 + n_src:1 + 2 * n_src]
    pos = 1 + 2 * n_src
    cq_ref = cs_ref = sel_ref = kt_ref = vt_ref = cst_ref = None
    if has_bias:
        cq_ref, cs_ref = refs[pos], refs[pos + 1]
        pos += 2
    if has_sel:
        sel_ref = refs[pos]
        pos += 1
    if has_tail:
        kt_ref, vt_ref = refs[pos], refs[pos + 1]
        pos += 2
        if has_bias:
            cst_ref = refs[pos]
            pos += 1
    o_ref = refs[pos]
    qs_ref, m_ref, l_ref, acc_ref = refs[pos + 1:pos + 5]

    step = pl.program_id(2)
    g = pl.program_id(1)
    qi = qi_tab[step]
    kj = kj_tab[step]

    @pl.when(first_tab[step] == 1)
    def _():
        qs_ref[...] = (q_ref[...] * SCALE).astype(BF16)
        m_ref[...] = jnp.full(m_ref.shape, NEG_BIG, F32)
        l_ref[...] = jnp.zeros(l_ref.shape, F32)
        acc_ref[...] = jnp.zeros(acc_ref.shape, F32)

    q_pos = q_base + qi * tq + lax.broadcasted_iota(jnp.int32, (tq, 1), 0)

    def attend(k, v, k_pos, cs, sel_bias):
        kb = k.astype(BF16)
        vb = v.astype(BF16)
        mask = k_pos <= q_pos
        if windowed:
            mask = mask & ((q_pos - k_pos) < WINDOW)
        for h in range(GROUP):
            s = lax.dot_general(qs_ref[:, h * HEAD_DIM:(h + 1) * HEAD_DIM], kb, (((1,), (1,)), ((), ())),
                                preferred_element_type=F32)
            if cs is not None:
                hh = g * GROUP + h
                cq = jnp.sum(jnp.where(lax.broadcasted_iota(jnp.int32, cq_ref.shape, 1) == hh, cq_ref[...], 0.0),
                             axis=1, keepdims=True)
                s = s + (cq - cs[h:h + 1, :])
            if sel_bias is not None:
                s = s + sel_bias
            s = jnp.where(mask, s, NEG_BIG)
            m_prev = m_ref[h]
            m_new = jnp.maximum(m_prev, jnp.max(s, axis=-1, keepdims=True))
            alpha = jnp.exp(m_prev - m_new)
            p = jnp.exp(s - m_new)
            l_ref[h] = alpha * l_ref[h] + jnp.sum(p, axis=-1, keepdims=True)
            acc_ref[h] = alpha * acc_ref[h] + jnp.dot(p.astype(BF16), vb, preferred_element_type=F32)
            m_ref[h] = m_new

    if n_src == 1:
        k, v = k_srcs[0][...], v_srcs[0][...]
    else:
        k = jnp.concatenate([r[...] for r in k_srcs], axis=0)
        v = jnp.concatenate([r[...] for r in v_srcs], axis=0)
    k_pos = k_base + kj * n_keys + lax.broadcasted_iota(jnp.int32, (1, n_keys), 1)
    sel_bias = None
    if has_sel:
        blk = ((kj * n_keys + lax.broadcasted_iota(jnp.int32, (1, n_keys), 1)) >> SLC_SHIFT) & (LANES - 1)
        onehot = (lax.broadcasted_iota(jnp.int32, (LANES, n_keys), 0) == blk).astype(BF16)
        sel_bias = jnp.dot(sel_ref[...], onehot, preferred_element_type=F32)
    attend(k, v, k_pos, cs_ref[...] if has_bias else None, sel_bias)

    @pl.when(last_tab[step] == 1)
    def _():
        if has_tail:
            t_pos = tail_base + lax.broadcasted_iota(jnp.int32, (1, DEC_PAD), 1)
            attend(kt_ref[...], vt_ref[...], t_pos, cst_ref[...] if has_bias else None, None)
        for h in range(GROUP):
            o_ref[:, h * HEAD_DIM:(h + 1) * HEAD_DIM] = acc_ref[h] / jnp.maximum(l_ref[h], 1e-30)


def _flash(q_arr, q_col, tq, n_qblk, B, tables, k_specs, v_specs, kv_args, n_keys, *, prefetch=(),
           bias=None, sel=None, tail=None, windowed=False, q_base=0, k_base=0, tail_base=0, name="flash"):
    n_sp = len(tables) + len(prefetch)
    n_steps = tables[0].shape[0]
    n_src = len(k_specs)
    qmap = lambda b, g, s, qi, *_: (b * n_qblk + qi[s], q_col + g)
    in_specs = [pl.BlockSpec((tq, GROUP * HEAD_DIM), qmap)] + list(k_specs) + list(v_specs)
    args = [q_arr] + list(kv_args)
    if bias is not None:
        in_specs.append(pl.BlockSpec((tq, FOX_HEADS), lambda b, g, s, qi, *_: (b * n_qblk + qi[s], 0)))
        in_specs.append(pl.BlockSpec((None, None, GROUP, n_keys), lambda b, g, s, qi, kj, *_: (b, g, 0, kj[s])))
        args += [bias[0], bias[1]]
    if sel is not None:
        blocks_per_step = n_keys // SLC_BLOCK
        in_specs.append(pl.BlockSpec(
            (None, None, None, tq, LANES),
            lambda b, g, s, qi, kj, *_: (b, g, (kj[s] * blocks_per_step) // LANES, qi[s], 0)))
        args.append(sel)
    if tail is not None:
        k_arr, k_col, v_arr, v_col = tail
        in_specs.append(pl.BlockSpec((DEC_PAD, HEAD_DIM), lambda b, g, s, *_: (b, k_col + g)))
        in_specs.append(pl.BlockSpec((DEC_PAD, HEAD_DIM), lambda b, g, s, *_: (b, v_col + g)))
        args += [k_arr, v_arr]
        if bias is not None:
            in_specs.append(pl.BlockSpec((None, None, GROUP, DEC_PAD), lambda b, g, s, *_: (b, g, 0, 0)))
            args.append(bias[2])
    kern = functools.partial(
        _flash_kernel, n_sp=n_sp, n_src=n_src, tq=tq, n_keys=n_keys, has_bias=bias is not None,
        has_sel=sel is not None, has_tail=tail is not None, windowed=windowed,
        q_base=q_base, k_base=k_base, tail_base=tail_base)
    return pl.pallas_call(
        kern,
        out_shape=jax.ShapeDtypeStruct((B * n_qblk * tq, 2 * GROUP * HEAD_DIM), F32),
        grid_spec=pltpu.PrefetchScalarGridSpec(
            num_scalar_prefetch=n_sp,
            grid=(B, 2, n_steps),
            in_specs=in_specs,
            out_specs=pl.BlockSpec((tq, GROUP * HEAD_DIM), lambda b, g, s, qi, *_: (b * n_qblk + qi[s], g)),
            scratch_shapes=[pltpu.VMEM((tq, GROUP * HEAD_DIM), BF16),
                            pltpu.VMEM((GROUP, tq, 1), F32),
                            pltpu.VMEM((GROUP, tq, 1), F32),
                            pltpu.VMEM((GROUP, tq, HEAD_DIM), F32)],
        ),
        compiler_params=_cparams(("arbitrary", "arbitrary", "arbitrary")),
        name=name,
    )(*tables, *prefetch, *args)


def _causal_tables(n_blk, lookback=None):
    qi, kj, first, last = [], [], [], []
    for i in range(n_blk):
        lo = 0 if lookback is None else max(0, i - lookback)
        for j in range(lo, i + 1):
            qi.append(i)
            kj.append(j)
            first.append(int(j == lo))
            last.append(int(j == i))
    return tuple(jnp.asarray(a, jnp.int32) for a in (qi, kj, first, last))


def _linear_tables(n_steps):
    z = np.zeros((n_steps,), np.int32)
    first, last = z.copy(), z.copy()
    first[0], last[-1] = 1, 1
    return tuple(jnp.asarray(a, jnp.int32) for a in (z, np.arange(n_steps, dtype=np.int32), first, last))


LOG2E = 1.4426950408889634
AUG = 128
N_SPLIT = 3
AUG_C0 = N_SPLIT
NRM_ROWS = 16
UNDERFLOW_LOG2 = -160.0


def _split_part(x, part):
    hi = x.astype(BF16)
    r1 = x - hi.astype(F32)
    mid = r1.astype(BF16)
    lo = (r1 - mid.astype(F32)).astype(BF16)
    return jnp.where(part == 0, hi, jnp.where(part == 1, mid, lo))


def _pack_kernel(big_q, big_kv, c_ref, qm_ref, fa_ref, kf_ref, vf_ref, ks_ref, vs_ref, kw_ref, vw_ref, nrm_ref, *, tr):
    i = pl.program_id(0)
    lane = lax.broadcasted_iota(jnp.int32, (tr, AUG), 1)
    in_c = (lane >= AUG_C0) & (lane < AUG_C0 + GROUP * N_SPLIT)
    slot = jnp.zeros_like(lane)
    for j in range(1, GROUP):
        slot = slot + jnp.where(lane >= AUG_C0 + j * N_SPLIT, 1, 0)
    part = jnp.where(lane < AUG_C0, lane, lane - AUG_C0 - N_SPLIT * slot)
    c2 = c_ref[...] * LOG2E
    ones_lane = jnp.where(lane == 0, 1.0, 0.0).astype(BF16)
    zeros = jnp.zeros((tr, AUG), BF16)

    def max_sq_norm(x):
        xf = x.astype(F32)
        n2 = jnp.max(jnp.sum(xf * xf, axis=-1, keepdims=True), axis=0, keepdims=True)
        return jnp.broadcast_to(n2, (1, LANES))

    qmb = (big_q[...] * (SCALE * LOG2E)).astype(BF16)
    qm_ref[...] = qmb
    nrm_ref[...] = jnp.zeros(nrm_ref.shape, F32)
    for h in range(FOX_HEADS):
        nrm_ref[h:h + 1, :] = max_sq_norm(qmb[:, h * HEAD_DIM:(h + 1) * HEAD_DIM])
        cq = jnp.where(lane < AUG_C0, c2[:, h:h + 1], 0.0)
        own = in_c & (slot == h % GROUP)
        fa_ref[:, h * AUG:(h + 1) * AUG] = jnp.where(own, -1.0, _split_part(cq, part).astype(F32)).astype(BF16)

    tok = i * tr + lax.broadcasted_iota(jnp.int32, (tr, AUG), 0)
    blk_onehot = jnp.where(lane == ((tok >> SLC_SHIFT) & (AUG - 1)), 1.0, 0.0).astype(BF16)
    for g in range(NSA_KV):
        cs = jnp.zeros((tr, AUG), F32)
        for j in range(GROUP):
            cs = jnp.where(in_c & (slot == j), c2[:, g * GROUP + j:g * GROUP + j + 1], cs)
        k_aug = jnp.where(lane < AUG_C0, 1.0, _split_part(cs, part).astype(F32)).astype(BF16)
        col = lambda c: big_kv[:, (c - C_FK) * CH + g * HEAD_DIM:(c - C_FK) * CH + (g + 1) * HEAD_DIM].astype(BF16)
        lo, hi = g * 2 * HEAD_DIM, g * 2 * HEAD_DIM + HEAD_DIM
        for k_out, v_out, kc, vc, aug in ((kf_ref, vf_ref, C_FK, C_FV, k_aug), (ks_ref, vs_ref, C_SK, C_SV, blk_onehot),
                                          (kw_ref, vw_ref, C_WK, C_WV, zeros)):
            k_out[:, lo:hi] = col(kc)
            if kc == C_FK:
                nrm_ref[FOX_HEADS + g:FOX_HEADS + g + 1, :] = max_sq_norm(col(kc))
            k_out[:, hi:hi + AUG] = aug
            v_out[:, lo:hi] = col(vc)
            v_out[:, hi:hi + AUG] = ones_lane


def _pack(big, c_rows, tr):
    T = big.shape[0]
    row = lambda i: (i, 0)
    kv = jax.ShapeDtypeStruct((T, NSA_KV * (HEAD_DIM + AUG)), BF16)
    kvspec = pl.BlockSpec((tr, NSA_KV * (HEAD_DIM + AUG)), row)
    return pl.pallas_call(
        functools.partial(_pack_kernel, tr=tr),
        out_shape=(jax.ShapeDtypeStruct((T, 2048), BF16), jax.ShapeDtypeStruct((T, FOX_HEADS * AUG), BF16)) + (kv,) * 6
        + (jax.ShapeDtypeStruct((T // tr, NRM_ROWS, LANES), F32),),
        grid=(T // tr,),
        in_specs=[pl.BlockSpec((tr, 2048), row),
                  pl.BlockSpec((tr, 2048), lambda i: (i, C_FK // 8)),
                  pl.BlockSpec((tr, FOX_HEADS), row)],
        out_specs=(pl.BlockSpec((tr, 2048), row), pl.BlockSpec((tr, FOX_HEADS * AUG), row)) + (kvspec,) * 6
        + (pl.BlockSpec((None, NRM_ROWS, LANES), lambda i: (i, 0, 0)),),
        compiler_params=_cparams(("arbitrary",)),
        name="pack_qkv",
    )(big, big, c_rows)


def _pflash_kernel(qi_tab, kj_tab, first_tab, last_tab, kind_tab, aug_tab, skip_tab, kjd_tab, *refs, tq, tk,
                   aug_mode):
    qm_ref, k_ref, v_ref = refs[:3]
    pos = 3
    a_ref = None
    if aug_mode != "none":
        a_ref = refs[pos]
        pos += 1
    o_ref, qs_ref, m_ref, acc_ref = refs[pos:pos + 4]
    step = pl.program_id(1)
    active = skip_tab[pl.program_id(0) * pl.num_programs(1) + step] == 0

    @pl.when(first_tab[step] == 1)
    def _():
        for h in range(GROUP):
            qs_ref[h, :, :HEAD_DIM] = qm_ref[:, h * HEAD_DIM:(h + 1) * HEAD_DIM]
            if aug_mode == "head":
                qs_ref[h, :, HEAD_DIM:] = a_ref[:, h * AUG:(h + 1) * AUG]
            elif aug_mode == "none":
                qs_ref[h, :, HEAD_DIM:] = jnp.zeros((tq, AUG), BF16)
        m_ref[...] = jnp.full(m_ref.shape, NEG_BIG, F32)
        acc_ref[...] = jnp.zeros(acc_ref.shape, F32)

    if aug_mode == "group":
        @pl.when(aug_tab[step] == 1)
        def _():
            for h in range(GROUP):
                qs_ref[h, :, HEAD_DIM:] = a_ref[...]

    def attend(kind):
        kb = k_ref[...]
        vb = v_ref[...]
        if kind:
            ahead = (qi_tab[step] * tq - kj_tab[step] * tk + lax.broadcasted_iota(jnp.int32, (tq, tk), 0)
                     - lax.broadcasted_iota(jnp.int32, (tq, tk), 1))
            keep = (ahead >= 0) if kind == 1 else ((ahead >= 0) & (ahead < WINDOW))
        for h in range(GROUP):
            s = lax.dot_general(qs_ref[h], kb, (((1,), (1,)), ((), ())), preferred_element_type=F32)
            if kind:
                s = jnp.where(keep, s, NEG_BIG)
            m_prev = m_ref[h]
            m_tile = s[:, :LANES]
            for c in range(1, tk // LANES):
                m_tile = jnp.maximum(m_tile, s[:, c * LANES:(c + 1) * LANES])
            m_new = jnp.maximum(m_prev, jnp.max(m_tile, axis=-1, keepdims=True))
            p = jnp.exp2(s - jnp.tile(m_new, (1, tk // LANES))).astype(BF16)
            alpha = jnp.exp2(m_prev - m_new)
            acc_ref[h] = jnp.tile(alpha, (1, 2)) * acc_ref[h] + jnp.dot(p, vb, preferred_element_type=F32)
            m_ref[h] = m_new

    for kind in range(3):
        pl.when((kind_tab[step] == kind) & active)(functools.partial(attend, kind))

    @pl.when(last_tab[step] == 1)
    def _():
        for h in range(GROUP):
            acc = acc_ref[h]
            o_ref[:, h * HEAD_DIM:(h + 1) * HEAD_DIM] = acc[:, :HEAD_DIM] / jnp.maximum(
                acc[:, HEAD_DIM:HEAD_DIM + 1], 1e-30)


def _pflash(qm, q_col, k_arr, v_arr, tables, tq, tk, *, aug=None, aug_mode="none", skip=None, name):
    T = qm.shape[0]
    n_steps = tables[0].shape[0]
    kvw = HEAD_DIM + AUG
    in_specs = [pl.BlockSpec((tq, GROUP * HEAD_DIM), lambda g, s, qi, *_: (qi[s], q_col * 2 + g)),
                pl.BlockSpec((tk, kvw), lambda g, s, *tabs: (tabs[7][g * n_steps + s], g)),
                pl.BlockSpec((tk, kvw), lambda g, s, *tabs: (tabs[7][g * n_steps + s], g))]
    args = [qm, k_arr, v_arr]
    if aug_mode == "head":
        in_specs.append(pl.BlockSpec((tq, GROUP * AUG), lambda g, s, qi, *_: (qi[s], g)))
        args.append(aug)
    elif aug_mode == "group":
        per_step = tk // SLC_BLOCK
        in_specs.append(pl.BlockSpec((None, None, tq, AUG),
                                     lambda g, s, qi, kj, *_: (g, (kj[s] * per_step) // AUG, qi[s], 0)))
        args.append(aug)
    return pl.pallas_call(
        functools.partial(_pflash_kernel, tq=tq, tk=tk, aug_mode=aug_mode),
        out_shape=jax.ShapeDtypeStruct((T, 2 * GROUP * HEAD_DIM), F32),
        grid_spec=pltpu.PrefetchScalarGridSpec(
            num_scalar_prefetch=8,
            grid=(2, n_steps),
            in_specs=in_specs,
            out_specs=pl.BlockSpec((tq, GROUP * HEAD_DIM), lambda g, s, qi, *_: (qi[s], g)),
            scratch_shapes=[pltpu.VMEM((GROUP, tq, kvw), BF16),
                            pltpu.VMEM((GROUP, tq, LANES), F32),
                            pltpu.VMEM((GROUP, tq, kvw), F32)],
        ),
        compiler_params=_cparams(("arbitrary", "arbitrary")),
        name=name,
    )(*(jnp.asarray(t) for t in tables), *(skip if skip is not None else _no_skip(tables)), *args)


def _fox_skip(tables, nrm, c_rows, T, tq, tk):
    qi, kj, last = np.asarray(tables[0]), np.asarray(tables[1]), np.asarray(tables[3])
    nq, nt, r = T // tq, T // tk, tq // tk
    qn = jnp.sqrt(nrm[:, :FOX_HEADS, 0].reshape(nq, r, FOX_HEADS).max(axis=1))
    kn = jnp.sqrt(nrm[:, FOX_HEADS:FOX_HEADS + FOX_KV, 0])
    kn_own = kn.reshape(nq, r, FOX_KV).max(axis=1)
    c2 = c_rows * LOG2E
    c_first_q, c_last_k = c2[::tq], c2[tk - 1::tk]
    rep = lambda a: jnp.repeat(a, GROUP, axis=-1)
    logit_max = qn[qi] * rep(kn[kj]) + (c_first_q[qi] - c_last_k[kj])
    max_lb = -qn[qi] * rep(kn_own[qi])
    before = jnp.asarray((kj + 1) * tk - 1 < qi * tq)[:, None]
    dead = (before & (logit_max < max_lb + UNDERFLOW_LOG2)).reshape(-1, FOX_KV, GROUP).all(axis=-1)
    last_of_block = np.zeros_like(kj)
    nxt = 0
    for s in range(len(kj) - 1, -1, -1):
        if last[s] == 1:
            nxt = kj[s]
        last_of_block[s] = nxt
    skip = dead.T.astype(jnp.int32).reshape(-1)
    kjd = jnp.where(dead.T, jnp.asarray(last_of_block)[None, :], jnp.asarray(kj)[None, :]).astype(jnp.int32)
    return skip, kjd.reshape(-1)


def _no_skip(tables):
    n = tables[1].shape[0]
    return jnp.zeros((2 * n,), jnp.int32), jnp.asarray(np.tile(tables[1], 2))


def _prompt_tables(T, tq, tk, window=False):
    per_half = AUG * SLC_BLOCK // tk
    rows = []
    for i in range(T // tq):
        q_lo, q_hi = i * tq, (i + 1) * tq - 1
        lo = max(0, q_lo - WINDOW + 1) // tk if window else 0
        hi = q_hi // tk
        for j in range(lo, hi + 1):
            crosses = (j + 1) * tk - 1 > q_lo
            kind = 2 if window else int(crosses)
            rows.append((i, j, int(j == lo), int(j == hi), kind, int(j == lo or j % per_half == 0)))
    return tuple(np.asarray(a, np.int32) for a in zip(*rows))


def _cmp_cat(w1):
    return jnp.concatenate([w1[:CMP_STRIDE], w1[CMP_STRIDE:]], axis=-1)


def _cmp_partial_prompt_kernel(xk0_ref, xk1_ref, xv0_ref, xv1_ref, wk_ref, wv_ref, ok_ref, ov_ref, *, tn):
    for x_refs, w_ref, o_ref in (((xk0_ref, xk1_ref), wk_ref, ok_ref), ((xv0_ref, xv1_ref), wv_ref, ov_ref)):
        for kv in range(NSA_KV):
            acc = None
            for r in range(CMP_STRIDE):
                x = x_refs[kv][pl.ds(r, tn, stride=CMP_STRIDE), :]
                d = jnp.dot(x.astype(BF16), w_ref[r], preferred_element_type=F32)
                acc = d if acc is None else acc + d
            o_ref[:, kv * HEAD_DIM:(kv + 1) * HEAD_DIM] = acc[:, :HEAD_DIM]
            o_ref[:, (NSA_KV + kv) * HEAD_DIM:(NSA_KV + kv + 1) * HEAD_DIM] = acc[:, HEAD_DIM:]


def _cmp_partial_prompt(big, T, w1k, w1v, tn):
    ns = T // CMP_STRIDE
    wspec = pl.BlockSpec((CMP_STRIDE, HEAD_DIM, 2 * HEAD_DIM), lambda i: (0, 0, 0))
    out = jax.ShapeDtypeStruct((ns, 4 * HEAD_DIM), F32)
    return pl.pallas_call(
        functools.partial(_cmp_partial_prompt_kernel, tn=tn),
        out_shape=(out, out),
        grid=(ns // tn,),
        in_specs=[pl.BlockSpec((tn * CMP_STRIDE, HEAD_DIM), (lambda c: (lambda i: (i, c)))(c))
                  for c in (2 * C_CK, 2 * C_CK + 1, 2 * C_CV, 2 * C_CV + 1)] + [wspec, wspec],
        out_specs=(pl.BlockSpec((tn, 4 * HEAD_DIM), lambda i: (i, 0)),) * 2,
        compiler_params=_cparams(("arbitrary",)),
        name="cmp_partial_prompt",
    )(big, big, big, big, _cmp_cat(w1k), _cmp_cat(w1v))


def _cmp_partial_paged_kernel(pt_ref, *refs, n_pages):
    xk, xv = refs[:n_pages], refs[n_pages:2 * n_pages]
    wk_ref, wv_ref, ok_ref, ov_ref = refs[2 * n_pages:]
    sub = PAGE // CMP_STRIDE
    rows_per_sub = CMP_STRIDE * NSA_KV
    for xs, w_ref, o_ref in ((xk, wk_ref, ok_ref), (xv, wv_ref, ov_ref)):
        x_all = jnp.concatenate([p[...] for p in xs], axis=0).reshape(n_pages * sub, rows_per_sub, HEAD_DIM)
        by_row = jnp.swapaxes(x_all, 0, 1).astype(BF16)
        for kv in range(NSA_KV):
            acc = None
            for r in range(CMP_STRIDE):
                d = jnp.dot(by_row[NSA_KV * r + kv], w_ref[r], preferred_element_type=F32)
                acc = d if acc is None else acc + d
            o_ref[:, kv * HEAD_DIM:(kv + 1) * HEAD_DIM] = acc[:, :HEAD_DIM]
            o_ref[:, (NSA_KV + kv) * HEAD_DIM:(NSA_KV + kv + 1) * HEAD_DIM] = acc[:, HEAD_DIM:]


def _cmp_partial_paged(cache_k, cache_v, page_table, w1k, w1v, pages_per_step):
    B, n_pages = page_table.shape
    P = pages_per_step
    sub = PAGE // CMP_STRIDE
    n_pool = cache_k.shape[0]
    vk = cache_k.reshape(n_pool * PAGE * NSA_KV, HEAD_DIM)
    vv = cache_v.reshape(n_pool * PAGE * NSA_KV, HEAD_DIM)
    pt = page_table.reshape(-1)

    def src(u):
        return pl.BlockSpec((PAGE * NSA_KV, HEAD_DIM), lambda b, c, pt_ref: (pt_ref[b * n_pages + c * P + u], 0))

    wspec = pl.BlockSpec((CMP_STRIDE, HEAD_DIM, 2 * HEAD_DIM), lambda b, c, pt_ref: (0, 0, 0))
    out = jax.ShapeDtypeStruct((B, n_pages * sub, 4 * HEAD_DIM), F32)
    ospec = pl.BlockSpec((None, P * sub, 4 * HEAD_DIM), lambda b, c, pt_ref: (b, c, 0))
    return pl.pallas_call(
        functools.partial(_cmp_partial_paged_kernel, n_pages=P),
        out_shape=(out, out),
        grid_spec=pltpu.PrefetchScalarGridSpec(
            num_scalar_prefetch=1,
            grid=(B, n_pages // P),
            in_specs=[src(u) for u in range(P)] * 2 + [wspec, wspec],
            out_specs=(ospec, ospec),
        ),
        compiler_params=_cparams(("arbitrary", "arbitrary")),
        name="cmp_partial_paged",
    )(pt, *([vk] * P), *([vv] * P), _cmp_cat(w1k), _cmp_cat(w1v))


def _cmp_mlp_kernel(pk_ref, pv_ref, b1k_ref, b1v_ref, w2k_ref, w2v_ref, ok_ref, ov_ref, *, ns):
    for p_ref, b_ref, w_ref, o_ref in ((pk_ref, b1k_ref, w2k_ref, ok_ref), (pv_ref, b1v_ref, w2v_ref, ov_ref)):
        for kv in range(NSA_KV):
            p0 = p_ref[:, kv * HEAD_DIM:(kv + 1) * HEAD_DIM]
            p1 = p_ref[:, (NSA_KV + kv) * HEAD_DIM:(NSA_KV + kv + 1) * HEAD_DIM]
            nxt = pltpu.roll(p1, ns - 1, axis=0)
            h = _silu(p0 + nxt + b_ref[...])
            o_ref[:, kv * HEAD_DIM:(kv + 1) * HEAD_DIM] = jnp.dot(h.astype(BF16), w_ref[...],
                                                                  preferred_element_type=F32).astype(BF16)


def _cmp_mlp(pk, pv, b1k, b1v, w2k, w2v):
    B, ns, _ = pk.shape
    pspec = pl.BlockSpec((None, ns, 4 * HEAD_DIM), lambda b: (b, 0, 0))
    bspec = pl.BlockSpec((1, HEAD_DIM), lambda b: (0, 0))
    wspec = pl.BlockSpec((HEAD_DIM, HEAD_DIM), lambda b: (0, 0))
    out = jax.ShapeDtypeStruct((B, ns, NSA_KV * HEAD_DIM), BF16)
    ospec = pl.BlockSpec((None, ns, NSA_KV * HEAD_DIM), lambda b: (b, 0, 0))
    return pl.pallas_call(
        functools.partial(_cmp_mlp_kernel, ns=ns),
        out_shape=(out, out),
        grid=(B,),
        in_specs=[pspec, pspec, bspec, bspec, wspec, wspec],
        out_specs=(ospec, ospec),
        compiler_params=_cparams(("arbitrary",)),
        name="cmp_mlp",
    )(pk, pv, b1k.reshape(1, -1), b1v.reshape(1, -1), w2k.astype(BF16), w2v.astype(BF16))


def _cmp_attn_kernel(q_ref, kc_ref, vc_ref, ov_ref, o_ref, sel_ref, score_ref, *, tq, nbat, ns, nb, q_base, widths):
    qi = pl.program_id(1)
    rows = nbat * tq
    q_pos = q_base + qi * tq + (lax.broadcasted_iota(jnp.int32, (rows, 1), 0) & (tq - 1))
    n_idx = lax.broadcasted_iota(jnp.int32, (1, ns), 1)
    cmp_end = n_idx * CMP_STRIDE + (CMP_BLOCK - 1)
    mask = (cmp_end <= q_pos) & (n_idx < ns - 1)
    overlap = ov_ref[...]
    blk = lax.broadcasted_iota(jnp.int32, (rows, nb), 1)
    blkf = blk.astype(F32)
    cur = q_pos >> SLC_SHIFT
    forced = (blk == 0) | (blk == cur) | (blk == cur - 1)
    valid = blk <= cur
    n_pick = SLC_TOPN - jnp.where(cur >= nb, 1, 0)

    def attend(w):
        for b in range(nbat):
            rs = slice(b * tq, (b + 1) * tq)
            for g in range(NSA_KV):
                kb = kc_ref[b, :w, g * HEAD_DIM:(g + 1) * HEAD_DIM]
                vb = vc_ref[b, :w, g * HEAD_DIM:(g + 1) * HEAD_DIM]
                imp = jnp.zeros((tq, w), F32)
                for h in range(GROUP):
                    c = (g * GROUP + h) * HEAD_DIM
                    qh = (q_ref[rs, c:c + HEAD_DIM] * (SCALE * LOG2E)).astype(BF16)
                    s = lax.dot_general(qh, kb, (((1,), (1,)), ((), ())), preferred_element_type=F32)
                    s = jnp.where(mask[rs, :w], s, NEG_BIG)
                    m = jnp.max(s, axis=-1, keepdims=True)
                    e = jnp.exp2(s - m)
                    any_visible = jnp.where(m > 0.5 * NEG_BIG, 1.0, 0.0)
                    p = e * (any_visible / jnp.maximum(jnp.sum(e, axis=-1, keepdims=True), 1e-30))
                    o_ref[rs, c:c + HEAD_DIM] = jnp.dot(p.astype(BF16), vb, preferred_element_type=F32)
                    imp = imp + p
                hi = imp.astype(BF16)
                r1 = imp - hi.astype(F32)
                mid = r1.astype(BF16)
                lo = (r1 - mid.astype(F32)).astype(BF16)
                ov = overlap[:w]
                imp_slc = (jnp.dot(hi, ov, preferred_element_type=F32) + jnp.dot(mid, ov, preferred_element_type=F32)
                           + jnp.dot(lo, ov, preferred_element_type=F32))
                score_ref[g, rs] = jnp.where(valid[rs], imp_slc + FORCE_BONUS * forced[rs].astype(F32), NEG_BIG)

    n_vis = (q_base + (qi + 1) * tq - CMP_BLOCK) // CMP_STRIDE + 1
    lo_w = -(2 ** 30)
    for w in widths:
        pl.when((n_vis > lo_w) & ((n_vis <= w) | (w == ns)))(functools.partial(attend, w))
        lo_w = w
    scores = [score_ref[g] for g in range(NSA_KV)]

    def pick_one(score, allowed):
        best = jnp.max(score, axis=-1, keepdims=True)
        first = jnp.min(jnp.where(score == best, blkf, float(nb)), axis=-1, keepdims=True)
        return jnp.where((blkf == first) & allowed, -jnp.inf, score)

    picked = lax.fori_loop(0, SLC_TOPN - 1, lambda it, sc: tuple(pick_one(s, True) for s in sc), tuple(scores))
    picked = [pick_one(s, n_pick >= SLC_TOPN) for s in picked]
    for g in range(NSA_KV):
        bias = jnp.where((picked[g] == -jnp.inf) & valid, 0.0, SEL_OFF).astype(BF16)
        for b in range(nbat):
            for half in range(nb // LANES):
                sel_ref[b, g, half] = bias[b * tq:(b + 1) * tq, half * LANES:(half + 1) * LANES]


def _cmp_attn(q_arr, q_col, B, n_qblk, tq, kc, vc, q_base, nb):
    ns = kc.shape[1]
    per = SLC_BLOCK // CMP_STRIDE
    ci = np.arange(ns)[:, None]
    cb = np.arange(nb)[None, :]
    overlap = jnp.asarray((ci >= per * cb - 1) & (ci <= per * cb + per - 1), BF16)
    assert tq & (tq - 1) == 0
    nbat = _pick(B, (4, 2, 1)) if n_qblk == 1 else 1
    kspec = pl.BlockSpec((nbat, ns, NSA_KV * HEAD_DIM), lambda b, i: (b, 0, 0))
    quarter = ns // 4
    widths = (quarter, 2 * quarter, 3 * quarter, ns) if (n_qblk > 1 and quarter % LANES == 0) else (ns,)
    return pl.pallas_call(
        functools.partial(_cmp_attn_kernel, tq=tq, nbat=nbat, ns=ns, nb=nb, q_base=q_base, widths=widths),
        out_shape=(jax.ShapeDtypeStruct((B * n_qblk * tq, NSA_HEADS * HEAD_DIM), F32),
                   jax.ShapeDtypeStruct((B, NSA_KV, nb // LANES, n_qblk * tq, LANES), BF16)),
        grid=(B // nbat, n_qblk),
        in_specs=[pl.BlockSpec((nbat * tq, NSA_HEADS * HEAD_DIM), lambda b, i: (b * n_qblk + i, q_col)), kspec, kspec,
                  pl.BlockSpec((ns, nb), lambda b, i: (0, 0))],
        out_specs=(pl.BlockSpec((nbat * tq, NSA_HEADS * HEAD_DIM), lambda b, i: (b * n_qblk + i, 0)),
                   pl.BlockSpec((nbat, NSA_KV, nb // LANES, tq, LANES), lambda b, i: (b, 0, 0, i, 0))),
        scratch_shapes=[pltpu.VMEM((NSA_KV, nbat * tq, nb), F32)],
        compiler_params=_cparams(("arbitrary", "arbitrary")),
        name="cmp_attn_select",
    )(q_arr, kc, vc, overlap)


def _mix_out_kernel(x_ref, of_ref, oc_ref, os_ref, ow_ref, sm_ref, fz_ref, nz_ref, w_ref, h_ref):
    mix_f = of_ref[...] * _silu(fz_ref[...])
    sm = sm_ref[...]
    parts = []
    for h in range(NSA_HEADS):
        sl = slice(h * HEAD_DIM, (h + 1) * HEAD_DIM)
        c = FOX_HEADS + h * N_BRANCH
        parts.append(sm[:, c:c + 1] * oc_ref[:, sl] + sm[:, c + 1:c + 2] * os_ref[:, sl]
                     + sm[:, c + 2:c + 3] * ow_ref[:, sl])
    mix_n = jnp.concatenate(parts, axis=1) * _silu(nz_ref[...])
    mix = jnp.concatenate([mix_f, mix_n], axis=1).astype(BF16)
    h_ref[...] = x_ref[...] + jnp.dot(mix, w_ref[...], preferred_element_type=F32)


def _mix_out(x, o_f, o_c, o_s, o_w, big, w_out, tm):
    R, D = x.shape
    row = lambda i: (i, 0)
    wide = pl.BlockSpec((tm, 1024), row)
    return pl.pallas_call(
        _mix_out_kernel,
        out_shape=jax.ShapeDtypeStruct((R, D), F32),
        grid=(R // tm,),
        in_specs=[pl.BlockSpec((tm, D), row), wide, wide, wide, wide,
                  pl.BlockSpec((tm, LANES), lambda i: (i, C_SM * 2)),
                  pl.BlockSpec((tm, 1024), lambda i: (i, C_FZ // 4)),
                  pl.BlockSpec((tm, 1024), lambda i: (i, C_NZ // 4)),
                  pl.BlockSpec(w_out.shape, lambda i: (0, 0))],
        out_specs=pl.BlockSpec((tm, D), row),
        compiler_params=_cparams(("arbitrary",)),
        name="mix_out",
    )(x, o_f, o_c, o_s, o_w, big, big, big, w_out)


def _ple_kernel(h_ref, p_ref, g_ref, wg_ref, wp_ref, y_ref):
    h = h_ref[...]
    ms = jnp.mean(h * h, axis=-1, keepdims=True)
    hn = (h * lax.rsqrt(ms + EPS) * g_ref[...]).astype(BF16)
    gate = _sigmoid(jnp.dot(hn, wg_ref[...], preferred_element_type=F32))
    y_ref[...] = h + gate * jnp.dot(p_ref[...].astype(BF16), wp_ref[...], preferred_element_type=F32)


def _ple(h, p, ple_norm, w_gate, w_ple, tm):
    R, D = h.shape
    row = lambda i: (i, 0)
    return pl.pallas_call(
        _ple_kernel,
        out_shape=jax.ShapeDtypeStruct((R, D), F32),
        grid=(R // tm,),
        in_specs=[pl.BlockSpec((tm, D), row), pl.BlockSpec((tm, p.shape[1]), row),
                  pl.BlockSpec((1, D), lambda i: (0, 0)),
                  pl.BlockSpec(w_gate.shape, lambda i: (0, 0)),
                  pl.BlockSpec(w_ple.shape, lambda i: (0, 0))],
        out_specs=pl.BlockSpec((tm, D), row),
        compiler_params=_cparams(("arbitrary",)),
        name="ple_gate",
    )(h, p, ple_norm.reshape(1, D), w_gate, w_ple)


STK = GROUP * 4
STK_SHIFT = 4


def _decode_kernel(pt_ref, *refs, P, n_tok, has_bias, has_sel):
    q_ref = refs[0]
    k_pages, v_pages = refs[1:1 + P], refs[1 + P:1 + 2 * P]
    pos = 1 + 2 * P
    cq_ref = cs_ref = cst_ref = sel_ref = None
    if has_bias:
        cq_ref, cs_ref, cst_ref = refs[pos:pos + 3]
        pos += 3
    if has_sel:
        sel_ref = refs[pos]
        pos += 1
    il_ref, kt_ref, vt_ref, o_ref, m_ref, l_ref, acc_ref = refs[pos:pos + 7]
    step = pl.program_id(1)
    n_keys = P * PAGE
    n_rows = n_keys * NSA_KV
    rows = NSA_KV * STK
    row_group = lax.broadcasted_iota(jnp.int32, (rows, 1), 0) >> STK_SHIFT

    @pl.when(step == 0)
    def _():
        m_ref[...] = jnp.full(m_ref.shape, NEG_BIG, F32)
        l_ref[...] = jnp.zeros(l_ref.shape, F32)
        acc_ref[...] = jnp.zeros(acc_ref.shape, F32)

    def update(s, pv):
        m_prev = m_ref[...]
        m_new = jnp.maximum(m_prev, jnp.max(s, axis=-1, keepdims=True))
        alpha = jnp.exp(m_prev - m_new)
        p = jnp.exp(s - m_new)
        l_ref[...] = alpha * l_ref[...] + jnp.sum(p, axis=-1, keepdims=True)
        acc_ref[...] = alpha * acc_ref[...] + pv(p.astype(BF16))
        m_ref[...] = m_new

    qb = (q_ref[...] * SCALE).astype(BF16)
    kb = jnp.concatenate([r[...] for r in k_pages], axis=0).astype(BF16)
    vb = jnp.concatenate([r[...] for r in v_pages], axis=0).astype(BF16)
    s = lax.dot_general(qb, kb, (((1,), (1,)), ((), ())), preferred_element_type=F32)
    col = lax.broadcasted_iota(jnp.int32, (1, n_rows), 1)
    if has_bias:
        spread = il_ref[...]
        parts = []
        for u in range(P):
            c = cs_ref[:, u * PAGE:(u + 1) * PAGE]
            hi = c.astype(BF16)
            r1 = c - hi.astype(F32)
            mid = r1.astype(BF16)
            lo = (r1 - mid.astype(F32)).astype(BF16)
            parts.append(jnp.dot(hi, spread, preferred_element_type=F32) + jnp.dot(mid, spread, preferred_element_type=F32)
                         + jnp.dot(lo, spread, preferred_element_type=F32))
        s = s + (cq_ref[...] - jnp.concatenate(parts, axis=1))
    if has_sel:
        blk = ((step * n_keys + (col >> 1)) >> SLC_SHIFT) & (LANES - 1)
        onehot = (lax.broadcasted_iota(jnp.int32, (LANES, n_rows), 0) == blk).astype(BF16)
        s = s + jnp.dot(sel_ref[...], onehot, preferred_element_type=F32)
    s = jnp.where((col & 1) == row_group, s, NEG_BIG)
    update(s, lambda p: jnp.dot(p, vb, preferred_element_type=F32))

    @pl.when(step == pl.num_programs(1) - 1)
    def _():
        row_tok = lax.broadcasted_iota(jnp.int32, (rows, DEC_PAD), 0) & 3
        key_tok = lax.broadcasted_iota(jnp.int32, (rows, DEC_PAD), 1)
        mask = (key_tok <= row_tok) & (key_tok < n_tok)
        by_group = lambda a0, a1: jnp.where(row_group == 0, a0, a1)
        kt = [kt_ref[:, g * HEAD_DIM:(g + 1) * HEAD_DIM].astype(BF16) for g in range(NSA_KV)]
        vt = [vt_ref[:, g * HEAD_DIM:(g + 1) * HEAD_DIM].astype(BF16) for g in range(NSA_KV)]
        st = by_group(*[lax.dot_general(qb, k, (((1,), (1,)), ((), ())), preferred_element_type=F32) for k in kt])
        if has_bias:
            st = st + (cq_ref[...] - cst_ref[...])
        update(jnp.where(mask, st, NEG_BIG),
               lambda p: by_group(*[jnp.dot(p, v, preferred_element_type=F32) for v in vt]))
        o_ref[...] = acc_ref[...] / jnp.maximum(l_ref[...], 1e-30)


def _decode_attn(q_st, cache_k, cache_v, page_table, big, k_col, v_col, P, n_tok, *, bias=None, sel=None, name):
    B, n_pages = page_table.shape
    n_pool = cache_k.shape[0]
    rows = NSA_KV * STK
    k2 = cache_k.reshape(n_pool * PAGE * NSA_KV, HEAD_DIM)
    v2 = cache_v.reshape(n_pool * PAGE * NSA_KV, HEAD_DIM)
    pt = page_table.reshape(-1)
    n_keys = P * PAGE
    whole = lambda b, s, pt_ref: (b, 0, 0)

    def page(u):
        return pl.BlockSpec((PAGE * NSA_KV, HEAD_DIM), lambda b, s, pt_ref: (pt_ref[b * n_pages + s * P + u], 0))

    in_specs = [pl.BlockSpec((None, rows, HEAD_DIM), whole)]
    in_specs += [page(u) for u in range(P)] * 2
    args = [q_st.reshape(B, rows, HEAD_DIM)] + [k2] * P + [v2] * P
    if bias is not None:
        cq, cs, cs_tail = bias
        in_specs += [pl.BlockSpec((None, rows, 1), whole),
                     pl.BlockSpec((None, rows, n_keys), lambda b, s, pt_ref: (b, 0, s)),
                     pl.BlockSpec((None, rows, DEC_PAD), whole)]
        args += [cq.reshape(B, rows, 1), cs.reshape(B, rows, -1), cs_tail.reshape(B, rows, DEC_PAD)]
    if sel is not None:
        per_step = n_keys // SLC_BLOCK
        n_half = sel.shape[2]
        in_specs.append(pl.BlockSpec((None, None, rows, LANES), lambda b, s, pt_ref: (b, (s * per_step) // LANES, 0, 0)))
        args.append(sel.transpose(0, 2, 1, 3, 4).reshape(B, n_half, rows, LANES))
    spread = jnp.asarray(np.arange(PAGE)[:, None] == (np.arange(PAGE * NSA_KV)[None, :] >> 1), BF16)
    in_specs += [pl.BlockSpec((PAGE, PAGE * NSA_KV), lambda b, s, pt_ref: (0, 0)),
                 pl.BlockSpec((DEC_PAD, CH), lambda b, s, pt_ref: (b, k_col)),
                 pl.BlockSpec((DEC_PAD, CH), lambda b, s, pt_ref: (b, v_col))]
    args += [spread, big, big]
    out = pl.pallas_call(
        functools.partial(_decode_kernel, P=P, n_tok=n_tok, has_bias=bias is not None, has_sel=sel is not None),
        out_shape=jax.ShapeDtypeStruct((B, rows, HEAD_DIM), F32),
        grid_spec=pltpu.PrefetchScalarGridSpec(
            num_scalar_prefetch=1,
            grid=(B, n_pages // P),
            in_specs=in_specs,
            out_specs=pl.BlockSpec((None, rows, HEAD_DIM), whole),
            scratch_shapes=[pltpu.VMEM((rows, 1), F32), pltpu.VMEM((rows, 1), F32), pltpu.VMEM((rows, HEAD_DIM), F32)],
        ),
        compiler_params=_cparams(("arbitrary", "arbitrary")),
        name=name,
    )(pt, *args)
    return out.reshape(B, NSA_KV, STK, HEAD_DIM)


def _stack_rows(a, n_tok):
    B = a.shape[0]
    x = a.shape[-1] // (NSA_KV * GROUP)
    return a[:, :n_tok].reshape(B, n_tok, NSA_KV, GROUP, x).transpose(0, 2, 3, 1, 4).reshape(B, NSA_KV, STK, x)


def _unstack_rows(o_st, n_tok):
    B = o_st.shape[0]
    o = o_st.reshape(B, NSA_KV, GROUP, n_tok, HEAD_DIM).transpose(0, 3, 1, 2, 4).reshape(B, n_tok, -1)
    return jnp.pad(o, ((0, 0), (0, DEC_PAD - n_tok), (0, 0))).reshape(B * DEC_PAD, -1)


def _pick(n, cands):
    for c in cands:
        if n % c == 0:
            return c
    raise ValueError(f"no tile in {cands} divides {n}")


def _col(big, c, width=CH):
    return big[:, c * CH:c * CH + width]


def _prompt_layer(x, p_i, prm):
    T, D = x.shape
    big = _inproj(x, jnp.arange(T, dtype=jnp.int32), prm["attn_norm"], prm["w_perm"], prm["gain"], prm["bf_pad"],
                  _pick(T, (1024, 512, 256, 128)))
    logf = big[:, C_SM * CH:C_SM * CH + FOX_HEADS]
    nr = T // LANES
    c4, _ = _cumsum(logf.reshape(1, nr, LANES, FOX_HEADS).transpose(0, 1, 3, 2),
                    jnp.zeros((1, FOX_HEADS, DEC_PAD), F32))
    c_rows = c4.transpose(0, 1, 3, 2).reshape(T, FOX_HEADS)

    tq, tk = _pick(T, (1024, 512)), 512
    qm, fa, kf, vf, ks, vs, kw, vw, nrm = _pack(big, c_rows, tk)
    tabs = _prompt_tables(T, tq, tk)
    o_f = _pflash(qm, 0, kf, vf, tabs, tq, tk, aug=fa, aug_mode="head",
                  skip=_fox_skip(tabs, nrm, c_rows, T, tq, tk), name="fox_prompt")

    pk, pv = _cmp_partial_prompt(big, T, prm["w1k"], prm["w1v"], _pick(T // CMP_STRIDE, (256, 128, 64)))
    kc, vc = _cmp_mlp(pk[None], pv[None], prm["b1k"], prm["b1v"], prm["w2k"], prm["w2v"])
    nb = -(-(T // SLC_BLOCK) // LANES) * LANES
    tqc = _pick(T, (256, 128))
    o_c, sel = _cmp_attn(big, C_NQ // 4, 1, T // tqc, tqc, kc, vc, 0, nb)
    o_s = _pflash(qm, 1, ks, vs, tabs, tq, tk, aug=sel[0], aug_mode="group", name="slc_prompt")
    o_w = _pflash(qm, 1, kw, vw, _prompt_tables(T, tk, tk, window=True), tk, tk, name="win_prompt")

    tm = _pick(T, (256, 128))
    h = _mix_out(x, o_f, o_c, o_s, o_w, big, prm["w_out"], tm)
    y = _ple(h, p_i, prm["ple_norm"], prm["w_gate"], prm["w_ple"], tm)
    n_win = min(WINDOW, T)
    kv5 = lambda c: _col(big, c).reshape(1, 1, T, 2, HEAD_DIM)
    state = (kv5(C_FK), kv5(C_FV), logf.reshape(1, 1, T, FOX_HEADS), kv5(C_CK), kv5(C_CV), kv5(C_SK), kv5(C_SV),
             kv5(C_WK)[:, :, T - n_win:], kv5(C_WV)[:, :, T - n_win:])
    return y.reshape(1, T, D), state


def _sample_layer(x, p_i, caches, page_table, prm):
    c_fk, c_fv, c_flogf, c_ck, c_cv, c_sk, c_sv, c_wk, c_wv = caches
    B, Tn, D = x.shape
    n_pages = page_table.shape[1]
    past = n_pages * PAGE
    R = B * DEC_PAD
    xp = jnp.pad(x, ((0, 0), (0, DEC_PAD - Tn), (0, 0))).reshape(R, D)
    pos = jnp.tile(past + jnp.arange(DEC_PAD, dtype=jnp.int32), B)
    big = _inproj(xp, pos, prm["attn_norm"], prm["w_perm"], prm["gain"], prm["bf_pad"], _pick(R, (256, 128, 16)))
    logf = big[:, C_SM * CH:C_SM * CH + FOX_HEADS]

    assert Tn * GROUP == STK, "decode kernel stacks 4 heads x 4 new tokens per KV group"
    P = _pick(n_pages, (32, 16, 8, 4, 2, 1))
    lf_pages = _gather_logf_pages(c_flogf.transpose(0, 2, 1), page_table, _pick(n_pages, (32, 16, 8, 4, 2, 1)))
    c4, c_new = _cumsum(lf_pages, logf.reshape(B, DEC_PAD, FOX_HEADS).transpose(0, 2, 1))
    rep = lambda a: jnp.repeat(a, Tn, axis=2)
    cs_past = rep(c4.transpose(0, 2, 1, 3).reshape(B, FOX_KV, GROUP, past))
    cs_tail = rep(c_new.reshape(B, FOX_KV, GROUP, DEC_PAD))
    cq = c_new[:, :, :Tn].reshape(B, FOX_KV, STK, 1)

    q3 = lambda c: big[:, c * CH:c * CH + 1024].reshape(B, DEC_PAD, 1024)
    o_f = _unstack_rows(
        _decode_attn(_stack_rows(q3(C_FQ), Tn), c_fk, c_fv, page_table, big, C_FK, C_FV, P, Tn,
                     bias=(cq, cs_past, cs_tail), name="fox_decode"), Tn)

    pk, pv = _cmp_partial_paged(c_ck, c_cv, page_table, prm["w1k"], prm["w1v"], _pick(n_pages, (32, 16, 8, 4, 2, 1)))
    kc, vc = _cmp_mlp(pk, pv, prm["b1k"], prm["b1v"], prm["w2k"], prm["w2v"])
    nb = -(-(past // SLC_BLOCK) // LANES) * LANES
    o_c, sel = _cmp_attn(big, C_NQ // 4, B, 1, DEC_PAD, kc, vc, past, nb)
    sel_st = jnp.tile(sel[:, :, :, None, :Tn], (1, 1, 1, GROUP, 1, 1)).reshape(B, NSA_KV, nb // LANES, STK, LANES)
    o_s = _unstack_rows(
        _decode_attn(_stack_rows(q3(C_NQ), Tn), c_sk, c_sv, page_table, big, C_SK, C_SV, P, Tn,
                     sel=sel_st, name="slc_decode"), Tn)
    n_buf = c_wk.shape[1]
    wspec = [pl.BlockSpec((n_buf, HEAD_DIM), lambda b, g, s, *_: (b, g))]
    wflat = lambda c: c.reshape(B * n_buf, NSA_KV * HEAD_DIM)
    o_w = _flash(big, C_NQ // 2, DEC_PAD, 1, B, _linear_tables(1), wspec, wspec, [wflat(c_wk), wflat(c_wv)], n_buf,
                 tail=(big, C_WK * 2, big, C_WV * 2), windowed=True,
                 q_base=past, k_base=past - n_buf, tail_base=past, name="win_decode")

    tm = _pick(R, (256, 128, 16))
    pp = jnp.pad(p_i, ((0, 0), (0, DEC_PAD - Tn), (0, 0))).reshape(R, -1)
    h = _mix_out(xp, o_f, o_c, o_s, o_w, big, prm["w_out"], tm)
    y = _ple(h, pp, prm["ple_norm"], prm["w_gate"], prm["w_ple"], tm)
    y = y.reshape(B, DEC_PAD, D)[:, :Tn]
    new = lambda c: _col(big, c).reshape(B, DEC_PAD, 2, HEAD_DIM)[:, :Tn]
    wk_new, wv_new = new(C_WK), new(C_WV)
    kw = jnp.concatenate([c_wk, wk_new], axis=1)[:, -n_buf:]
    vw = jnp.concatenate([c_wv, wv_new], axis=1)[:, -n_buf:]
    state = (new(C_FK), new(C_FV), logf.reshape(B, DEC_PAD, FOX_HEADS)[:, :Tn], new(C_CK), new(C_CV), new(C_SK),
             new(C_SV), kw, vw)
    return y, tuple(s[None] for s in state)


def kernel(x_prompt, x_sample, cache_fox_k, cache_fox_v, cache_fox_logf, cache_cmp_k, cache_cmp_v, cache_slc_k,
           cache_slc_v, cache_win_k, cache_win_v, page_table, p_prompt, p_sample, attn_norm, w_in, b_forget,
           fox_q_norm, fox_k_norm, nsa_q_norm, nsa_k_norm, cmp_k_w1, cmp_k_b1, cmp_k_w2, cmp_v_w1, cmp_v_b1,
           cmp_v_w2, w_out, ple_norm, w_ple, w_ple_gate):
    assert x_prompt.shape[0] == 1 and w_in.shape[0] == 1, "one prompt sequence, one layer"
    w_perm, gain, bf_pad = _prep_inproj_params(w_in[0], b_forget[0], fox_q_norm[0], fox_k_norm[0], nsa_q_norm[0],
                                               nsa_k_norm[0])
    prm = dict(attn_norm=attn_norm[0], w_perm=w_perm, gain=gain, bf_pad=bf_pad,
               w1k=cmp_k_w1[0].astype(BF16), w1v=cmp_v_w1[0].astype(BF16), b1k=cmp_k_b1[0], b1v=cmp_v_b1[0],
               w2k=cmp_k_w2[0], w2v=cmp_v_w2[0], w_out=w_out[0].astype(BF16), ple_norm=ple_norm[0],
               w_gate=w_ple_gate[0].astype(BF16), w_ple=w_ple[0].astype(BF16))
    y_p, st_p = _prompt_layer(x_prompt[0], p_prompt[0, 0], prm)
    caches = (cache_fox_k[0], cache_fox_v[0], cache_fox_logf[0], cache_cmp_k[0], cache_cmp_v[0], cache_slc_k[0],
              cache_slc_v[0], cache_win_k[0], cache_win_v[0])
    y_s, st_s = _sample_layer(x_sample, p_sample[0], caches, page_table, prm)
    return (y_p, y_s) + tuple(st_p) + tuple(st_s)
```

```python
import functools

import numpy as np
import jax
import jax.numpy as jnp
from jax import lax
from jax.experimental import pallas as pl
from jax.experimental.pallas import tpu as pltpu

F32 = jnp.float32
BF16 = jnp.bfloat16

HEAD_DIM = 128
FOX_HEADS = 8
FOX_KV = 2
NSA_HEADS = 8
NSA_KV = 2
GROUP = 4
N_BRANCH = 3
ROPE_THETA = 500000.0
ROT_DIM = HEAD_DIM // 4
ROT_HALF = ROT_DIM // 2
CMP_STRIDE = 16
CMP_BLOCK = 2 * CMP_STRIDE
SLC_BLOCK = 64
SLC_SHIFT = 6
SLC_TOPN = 16
WINDOW = 512
EPS = 1e-6
NEG_BIG = -1e30
FORCE_BONUS = 1e4
SEL_OFF = -30000.0
PAGE = 128
SCALE = HEAD_DIM ** -0.5

SPLIT_SIZES = (1024, 256, 256, 8, 1024, 1024, 256, 256, 256, 256, 256, 256, 24, 1024)

CH = 256
C_FQ, C_NQ, C_FZ, C_NZ = 0, 4, 8, 12
C_FK, C_CK, C_SK, C_WK, C_FV, C_CV, C_SV, C_WV, C_SM = 16, 17, 18, 19, 20, 21, 22, 23, 24
N_CH = 25
NP = N_CH * CH
LANES = 128
DEC_PAD = 16

VMEM_LIMIT = 56 * 1024 * 1024


def _cparams(sem):
    return pltpu.CompilerParams(dimension_semantics=sem, vmem_limit_bytes=VMEM_LIMIT)


def _sigmoid(x):
    return 1.0 / (1.0 + jnp.exp(-x))


def _silu(x):
    return x * _sigmoid(x)


def _inproj_kernel(x_ref, g_ref, w_ref, gain_ref, bf_ref, rc_ref, rs1_ref, rs2_ref, o_ref, xn_ref, y_ref):
    j = pl.program_id(1)
    c = j - 1
    cur, prev = j & 1, c & 1

    def matmul():
        y_ref[cur] = jnp.dot(xn_ref[...], w_ref[...], preferred_element_type=F32)

    @pl.when(j == 0)
    def _():
        x = x_ref[...]
        ms = jnp.mean(x * x, axis=-1, keepdims=True)
        xn_ref[...] = (x * lax.rsqrt(ms + EPS) * g_ref[...]).astype(BF16)
        matmul()

    is_q_rope = (c >= C_NQ) & (c < C_FZ)
    is_k_rope = (c >= C_CK) & (c <= C_WK)
    is_norm_only = ((c >= 0) & (c < C_NQ)) | (c == C_FK)
    is_rope = is_q_rope | is_k_rope
    is_raw = ((c >= C_FZ) & (c < C_FK)) | ((c >= C_FV) & (c < C_SM))

    def normed(h):
        yh = y_ref[prev, :, h * LANES:(h + 1) * LANES]
        ms = jnp.mean(yh * yh, axis=-1, keepdims=True)
        return yh * lax.rsqrt(ms + EPS) * gain_ref[:, h * LANES:(h + 1) * LANES]

    @pl.when(is_norm_only)
    def _():
        for h in range(CH // LANES):
            o_ref[:, h * LANES:(h + 1) * LANES] = normed(h)
        matmul()

    @pl.when(is_rope)
    def _():
        for h in range(CH // LANES):
            yn = normed(h)
            lo = pltpu.roll(yn, LANES - ROT_HALF, axis=1)
            hi = pltpu.roll(yn, ROT_HALF, axis=1)
            o_ref[:, h * LANES:(h + 1) * LANES] = yn * rc_ref[...] + lo * rs1_ref[...] + hi * rs2_ref[...]
        matmul()

    @pl.when(is_raw)
    def _():
        o_ref[...] = y_ref[prev]
        matmul()

    @pl.when(c == C_SM)
    def _():
        t = y_ref[prev, :, :LANES] + bf_ref[...]
        lane = lax.broadcasted_iota(jnp.int32, t.shape, 1)
        e = jnp.exp(-jnp.abs(t))
        logsig = jnp.minimum(t, 0.0) - jnp.log(1.0 + e)
        o_ref[:, :LANES] = jnp.where(lane < FOX_HEADS, logsig, _sigmoid(t))
        o_ref[:, LANES:] = jnp.zeros_like(t)


def _inproj(x, pos, attn_norm, w_perm, gain, bf_pad, tm):
    R, D = x.shape
    inv = ROPE_THETA ** (-(2.0 / ROT_DIM) * jnp.arange(ROT_HALF, dtype=F32))
    ang = pos.astype(F32)[:, None] * inv[None, :]
    cos, sin = jnp.cos(ang), jnp.sin(ang)
    z = jnp.zeros((R, LANES - ROT_DIM), F32)
    zh = jnp.zeros((R, ROT_HALF), F32)
    rc = jnp.concatenate([cos, cos, z + 1.0], axis=1)
    rs1 = jnp.concatenate([-sin, zh, z], axis=1)
    rs2 = jnp.concatenate([zh, sin, z], axis=1)
    row = lambda i, j: (i, 0)
    return pl.pallas_call(
        _inproj_kernel,
        out_shape=jax.ShapeDtypeStruct((R, NP), F32),
        grid=(R // tm, N_CH + 1),
        in_specs=[
            pl.BlockSpec((tm, D), row),
            pl.BlockSpec((1, D), lambda i, j: (0, 0)),
            pl.BlockSpec((D, CH), lambda i, j: (0, jnp.minimum(j, N_CH - 1))),
            pl.BlockSpec((1, CH), lambda i, j: (0, jnp.maximum(j - 1, 0))),
            pl.BlockSpec((1, LANES), lambda i, j: (0, 0)),
            pl.BlockSpec((tm, LANES), row),
            pl.BlockSpec((tm, LANES), row),
            pl.BlockSpec((tm, LANES), row),
        ],
        out_specs=pl.BlockSpec((tm, CH), lambda i, j: (i, jnp.maximum(j - 1, 0))),
        scratch_shapes=[pltpu.VMEM((tm, D), BF16), pltpu.VMEM((2, tm, CH), F32)],
        compiler_params=_cparams(("arbitrary", "arbitrary")),
        name="inproj",
    )(x, attn_norm.reshape(1, D), w_perm, gain, bf_pad, rc, rs1, rs2)


def _prep_inproj_params(w_in, b_forget, fox_q_norm, fox_k_norm, nsa_q_norm, nsa_k_norm):
    D = w_in.shape[0]
    offs = np.concatenate([[0], np.cumsum(SPLIT_SIZES)]).tolist()
    seg = [w_in[:, offs[i]:offs[i + 1]] for i in range(len(SPLIT_SIZES))]
    fq, fk, fv, flg, fz, nq, ck, cv, sk, sv, wk, wv, ng, nz = seg
    small = jnp.concatenate([flg, ng, jnp.zeros((D, CH - FOX_HEADS - NSA_HEADS * N_BRANCH), F32)], axis=1)
    w_perm = jnp.concatenate([fq, nq, fz, nz, fk, ck, sk, wk, fv, cv, sv, wv, small], axis=1).astype(BF16)
    zeros = lambda n: jnp.zeros((n,), F32)
    gain = jnp.concatenate([
        jnp.tile(fox_q_norm, FOX_HEADS), jnp.tile(nsa_q_norm, NSA_HEADS), zeros(2048),
        jnp.tile(fox_k_norm, FOX_KV), jnp.tile(nsa_k_norm[0], NSA_KV), jnp.tile(nsa_k_norm[1], NSA_KV),
        jnp.tile(nsa_k_norm[2], NSA_KV), zeros(4 * CH + CH)]).reshape(1, NP)
    bf_pad = jnp.concatenate([b_forget, zeros(LANES - FOX_HEADS)]).reshape(1, LANES)
    return w_perm, gain, bf_pad


def _gather_pages_kernel(pt_ref, *refs):
    n = len(refs) - 1
    o_ref = refs[n]
    for u in range(n):
        o_ref[u] = refs[u][...]


def _gather_logf_pages(cache_t, page_table, pages_per_step):
    B, n_pages = page_table.shape
    P = pages_per_step
    pt = page_table.reshape(-1)

    def src_map(u):
        return lambda b, c, pt_ref: (pt_ref[b * n_pages + c * P + u], 0, 0)

    return pl.pallas_call(
        _gather_pages_kernel,
        out_shape=jax.ShapeDtypeStruct((B, n_pages, FOX_HEADS, PAGE), F32),
        grid_spec=pltpu.PrefetchScalarGridSpec(
            num_scalar_prefetch=1,
            grid=(B, n_pages // P),
            in_specs=[pl.BlockSpec((None, FOX_HEADS, PAGE), src_map(u)) for u in range(P)],
            out_specs=pl.BlockSpec((None, P, FOX_HEADS, PAGE), lambda b, c, pt_ref: (b, c, 0, 0)),
        ),
        compiler_params=_cparams(("arbitrary", "arbitrary")),
        name="gather_logf_pages",
    )(pt, *([cache_t] * P))


def _cumsum_kernel(x_ref, new_ref, before_ref, o_ref, onew_ref, *, nr):
    hp = lax.Precision.HIGHEST
    x = x_ref[...].reshape(nr * FOX_HEADS, LANES)
    ci = lax.broadcasted_iota(jnp.int32, (LANES, LANES), 0)
    cj = lax.broadcasted_iota(jnp.int32, (LANES, LANES), 1)
    upper = (ci <= cj).astype(F32)
    lastcol = (ci == LANES - 1).astype(F32)
    within = jnp.dot(x, upper, precision=hp, preferred_element_type=F32)
    rowtot = jnp.dot(within, lastcol, precision=hp, preferred_element_type=F32)
    n = nr * FOX_HEADS
    before = before_ref[...]
    hi = rowtot.astype(BF16)
    r1 = rowtot - hi.astype(F32)
    mid = r1.astype(BF16)
    lo = (r1 - mid.astype(F32)).astype(BF16)
    prefix = (jnp.dot(before, hi, preferred_element_type=F32) + jnp.dot(before, mid, preferred_element_type=F32)
              + jnp.dot(before, lo, preferred_element_type=F32))
    o_ref[...] = (within + prefix).reshape(nr, FOX_HEADS, LANES)
    total = (prefix + rowtot)[n - FOX_HEADS:, :]
    nw = new_ref[...]
    ti = lax.broadcasted_iota(jnp.int32, (DEC_PAD, DEC_PAD), 0)
    tj = lax.broadcasted_iota(jnp.int32, (DEC_PAD, DEC_PAD), 1)
    onew_ref[...] = total[:, :DEC_PAD] + jnp.dot(nw, (ti <= tj).astype(F32), precision=hp,
                                                  preferred_element_type=F32)


def _cumsum(x4, new):
    B, nr = x4.shape[:2]
    n = nr * FOX_HEADS
    r = np.arange(n)
    before = jnp.asarray((r[:, None] % FOX_HEADS == r[None, :] % FOX_HEADS) & (r[None, :] < r[:, None]), BF16)
    return pl.pallas_call(
        functools.partial(_cumsum_kernel, nr=nr),
        out_shape=(jax.ShapeDtypeStruct(x4.shape, F32), jax.ShapeDtypeStruct(new.shape, F32)),
        grid=(B,),
        in_specs=[pl.BlockSpec((None, nr, FOX_HEADS, LANES), lambda b: (b, 0, 0, 0)),
                  pl.BlockSpec((None, FOX_HEADS, DEC_PAD), lambda b: (b, 0, 0)),
                  pl.BlockSpec((n, n), lambda b: (0, 0))],
        out_specs=(pl.BlockSpec((None, nr, FOX_HEADS, LANES), lambda b: (b, 0, 0, 0)),
                   pl.BlockSpec((None, FOX_HEADS, DEC_PAD), lambda b: (b, 0, 0))),
        compiler_params=_cparams(("arbitrary",)),
        name="cumsum_logf",
    )(x4, new, before)


def _flash_kernel(*refs, n_sp, n_src, tq, n_keys, has_bias, has_sel, has_tail, windowed,
                  q_base, k_base, tail_base):
    qi_tab, kj_tab, first_tab, last_tab = refs[:4]
    refs = refs[n_sp:]
    q_ref = refs[0]
    k_srcs = refs[1:1 + n_src]
    v_srcs = refs[1 + n_src:1 + 2 * n_src]
    pos = 1 + 2 * n_src
    cq_ref = cs_ref = sel_ref = kt_ref = vt_ref = cst_ref = None
    if has_bias:
        cq_ref, cs_ref = refs[pos], refs[pos + 1]
        pos += 2
    if has_sel:
        sel_ref = refs[pos]
        pos += 1
    if has_tail:
        kt_ref, vt_ref = refs[pos], refs[pos + 1]
        pos += 2
        if has_bias:
            cst_ref = refs[pos]
            pos += 1
    o_ref = refs[pos]
    qs_ref, m_ref, l_ref, acc_ref = refs[pos + 1:pos + 5]

    step = pl.program_id(2)
    g = pl.program_id(1)
    qi = qi_tab[step]
    kj = kj_tab[step]

    @pl.when(first_tab[step] == 1)
    def _():
        qs_ref[...] = (q_ref[...] * SCALE).astype(BF16)
        m_ref[...] = jnp.full(m_ref.shape, NEG_BIG, F32)
        l_ref[...] = jnp.zeros(l_ref.shape, F32)
        acc_ref[...] = jnp.zeros(acc_ref.shape, F32)

    q_pos = q_base + qi * tq + lax.broadcasted_iota(jnp.int32, (tq, 1), 0)

    def attend(k, v, k_pos, cs, sel_bias):
        kb = k.astype(BF16)
        vb = v.astype(BF16)
        mask = k_pos <= q_pos
        if windowed:
            mask = mask & ((q_pos - k_pos) < WINDOW)
        for h in range(GROUP):
            s = lax.dot_general(qs_ref[:, h * HEAD_DIM:(h + 1) * HEAD_DIM], kb, (((1,), (1,)), ((), ())),
                                preferred_element_type=F32)
            if cs is not None:
                hh = g * GROUP + h
                cq = jnp.sum(jnp.where(lax.broadcasted_iota(jnp.int32, cq_ref.shape, 1) == hh, cq_ref[...], 0.0),
                             axis=1, keepdims=True)
                s = s + (cq - cs[h:h + 1, :])
            if sel_bias is not None:
                s = s + sel_bias
            s = jnp.where(mask, s, NEG_BIG)
            m_prev = m_ref[h]
            m_new = jnp.maximum(m_prev, jnp.max(s, axis=-1, keepdims=True))
            alpha = jnp.exp(m_prev - m_new)
            p = jnp.exp(s - m_new)
            l_ref[h] = alpha * l_ref[h] + jnp.sum(p, axis=-1, keepdims=True)
            acc_ref[h] = alpha * acc_ref[h] + jnp.dot(p.astype(BF16), vb, preferred_element_type=F32)
            m_ref[h] = m_new

    if n_src == 1:
        k, v = k_srcs[0][...], v_srcs[0][...]
    else:
        k = jnp.concatenate([r[...] for r in k_srcs], axis=0)
        v = jnp.concatenate([r[...] for r in v_srcs], axis=0)
    k_pos = k_base + kj * n_keys + lax.broadcasted_iota(jnp.int32, (1, n_keys), 1)
    sel_bias = None
    if has_sel:
        blk = ((kj * n_keys + lax.broadcasted_iota(jnp.int32, (1, n_keys), 1)) >> SLC_SHIFT) & (LANES - 1)
        onehot = (lax.broadcasted_iota(jnp.int32, (LANES, n_keys), 0) == blk).astype(BF16)
        sel_bias = jnp.dot(sel_ref[...], onehot, preferred_element_type=F32)
    attend(k, v, k_pos, cs_ref[...] if has_bias else None, sel_bias)

    @pl.when(last_tab[step] == 1)
    def _():
        if has_tail:
            t_pos = tail_base + lax.broadcasted_iota(jnp.int32, (1, DEC_PAD), 1)
            attend(kt_ref[...], vt_ref[...], t_pos, cst_ref[...] if has_bias else None, None)
        for h in range(GROUP):
            o_ref[:, h * HEAD_DIM:(h + 1) * HEAD_DIM] = acc_ref[h] / jnp.maximum(l_ref[h], 1e-30)


def _flash(q_arr, q_col, tq, n_qblk, B, tables, k_specs, v_specs, kv_args, n_keys, *, prefetch=(),
           bias=None, sel=None, tail=None, windowed=False, q_base=0, k_base=0, tail_base=0, name="flash"):
    n_sp = len(tables) + len(prefetch)
    n_steps = tables[0].shape[0]
    n_src = len(k_specs)
    qmap = lambda b, g, s, qi, *_: (b * n_qblk + qi[s], q_col + g)
    in_specs = [pl.BlockSpec((tq, GROUP * HEAD_DIM), qmap)] + list(k_specs) + list(v_specs)
    args = [q_arr] + list(kv_args)
    if bias is not None:
        in_specs.append(pl.BlockSpec((tq, FOX_HEADS), lambda b, g, s, qi, *_: (b * n_qblk + qi[s], 0)))
        in_specs.append(pl.BlockSpec((None, None, GROUP, n_keys), lambda b, g, s, qi, kj, *_: (b, g, 0, kj[s])))
        args += [bias[0], bias[1]]
    if sel is not None:
        blocks_per_step = n_keys // SLC_BLOCK
        in_specs.append(pl.BlockSpec(
            (None, None, None, tq, LANES),
            lambda b, g, s, qi, kj, *_: (b, g, (kj[s] * blocks_per_step) // LANES, qi[s], 0)))
        args.append(sel)
    if tail is not None:
        k_arr, k_col, v_arr, v_col = tail
        in_specs.append(pl.BlockSpec((DEC_PAD, HEAD_DIM), lambda b, g, s, *_: (b, k_col + g)))
        in_specs.append(pl.BlockSpec((DEC_PAD, HEAD_DIM), lambda b, g, s, *_: (b, v_col + g)))
        args += [k_arr, v_arr]
        if bias is not None:
            in_specs.append(pl.BlockSpec((None, None, GROUP, DEC_PAD), lambda b, g, s, *_: (b, g, 0, 0)))
            args.append(bias[2])
    kern = functools.partial(
        _flash_kernel, n_sp=n_sp, n_src=n_src, tq=tq, n_keys=n_keys, has_bias=bias is not None,
        has_sel=sel is not None, has_tail=tail is not None, windowed=windowed,
        q_base=q_base, k_base=k_base, tail_base=tail_base)
    return pl.pallas_call(
        kern,
        out_shape=jax.ShapeDtypeStruct((B * n_qblk * tq, 2 * GROUP * HEAD_DIM), F32),
        grid_spec=pltpu.PrefetchScalarGridSpec(
            num_scalar_prefetch=n_sp,
            grid=(B, 2, n_steps),
            in_specs=in_specs,
            out_specs=pl.BlockSpec((tq, GROUP * HEAD_DIM), lambda b, g, s, qi, *_: (b * n_qblk + qi[s], g)),
            scratch_shapes=[pltpu.VMEM((tq, GROUP * HEAD_DIM), BF16),
                            pltpu.VMEM((GROUP, tq, 1), F32),
                            pltpu.VMEM((GROUP, tq, 1), F32),
                            pltpu.VMEM((GROUP, tq, HEAD_DIM), F32)],
        ),
        compiler_params=_cparams(("arbitrary", "arbitrary", "arbitrary")),
        name=name,
    )(*tables, *prefetch, *args)


def _causal_tables(n_blk, lookback=None):
    qi, kj, first, last = [], [], [], []
    for i in range(n_blk):
        lo = 0 if lookback is None else max(0, i - lookback)
        for j in range(lo, i + 1):
            qi.append(i)
            kj.append(j)
            first.append(int(j == lo))
            last.append(int(j == i))
    return tuple(jnp.asarray(a, jnp.int32) for a in (qi, kj, first, last))


def _linear_tables(n_steps):
    z = np.zeros((n_steps,), np.int32)
    first, last = z.copy(), z.copy()
    first[0], last[-1] = 1, 1
    return tuple(jnp.asarray(a, jnp.int32) for a in (z, np.arange(n_steps, dtype=np.int32), first, last))


LOG2E = 1.4426950408889634
AUG = 128
N_SPLIT = 3
AUG_C0 = N_SPLIT
NRM_ROWS = 16
UNDERFLOW_LOG2 = -160.0


def _split_part(x, part):
    hi = x.astype(BF16)
    r1 = x - hi.astype(F32)
    mid = r1.astype(BF16)
    lo = (r1 - mid.astype(F32)).astype(BF16)
    return jnp.where(part == 0, hi, jnp.where(part == 1, mid, lo))


def _pack_kernel(big_q, big_kv, c_ref, qm_ref, fa_ref, kf_ref, vf_ref, ks_ref, vs_ref, kw_ref, vw_ref, nrm_ref,
                 *state_refs, tr):
    i = pl.program_id(0)
    lane = lax.broadcasted_iota(jnp.int32, (tr, AUG), 1)
    in_c = (lane >= AUG_C0) & (lane < AUG_C0 + GROUP * N_SPLIT)
    slot = jnp.zeros_like(lane)
    for j in range(1, GROUP):
        slot = slot + jnp.where(lane >= AUG_C0 + j * N_SPLIT, 1, 0)
    part = jnp.where(lane < AUG_C0, lane, lane - AUG_C0 - N_SPLIT * slot)
    c2 = c_ref[...] * LOG2E
    ones_lane = jnp.where(lane == 0, 1.0, 0.0).astype(BF16)
    zeros = jnp.zeros((tr, AUG), BF16)

    def max_sq_norm(x):
        xf = x.astype(F32)
        n2 = jnp.max(jnp.sum(xf * xf, axis=-1, keepdims=True), axis=0, keepdims=True)
        return jnp.broadcast_to(n2, (1, LANES))

    qmb = (big_q[...] * (SCALE * LOG2E)).astype(BF16)
    qm_ref[...] = qmb
    nrm_ref[...] = jnp.zeros(nrm_ref.shape, F32)
    for h in range(FOX_HEADS):
        nrm_ref[h:h + 1, :] = max_sq_norm(qmb[:, h * HEAD_DIM:(h + 1) * HEAD_DIM])
        cq = jnp.where(lane < AUG_C0, c2[:, h:h + 1], 0.0)
        own = in_c & (slot == h % GROUP)
        fa_ref[:, h * AUG:(h + 1) * AUG] = jnp.where(own, -1.0, _split_part(cq, part).astype(F32)).astype(BF16)

    tok = i * tr + lax.broadcasted_iota(jnp.int32, (tr, AUG), 0)
    blk_onehot = jnp.where(lane == ((tok >> SLC_SHIFT) & (AUG - 1)), 1.0, 0.0).astype(BF16)
    for g in range(NSA_KV):
        cs = jnp.zeros((tr, AUG), F32)
        for j in range(GROUP):
            cs = jnp.where(in_c & (slot == j), c2[:, g * GROUP + j:g * GROUP + j + 1], cs)
        k_aug = jnp.where(lane < AUG_C0, 1.0, _split_part(cs, part).astype(F32)).astype(BF16)
        col = lambda c: big_kv[:, (c - C_FK) * CH + g * HEAD_DIM:(c - C_FK) * CH + (g + 1) * HEAD_DIM].astype(BF16)
        lo, hi = g * 2 * HEAD_DIM, g * 2 * HEAD_DIM + HEAD_DIM
        for k_out, v_out, kc, vc, aug in ((kf_ref, vf_ref, C_FK, C_FV, k_aug), (ks_ref, vs_ref, C_SK, C_SV, blk_onehot),
                                          (kw_ref, vw_ref, C_WK, C_WV, zeros)):
            k_out[:, lo:hi] = col(kc)
            if kc == C_FK:
                nrm_ref[FOX_HEADS + g:FOX_HEADS + g + 1, :] = max_sq_norm(col(kc))
            k_out[:, hi:hi + AUG] = aug
            v_out[:, lo:hi] = col(vc)
            v_out[:, hi:hi + AUG] = ones_lane

    for c, s_ref in enumerate(state_refs):
        x = big_kv[:, c * CH:(c + 1) * CH]
        halves = jnp.stack([x[:, :HEAD_DIM], x[:, HEAD_DIM:]], axis=0)
        s_ref[...] = jnp.swapaxes(halves, 0, 1).reshape(NSA_KV * tr, HEAD_DIM)


def _pack(big, c_rows, tr):
    T = big.shape[0]
    row = lambda i: (i, 0)
    kv = jax.ShapeDtypeStruct((T, NSA_KV * (HEAD_DIM + AUG)), BF16)
    kvspec = pl.BlockSpec((tr, NSA_KV * (HEAD_DIM + AUG)), row)
    return pl.pallas_call(
        functools.partial(_pack_kernel, tr=tr),
        out_shape=(jax.ShapeDtypeStruct((T, 2048), BF16), jax.ShapeDtypeStruct((T, FOX_HEADS * AUG), BF16)) + (kv,) * 6
        + (jax.ShapeDtypeStruct((T // tr, NRM_ROWS, LANES), F32),)
        + (jax.ShapeDtypeStruct((NSA_KV * T, HEAD_DIM), F32),) * 8,
        grid=(T // tr,),
        in_specs=[pl.BlockSpec((tr, 2048), row),
                  pl.BlockSpec((tr, 2048), lambda i: (i, C_FK // 8)),
                  pl.BlockSpec((tr, FOX_HEADS), row)],
        out_specs=(pl.BlockSpec((tr, 2048), row), pl.BlockSpec((tr, FOX_HEADS * AUG), row)) + (kvspec,) * 6
        + (pl.BlockSpec((None, NRM_ROWS, LANES), lambda i: (i, 0, 0)),)
        + (pl.BlockSpec((NSA_KV * tr, HEAD_DIM), row),) * 8,
        compiler_params=_cparams(("arbitrary",)),
        name="pack_qkv",
    )(big, big, c_rows)


def _pflash_kernel(qi_tab, kj_tab, first_tab, last_tab, kind_tab, aug_tab, skip_tab, kjd_tab, *refs, tq, tk,
                   aug_mode):
    qm_ref, k_ref, v_ref = refs[:3]
    pos = 3
    a_ref = None
    if aug_mode != "none":
        a_ref = refs[pos]
        pos += 1
    o_ref, qs_ref, m_ref, acc_ref = refs[pos:pos + 4]
    step = pl.program_id(1)
    active = skip_tab[pl.program_id(0) * pl.num_programs(1) + step] == 0

    @pl.when(first_tab[step] == 1)
    def _():
        for h in range(GROUP):
            qs_ref[h, :, :HEAD_DIM] = qm_ref[:, h * HEAD_DIM:(h + 1) * HEAD_DIM]
            if aug_mode == "head":
                qs_ref[h, :, HEAD_DIM:] = a_ref[:, h * AUG:(h + 1) * AUG]
            elif aug_mode == "none":
                qs_ref[h, :, HEAD_DIM:] = jnp.zeros((tq, AUG), BF16)
        m_ref[...] = jnp.full(m_ref.shape, NEG_BIG, F32)
        acc_ref[...] = jnp.zeros(acc_ref.shape, F32)

    if aug_mode == "group":
        @pl.when(aug_tab[step] == 1)
        def _():
            for h in range(GROUP):
                qs_ref[h, :, HEAD_DIM:] = a_ref[...]

    def attend(kind):
        kb = k_ref[...]
        vb = v_ref[...]
        if kind:
            ahead = (qi_tab[step] * tq - kj_tab[step] * tk + lax.broadcasted_iota(jnp.int32, (tq, tk), 0)
                     - lax.broadcasted_iota(jnp.int32, (tq, tk), 1))
            keep = (ahead >= 0) if kind == 1 else ((ahead >= 0) & (ahead < WINDOW))
        for h in range(GROUP):
            s = lax.dot_general(qs_ref[h], kb, (((1,), (1,)), ((), ())), preferred_element_type=F32)
            if kind:
                s = jnp.where(keep, s, NEG_BIG)
            m_prev = m_ref[h]
            m_tile = s[:, :LANES]
            for c in range(1, tk // LANES):
                m_tile = jnp.maximum(m_tile, s[:, c * LANES:(c + 1) * LANES])
            m_new = jnp.maximum(m_prev, jnp.max(m_tile, axis=-1, keepdims=True))
            p = jnp.exp2(s - jnp.tile(m_new, (1, tk // LANES))).astype(BF16)
            alpha = jnp.exp2(m_prev - m_new)
            acc_ref[h] = jnp.tile(alpha, (1, 2)) * acc_ref[h] + jnp.dot(p, vb, preferred_element_type=F32)
            m_ref[h] = m_new

    for kind in range(3):
        pl.when((kind_tab[step] == kind) & active)(functools.partial(attend, kind))

    @pl.when(last_tab[step] == 1)
    def _():
        for h in range(GROUP):
            acc = acc_ref[h]
            o_ref[:, h * HEAD_DIM:(h + 1) * HEAD_DIM] = acc[:, :HEAD_DIM] / jnp.maximum(
                acc[:, HEAD_DIM:HEAD_DIM + 1], 1e-30)


def _pflash(qm, q_col, k_arr, v_arr, tables, tq, tk, *, aug=None, aug_mode="none", skip=None, name):
    T = qm.shape[0]
    n_steps = tables[0].shape[0]
    kvw = HEAD_DIM + AUG
    in_specs = [pl.BlockSpec((tq, GROUP * HEAD_DIM), lambda g, s, qi, *_: (qi[s], q_col * 2 + g)),
                pl.BlockSpec((tk, kvw), lambda g, s, *tabs: (tabs[7][g * n_steps + s], g)),
                pl.BlockSpec((tk, kvw), lambda g, s, *tabs: (tabs[7][g * n_steps + s], g))]
    args = [qm, k_arr, v_arr]
    if aug_mode == "head":
        in_specs.append(pl.BlockSpec((tq, GROUP * AUG), lambda g, s, qi, *_: (qi[s], g)))
        args.append(aug)
    elif aug_mode == "group":
        per_step = tk // SLC_BLOCK
        in_specs.append(pl.BlockSpec((None, None, tq, AUG),
                                     lambda g, s, qi, kj, *_: (g, (kj[s] * per_step) // AUG, qi[s], 0)))
        args.append(aug)
    return pl.pallas_call(
        functools.partial(_pflash_kernel, tq=tq, tk=tk, aug_mode=aug_mode),
        out_shape=jax.ShapeDtypeStruct((T, 2 * GROUP * HEAD_DIM), F32),
        grid_spec=pltpu.PrefetchScalarGridSpec(
            num_scalar_prefetch=8,
            grid=(2, n_steps),
            in_specs=in_specs,
            out_specs=pl.BlockSpec((tq, GROUP * HEAD_DIM), lambda g, s, qi, *_: (qi[s], g)),
            scratch_shapes=[pltpu.VMEM((GROUP, tq, kvw), BF16),
                            pltpu.VMEM((GROUP, tq, LANES), F32),
                            pltpu.VMEM((GROUP, tq, kvw), F32)],
        ),
        compiler_params=_cparams(("arbitrary", "arbitrary")),
        name=name,
    )(*(jnp.asarray(t) for t in tables), *(skip if skip is not None else _no_skip(tables)), *args)


def _fox_skip(tables, nrm, c_rows, T, tq, tk):
    qi, kj, last = np.asarray(tables[0]), np.asarray(tables[1]), np.asarray(tables[3])
    nq, nt, r = T // tq, T // tk, tq // tk
    qn = jnp.sqrt(nrm[:, :FOX_HEADS, 0].reshape(nq, r, FOX_HEADS).max(axis=1))
    kn = jnp.sqrt(nrm[:, FOX_HEADS:FOX_HEADS + FOX_KV, 0])
    kn_own = kn.reshape(nq, r, FOX_KV).max(axis=1)
    c2 = c_rows * LOG2E
    c_first_q, c_last_k = c2[::tq], c2[tk - 1::tk]
    rep = lambda a: jnp.repeat(a, GROUP, axis=-1)
    logit_max = qn[qi] * rep(kn[kj]) + (c_first_q[qi] - c_last_k[kj])
    max_lb = -qn[qi] * rep(kn_own[qi])
    before = jnp.asarray((kj + 1) * tk - 1 < qi * tq)[:, None]
    dead = (before & (logit_max < max_lb + UNDERFLOW_LOG2)).reshape(-1, FOX_KV, GROUP).all(axis=-1)
    last_of_block = np.zeros_like(kj)
    nxt = 0
    for s in range(len(kj) - 1, -1, -1):
        if last[s] == 1:
            nxt = kj[s]
        last_of_block[s] = nxt
    skip = dead.T.astype(jnp.int32).reshape(-1)
    kjd = jnp.where(dead.T, jnp.asarray(last_of_block)[None, :], jnp.asarray(kj)[None, :]).astype(jnp.int32)
    return skip, kjd.reshape(-1)


def _no_skip(tables):
    n = tables[1].shape[0]
    return jnp.zeros((2 * n,), jnp.int32), jnp.asarray(np.tile(tables[1], 2))


def _prompt_tables(T, tq, tk, window=False):
    per_half = AUG * SLC_BLOCK // tk
    rows = []
    for i in range(T // tq):
        q_lo, q_hi = i * tq, (i + 1) * tq - 1
        lo = max(0, q_lo - WINDOW + 1) // tk if window else 0
        hi = q_hi // tk
        for j in range(lo, hi + 1):
            crosses = (j + 1) * tk - 1 > q_lo
            kind = 2 if window else int(crosses)
            rows.append((i, j, int(j == lo), int(j == hi), kind, int(j == lo or j % per_half == 0)))
    return tuple(np.asarray(a, np.int32) for a in zip(*rows))


def _cmp_cat(w1):
    return jnp.concatenate([w1[:CMP_STRIDE], w1[CMP_STRIDE:]], axis=-1)


def _cmp_partial_prompt_kernel(xk0_ref, xk1_ref, xv0_ref, xv1_ref, wk_ref, wv_ref, ok_ref, ov_ref, *, tn):
    for x_refs, w_ref, o_ref in (((xk0_ref, xk1_ref), wk_ref, ok_ref), ((xv0_ref, xv1_ref), wv_ref, ov_ref)):
        for kv in range(NSA_KV):
            acc = None
            for r in range(CMP_STRIDE):
                x = x_refs[kv][pl.ds(r, tn, stride=CMP_STRIDE), :]
                d = jnp.dot(x.astype(BF16), w_ref[r], preferred_element_type=F32)
                acc = d if acc is None else acc + d
            o_ref[:, kv * HEAD_DIM:(kv + 1) * HEAD_DIM] = acc[:, :HEAD_DIM]
            o_ref[:, (NSA_KV + kv) * HEAD_DIM:(NSA_KV + kv + 1) * HEAD_DIM] = acc[:, HEAD_DIM:]


def _cmp_partial_prompt(big, T, w1k, w1v, tn):
    ns = T // CMP_STRIDE
    wspec = pl.BlockSpec((CMP_STRIDE, HEAD_DIM, 2 * HEAD_DIM), lambda i: (0, 0, 0))
    out = jax.ShapeDtypeStruct((ns, 4 * HEAD_DIM), F32)
    return pl.pallas_call(
        functools.partial(_cmp_partial_prompt_kernel, tn=tn),
        out_shape=(out, out),
        grid=(ns // tn,),
        in_specs=[pl.BlockSpec((tn * CMP_STRIDE, HEAD_DIM), (lambda c: (lambda i: (i, c)))(c))
                  for c in (2 * C_CK, 2 * C_CK + 1, 2 * C_CV, 2 * C_CV + 1)] + [wspec, wspec],
        out_specs=(pl.BlockSpec((tn, 4 * HEAD_DIM), lambda i: (i, 0)),) * 2,
        compiler_params=_cparams(("arbitrary",)),
        name="cmp_partial_prompt",
    )(big, big, big, big, _cmp_cat(w1k), _cmp_cat(w1v))


def _cmp_partial_paged_kernel(pt_ref, *refs, n_pages):
    xk, xv = refs[:n_pages], refs[n_pages:2 * n_pages]
    wk_ref, wv_ref, ok_ref, ov_ref = refs[2 * n_pages:]
    sub = PAGE // CMP_STRIDE
    rows_per_sub = CMP_STRIDE * NSA_KV
    for xs, w_ref, o_ref in ((xk, wk_ref, ok_ref), (xv, wv_ref, ov_ref)):
        x_all = jnp.concatenate([p[...] for p in xs], axis=0).reshape(n_pages * sub, rows_per_sub, HEAD_DIM)
        by_row = jnp.swapaxes(x_all, 0, 1).astype(BF16)
        for kv in range(NSA_KV):
            acc = None
            for r in range(CMP_STRIDE):
                d = jnp.dot(by_row[NSA_KV * r + kv], w_ref[r], preferred_element_type=F32)
                acc = d if acc is None else acc + d
            o_ref[:, kv * HEAD_DIM:(kv + 1) * HEAD_DIM] = acc[:, :HEAD_DIM]
            o_ref[:, (NSA_KV + kv) * HEAD_DIM:(NSA_KV + kv + 1) * HEAD_DIM] = acc[:, HEAD_DIM:]


def _cmp_partial_paged(cache_k, cache_v, page_table, w1k, w1v, pages_per_step):
    B, n_pages = page_table.shape
    P = pages_per_step
    sub = PAGE // CMP_STRIDE
    n_pool = cache_k.shape[0]
    vk = cache_k.reshape(n_pool * PAGE * NSA_KV, HEAD_DIM)
    vv = cache_v.reshape(n_pool * PAGE * NSA_KV, HEAD_DIM)
    pt = page_table.reshape(-1)

    def src(u):
        return pl.BlockSpec((PAGE * NSA_KV, HEAD_DIM), lambda b, c, pt_ref: (pt_ref[b * n_pages + c * P + u], 0))

    wspec = pl.BlockSpec((CMP_STRIDE, HEAD_DIM, 2 * HEAD_DIM), lambda b, c, pt_ref: (0, 0, 0))
    out = jax.ShapeDtypeStruct((B, n_pages * sub, 4 * HEAD_DIM), F32)
    ospec = pl.BlockSpec((None, P * sub, 4 * HEAD_DIM), lambda b, c, pt_ref: (b, c, 0))
    return pl.pallas_call(
        functools.partial(_cmp_partial_paged_kernel, n_pages=P),
        out_shape=(out, out),
        grid_spec=pltpu.PrefetchScalarGridSpec(
            num_scalar_prefetch=1,
            grid=(B, n_pages // P),
            in_specs=[src(u) for u in range(P)] * 2 + [wspec, wspec],
            out_specs=(ospec, ospec),
        ),
        compiler_params=_cparams(("arbitrary", "arbitrary")),
        name="cmp_partial_paged",
    )(pt, *([vk] * P), *([vv] * P), _cmp_cat(w1k), _cmp_cat(w1v))


def _cmp_mlp_kernel(pk_ref, pv_ref, b1k_ref, b1v_ref, w2k_ref, w2v_ref, ok_ref, ov_ref, *, ns):
    for p_ref, b_ref, w_ref, o_ref in ((pk_ref, b1k_ref, w2k_ref, ok_ref), (pv_ref, b1v_ref, w2v_ref, ov_ref)):
        for kv in range(NSA_KV):
            p0 = p_ref[:, kv * HEAD_DIM:(kv + 1) * HEAD_DIM]
            p1 = p_ref[:, (NSA_KV + kv) * HEAD_DIM:(NSA_KV + kv + 1) * HEAD_DIM]
            nxt = pltpu.roll(p1, ns - 1, axis=0)
            h = _silu(p0 + nxt + b_ref[...])
            o_ref[:, kv * HEAD_DIM:(kv + 1) * HEAD_DIM] = jnp.dot(h.astype(BF16), w_ref[...],
                                                                  preferred_element_type=F32).astype(BF16)


def _cmp_mlp(pk, pv, b1k, b1v, w2k, w2v):
    B, ns, _ = pk.shape
    pspec = pl.BlockSpec((None, ns, 4 * HEAD_DIM), lambda b: (b, 0, 0))
    bspec = pl.BlockSpec((1, HEAD_DIM), lambda b: (0, 0))
    wspec = pl.BlockSpec((HEAD_DIM, HEAD_DIM), lambda b: (0, 0))
    out = jax.ShapeDtypeStruct((B, ns, NSA_KV * HEAD_DIM), BF16)
    ospec = pl.BlockSpec((None, ns, NSA_KV * HEAD_DIM), lambda b: (b, 0, 0))
    return pl.pallas_call(
        functools.partial(_cmp_mlp_kernel, ns=ns),
        out_shape=(out, out),
        grid=(B,),
        in_specs=[pspec, pspec, bspec, bspec, wspec, wspec],
        out_specs=(ospec, ospec),
        compiler_params=_cparams(("arbitrary",)),
        name="cmp_mlp",
    )(pk, pv, b1k.reshape(1, -1), b1v.reshape(1, -1), w2k.astype(BF16), w2v.astype(BF16))


def _cmp_attn_kernel(q_ref, kc_ref, vc_ref, ov_ref, o_ref, sel_ref, score_ref, *, tq, nbat, ns, nb, q_base, widths):
    qi = pl.program_id(1)
    rows = nbat * tq
    q_pos = q_base + qi * tq + (lax.broadcasted_iota(jnp.int32, (rows, 1), 0) & (tq - 1))
    n_idx = lax.broadcasted_iota(jnp.int32, (1, ns), 1)
    cmp_end = n_idx * CMP_STRIDE + (CMP_BLOCK - 1)
    mask = (cmp_end <= q_pos) & (n_idx < ns - 1)
    overlap = ov_ref[...]
    blk = lax.broadcasted_iota(jnp.int32, (rows, nb), 1)
    blkf = blk.astype(F32)
    cur = q_pos >> SLC_SHIFT
    forced = (blk == 0) | (blk == cur) | (blk == cur - 1)
    valid = blk <= cur
    n_pick = SLC_TOPN - jnp.where(cur >= nb, 1, 0)

    def attend(w):
        for b in range(nbat):
            rs = slice(b * tq, (b + 1) * tq)
            for g in range(NSA_KV):
                kb = kc_ref[b, :w, g * HEAD_DIM:(g + 1) * HEAD_DIM]
                vb = vc_ref[b, :w, g * HEAD_DIM:(g + 1) * HEAD_DIM]
                imp = jnp.zeros((tq, w), F32)
                for h in range(GROUP):
                    c = (g * GROUP + h) * HEAD_DIM
                    qh = (q_ref[rs, c:c + HEAD_DIM] * (SCALE * LOG2E)).astype(BF16)
                    s = lax.dot_general(qh, kb, (((1,), (1,)), ((), ())), preferred_element_type=F32)
                    s = jnp.where(mask[rs, :w], s, NEG_BIG)
                    m = jnp.max(s, axis=-1, keepdims=True)
                    e = jnp.exp2(s - m)
                    any_visible = jnp.where(m > 0.5 * NEG_BIG, 1.0, 0.0)
                    p = e * (any_visible / jnp.maximum(jnp.sum(e, axis=-1, keepdims=True), 1e-30))
                    o_ref[rs, c:c + HEAD_DIM] = jnp.dot(p.astype(BF16), vb, preferred_element_type=F32)
                    imp = imp + p
                hi = imp.astype(BF16)
                r1 = imp - hi.astype(F32)
                mid = r1.astype(BF16)
                lo = (r1 - mid.astype(F32)).astype(BF16)
                ov = overlap[:w]
                imp_slc = (jnp.dot(hi, ov, preferred_element_type=F32) + jnp.dot(mid, ov, preferred_element_type=F32)
                           + jnp.dot(lo, ov, preferred_element_type=F32))
                score_ref[g, rs] = jnp.where(valid[rs], imp_slc + FORCE_BONUS * forced[rs].astype(F32), NEG_BIG)

    n_vis = (q_base + (qi + 1) * tq - CMP_BLOCK) // CMP_STRIDE + 1
    lo_w = -(2 ** 30)
    for w in widths:
        pl.when((n_vis > lo_w) & ((n_vis <= w) | (w == ns)))(functools.partial(attend, w))
        lo_w = w
    scores = [score_ref[g] for g in range(NSA_KV)]

    def pick_one(score, allowed):
        best = jnp.max(score, axis=-1, keepdims=True)
        first = jnp.min(jnp.where(score == best, blkf, float(nb)), axis=-1, keepdims=True)
        return jnp.where((blkf == first) & allowed, -jnp.inf, score)

    picked = lax.fori_loop(0, SLC_TOPN - 1, lambda it, sc: tuple(pick_one(s, True) for s in sc), tuple(scores))
    picked = [pick_one(s, n_pick >= SLC_TOPN) for s in picked]
    for g in range(NSA_KV):
        bias = jnp.where((picked[g] == -jnp.inf) & valid, 0.0, SEL_OFF).astype(BF16)
        for b in range(nbat):
            for half in range(nb // LANES):
                sel_ref[b, g, half] = bias[b * tq:(b + 1) * tq, half * LANES:(half + 1) * LANES]


def _cmp_attn(q_arr, q_col, B, n_qblk, tq, kc, vc, q_base, nb):
    ns = kc.shape[1]
    per = SLC_BLOCK // CMP_STRIDE
    ci = np.arange(ns)[:, None]
    cb = np.arange(nb)[None, :]
    overlap = jnp.asarray((ci >= per * cb - 1) & (ci <= per * cb + per - 1), BF16)
    assert tq & (tq - 1) == 0
    nbat = _pick(B, (4, 2, 1)) if n_qblk == 1 else 1
    kspec = pl.BlockSpec((nbat, ns, NSA_KV * HEAD_DIM), lambda b, i: (b, 0, 0))
    quarter = ns // 4
    widths = (quarter, 2 * quarter, 3 * quarter, ns) if (n_qblk > 1 and quarter % LANES == 0) else (ns,)
    return pl.pallas_call(
        functools.partial(_cmp_attn_kernel, tq=tq, nbat=nbat, ns=ns, nb=nb, q_base=q_base, widths=widths),
        out_shape=(jax.ShapeDtypeStruct((B * n_qblk * tq, NSA_HEADS * HEAD_DIM), F32),
                   jax.ShapeDtypeStruct((B, NSA_KV, nb // LANES, n_qblk * tq, LANES), BF16)),
        grid=(B // nbat, n_qblk),
        in_specs=[pl.BlockSpec((nbat * tq, NSA_HEADS * HEAD_DIM), lambda b, i: (b * n_qblk + i, q_col)), kspec, kspec,
                  pl.BlockSpec((ns, nb), lambda b, i: (0, 0))],
        out_specs=(pl.BlockSpec((nbat * tq, NSA_HEADS * HEAD_DIM), lambda b, i: (b * n_qblk + i, 0)),
                   pl.BlockSpec((nbat, NSA_KV, nb // LANES, tq, LANES), lambda b, i: (b, 0, 0, i, 0))),
        scratch_shapes=[pltpu.VMEM((NSA_KV, nbat * tq, nb), F32)],
        compiler_params=_cparams(("arbitrary", "arbitrary")),
        name="cmp_attn_select",
    )(q_arr, kc, vc, overlap)


def _mix_out_kernel(x_ref, of_ref, oc_ref, os_ref, ow_ref, sm_ref, fz_ref, nz_ref, w_ref, h_ref):
    mix_f = of_ref[...] * _silu(fz_ref[...])
    sm = sm_ref[...]
    parts = []
    for h in range(NSA_HEADS):
        sl = slice(h * HEAD_DIM, (h + 1) * HEAD_DIM)
        c = FOX_HEADS + h * N_BRANCH
        parts.append(sm[:, c:c + 1] * oc_ref[:, sl] + sm[:, c + 1:c + 2] * os_ref[:, sl]
                     + sm[:, c + 2:c + 3] * ow_ref[:, sl])
    mix_n = jnp.concatenate(parts, axis=1) * _silu(nz_ref[...])
    mix = jnp.concatenate([mix_f, mix_n], axis=1).astype(BF16)
    h_ref[...] = x_ref[...] + jnp.dot(mix, w_ref[...], preferred_element_type=F32)


def _mix_out(x, o_f, o_c, o_s, o_w, big, w_out, tm):
    R, D = x.shape
    row = lambda i: (i, 0)
    wide = pl.BlockSpec((tm, 1024), row)
    return pl.pallas_call(
        _mix_out_kernel,
        out_shape=jax.ShapeDtypeStruct((R, D), F32),
        grid=(R // tm,),
        in_specs=[pl.BlockSpec((tm, D), row), wide, wide, wide, wide,
                  pl.BlockSpec((tm, LANES), lambda i: (i, C_SM * 2)),
                  pl.BlockSpec((tm, 1024), lambda i: (i, C_FZ // 4)),
                  pl.BlockSpec((tm, 1024), lambda i: (i, C_NZ // 4)),
                  pl.BlockSpec(w_out.shape, lambda i: (0, 0))],
        out_specs=pl.BlockSpec((tm, D), row),
        compiler_params=_cparams(("arbitrary",)),
        name="mix_out",
    )(x, o_f, o_c, o_s, o_w, big, big, big, w_out)


def _ple_kernel(h_ref, p_ref, g_ref, wg_ref, wp_ref, y_ref):
    h = h_ref[...]
    ms = jnp.mean(h * h, axis=-1, keepdims=True)
    hn = (h * lax.rsqrt(ms + EPS) * g_ref[...]).astype(BF16)
    gate = _sigmoid(jnp.dot(hn, wg_ref[...], preferred_element_type=F32))
    y_ref[...] = h + gate * jnp.dot(p_ref[...].astype(BF16), wp_ref[...], preferred_element_type=F32)


def _ple(h, p, ple_norm, w_gate, w_ple, tm):
    R, D = h.shape
    row = lambda i: (i, 0)
    return pl.pallas_call(
        _ple_kernel,
        out_shape=jax.ShapeDtypeStruct((R, D), F32),
        grid=(R // tm,),
        in_specs=[pl.BlockSpec((tm, D), row), pl.BlockSpec((tm, p.shape[1]), row),
                  pl.BlockSpec((1, D), lambda i: (0, 0)),
                  pl.BlockSpec(w_gate.shape, lambda i: (0, 0)),
                  pl.BlockSpec(w_ple.shape, lambda i: (0, 0))],
        out_specs=pl.BlockSpec((tm, D), row),
        compiler_params=_cparams(("arbitrary",)),
        name="ple_gate",
    )(h, p, ple_norm.reshape(1, D), w_gate, w_ple)


STK = GROUP * 4
STK_SHIFT = 4


def _decode_kernel(pt_ref, *refs, P, n_tok, has_bias, has_sel):
    q_ref = refs[0]
    k_pages, v_pages = refs[1:1 + P], refs[1 + P:1 + 2 * P]
    pos = 1 + 2 * P
    cq_ref = cs_ref = cst_ref = sel_ref = None
    if has_bias:
        cq_ref, cs_ref, cst_ref = refs[pos:pos + 3]
        pos += 3
    if has_sel:
        sel_ref = refs[pos]
        pos += 1
    il_ref, kt_ref, vt_ref, o_ref, m_ref, l_ref, acc_ref = refs[pos:pos + 7]
    step = pl.program_id(1)
    n_keys = P * PAGE
    n_rows = n_keys * NSA_KV
    rows = NSA_KV * STK
    row_group = lax.broadcasted_iota(jnp.int32, (rows, 1), 0) >> STK_SHIFT

    @pl.when(step == 0)
    def _():
        m_ref[...] = jnp.full(m_ref.shape, NEG_BIG, F32)
        l_ref[...] = jnp.zeros(l_ref.shape, F32)
        acc_ref[...] = jnp.zeros(acc_ref.shape, F32)

    def update(s, pv):
        m_prev = m_ref[...]
        m_new = jnp.maximum(m_prev, jnp.max(s, axis=-1, keepdims=True))
        alpha = jnp.exp(m_prev - m_new)
        p = jnp.exp(s - m_new)
        l_ref[...] = alpha * l_ref[...] + jnp.sum(p, axis=-1, keepdims=True)
        acc_ref[...] = alpha * acc_ref[...] + pv(p.astype(BF16))
        m_ref[...] = m_new

    qb = (q_ref[...] * SCALE).astype(BF16)
    kb = jnp.concatenate([r[...] for r in k_pages], axis=0).astype(BF16)
    vb = jnp.concatenate([r[...] for r in v_pages], axis=0).astype(BF16)
    s = lax.dot_general(qb, kb, (((1,), (1,)), ((), ())), preferred_element_type=F32)
    col = lax.broadcasted_iota(jnp.int32, (1, n_rows), 1)
    if has_bias:
        spread = il_ref[...]
        parts = []
        for u in range(P):
            c = cs_ref[:, u * PAGE:(u + 1) * PAGE]
            hi = c.astype(BF16)
            r1 = c - hi.astype(F32)
            mid = r1.astype(BF16)
            lo = (r1 - mid.astype(F32)).astype(BF16)
            parts.append(jnp.dot(hi, spread, preferred_element_type=F32) + jnp.dot(mid, spread, preferred_element_type=F32)
                         + jnp.dot(lo, spread, preferred_element_type=F32))
        s = s + (cq_ref[...] - jnp.concatenate(parts, axis=1))
    if has_sel:
        blk = ((step * n_keys + (col >> 1)) >> SLC_SHIFT) & (LANES - 1)
        onehot = (lax.broadcasted_iota(jnp.int32, (LANES, n_rows), 0) == blk).astype(BF16)
        s = s + jnp.dot(sel_ref[...], onehot, preferred_element_type=F32)
    s = jnp.where((col & 1) == row_group, s, NEG_BIG)
    update(s, lambda p: jnp.dot(p, vb, preferred_element_type=F32))

    @pl.when(step == pl.num_programs(1) - 1)
    def _():
        row_tok = lax.broadcasted_iota(jnp.int32, (rows, DEC_PAD), 0) & 3
        key_tok = lax.broadcasted_iota(jnp.int32, (rows, DEC_PAD), 1)
        mask = (key_tok <= row_tok) & (key_tok < n_tok)
        by_group = lambda a0, a1: jnp.where(row_group == 0, a0, a1)
        kt = [kt_ref[:, g * HEAD_DIM:(g + 1) * HEAD_DIM].astype(BF16) for g in range(NSA_KV)]
        vt = [vt_ref[:, g * HEAD_DIM:(g + 1) * HEAD_DIM].astype(BF16) for g in range(NSA_KV)]
        st = by_group(*[lax.dot_general(qb, k, (((1,), (1,)), ((), ())), preferred_element_type=F32) for k in kt])
        if has_bias:
            st = st + (cq_ref[...] - cst_ref[...])
        update(jnp.where(mask, st, NEG_BIG),
               lambda p: by_group(*[jnp.dot(p, v, preferred_element_type=F32) for v in vt]))
        o_ref[...] = acc_ref[...] / jnp.maximum(l_ref[...], 1e-30)


def _decode_attn(q_st, cache_k, cache_v, page_table, big, k_col, v_col, P, n_tok, *, bias=None, sel=None, name):
    B, n_pages = page_table.shape
    n_pool = cache_k.shape[0]
    rows = NSA_KV * STK
    k2 = cache_k.reshape(n_pool * PAGE * NSA_KV, HEAD_DIM)
    v2 = cache_v.reshape(n_pool * PAGE * NSA_KV, HEAD_DIM)
    pt = page_table.reshape(-1)
    n_keys = P * PAGE
    whole = lambda b, s, pt_ref: (b, 0, 0)

    def page(u):
        return pl.BlockSpec((PAGE * NSA_KV, HEAD_DIM), lambda b, s, pt_ref: (pt_ref[b * n_pages + s * P + u], 0))

    in_specs = [pl.BlockSpec((None, rows, HEAD_DIM), whole)]
    in_specs += [page(u) for u in range(P)] * 2
    args = [q_st.reshape(B, rows, HEAD_DIM)] + [k2] * P + [v2] * P
    if bias is not None:
        cq, cs, cs_tail = bias
        in_specs += [pl.BlockSpec((None, rows, 1), whole),
                     pl.BlockSpec((None, rows, n_keys), lambda b, s, pt_ref: (b, 0, s)),
                     pl.BlockSpec((None, rows, DEC_PAD), whole)]
        args += [cq.reshape(B, rows, 1), cs.reshape(B, rows, -1), cs_tail.reshape(B, rows, DEC_PAD)]
    if sel is not None:
        per_step = n_keys // SLC_BLOCK
        n_half = sel.shape[2]
        in_specs.append(pl.BlockSpec((None, None, rows, LANES), lambda b, s, pt_ref: (b, (s * per_step) // LANES, 0, 0)))
        args.append(sel.transpose(0, 2, 1, 3, 4).reshape(B, n_half, rows, LANES))
    spread = jnp.asarray(np.arange(PAGE)[:, None] == (np.arange(PAGE * NSA_KV)[None, :] >> 1), BF16)
    in_specs += [pl.BlockSpec((PAGE, PAGE * NSA_KV), lambda b, s, pt_ref: (0, 0)),
                 pl.BlockSpec((DEC_PAD, CH), lambda b, s, pt_ref: (b, k_col)),
                 pl.BlockSpec((DEC_PAD, CH), lambda b, s, pt_ref: (b, v_col))]
    args += [spread, big, big]
    out = pl.pallas_call(
        functools.partial(_decode_kernel, P=P, n_tok=n_tok, has_bias=bias is not None, has_sel=sel is not None),
        out_shape=jax.ShapeDtypeStruct((B, rows, HEAD_DIM), F32),
        grid_spec=pltpu.PrefetchScalarGridSpec(
            num_scalar_prefetch=1,
            grid=(B, n_pages // P),
            in_specs=in_specs,
            out_specs=pl.BlockSpec((None, rows, HEAD_DIM), whole),
            scratch_shapes=[pltpu.VMEM((rows, 1), F32), pltpu.VMEM((rows, 1), F32), pltpu.VMEM((rows, HEAD_DIM), F32)],
        ),
        compiler_params=_cparams(("arbitrary", "arbitrary")),
        name=name,
    )(pt, *args)
    return out.reshape(B, NSA_KV, STK, HEAD_DIM)


def _stack_rows(a, n_tok):
    B = a.shape[0]
    x = a.shape[-1] // (NSA_KV * GROUP)
    return a[:, :n_tok].reshape(B, n_tok, NSA_KV, GROUP, x).transpose(0, 2, 3, 1, 4).reshape(B, NSA_KV, STK, x)


def _unstack_rows(o_st, n_tok):
    B = o_st.shape[0]
    o = o_st.reshape(B, NSA_KV, GROUP, n_tok, HEAD_DIM).transpose(0, 3, 1, 2, 4).reshape(B, n_tok, -1)
    return jnp.pad(o, ((0, 0), (0, DEC_PAD - n_tok), (0, 0))).reshape(B * DEC_PAD, -1)


def _pick(n, cands):
    for c in cands:
        if n % c == 0:
            return c
    raise ValueError(f"no tile in {cands} divides {n}")


def _col(big, c, width=CH):
    return big[:, c * CH:c * CH + width]


def _prompt_layer(x, p_i, prm):
    T, D = x.shape
    big = _inproj(x, jnp.arange(T, dtype=jnp.int32), prm["attn_norm"], prm["w_perm"], prm["gain"], prm["bf_pad"],
                  _pick(T, (1024, 512, 256, 128)))
    logf = big[:, C_SM * CH:C_SM * CH + FOX_HEADS]
    nr = T // LANES
    c4, _ = _cumsum(logf.reshape(1, nr, LANES, FOX_HEADS).transpose(0, 1, 3, 2),
                    jnp.zeros((1, FOX_HEADS, DEC_PAD), F32))
    c_rows = c4.transpose(0, 1, 3, 2).reshape(T, FOX_HEADS)

    tq, tk = _pick(T, (1024, 512)), 512
    qm, fa, kf, vf, ks, vs, kw, vw, nrm, *new_rows = _pack(big, c_rows, tk)
    tabs = _prompt_tables(T, tq, tk)
    o_f = _pflash(qm, 0, kf, vf, tabs, tq, tk, aug=fa, aug_mode="head",
                  skip=_fox_skip(tabs, nrm, c_rows, T, tq, tk), name="fox_prompt")

    pk, pv = _cmp_partial_prompt(big, T, prm["w1k"], prm["w1v"], _pick(T // CMP_STRIDE, (256, 128, 64)))
    kc, vc = _cmp_mlp(pk[None], pv[None], prm["b1k"], prm["b1v"], prm["w2k"], prm["w2v"])
    nb = -(-(T // SLC_BLOCK) // LANES) * LANES
    tqc = _pick(T, (256, 128))
    o_c, sel = _cmp_attn(big, C_NQ // 4, 1, T // tqc, tqc, kc, vc, 0, nb)
    o_s = _pflash(qm, 1, ks, vs, tabs, tq, tk, aug=sel[0], aug_mode="group", name="slc_prompt")
    o_w = _pflash(qm, 1, kw, vw, _prompt_tables(T, tk, tk, window=True), tk, tk, name="win_prompt")

    tm = _pick(T, (256, 128))
    h = _mix_out(x, o_f, o_c, o_s, o_w, big, prm["w_out"], tm)
    y = _ple(h, p_i, prm["ple_norm"], prm["w_gate"], prm["w_ple"], tm)
    n_win = min(WINDOW, T)
    kv5 = lambda c: new_rows[c - C_FK].reshape(1, 1, T, 2, HEAD_DIM)
    state = (kv5(C_FK), kv5(C_FV), logf.reshape(1, 1, T, FOX_HEADS), kv5(C_CK), kv5(C_CV), kv5(C_SK), kv5(C_SV),
             kv5(C_WK)[:, :, T - n_win:], kv5(C_WV)[:, :, T - n_win:])
    return y.reshape(1, T, D), state


def _sample_layer(x, p_i, caches, page_table, prm):
    c_fk, c_fv, c_flogf, c_ck, c_cv, c_sk, c_sv, c_wk, c_wv = caches
    B, Tn, D = x.shape
    n_pages = page_table.shape[1]
    past = n_pages * PAGE
    R = B * DEC_PAD
    xp = jnp.pad(x, ((0, 0), (0, DEC_PAD - Tn), (0, 0))).reshape(R, D)
    pos = jnp.tile(past + jnp.arange(DEC_PAD, dtype=jnp.int32), B)
    big = _inproj(xp, pos, prm["attn_norm"], prm["w_perm"], prm["gain"], prm["bf_pad"], _pick(R, (256, 128, 16)))
    logf = big[:, C_SM * CH:C_SM * CH + FOX_HEADS]

    assert Tn * GROUP == STK, "decode kernel stacks 4 heads x 4 new tokens per KV group"
    P = _pick(n_pages, (32, 16, 8, 4, 2, 1))
    lf_pages = _gather_logf_pages(c_flogf.transpose(0, 2, 1), page_table, _pick(n_pages, (32, 16, 8, 4, 2, 1)))
    c4, c_new = _cumsum(lf_pages, logf.reshape(B, DEC_PAD, FOX_HEADS).transpose(0, 2, 1))
    rep = lambda a: jnp.repeat(a, Tn, axis=2)
    cs_past = rep(c4.transpose(0, 2, 1, 3).reshape(B, FOX_KV, GROUP, past))
    cs_tail = rep(c_new.reshape(B, FOX_KV, GROUP, DEC_PAD))
    cq = c_new[:, :, :Tn].reshape(B, FOX_KV, STK, 1)

    q3 = lambda c: big[:, c * CH:c * CH + 1024].reshape(B, DEC_PAD, 1024)
    o_f = _unstack_rows(
        _decode_attn(_stack_rows(q3(C_FQ), Tn), c_fk, c_fv, page_table, big, C_FK, C_FV, P, Tn,
                     bias=(cq, cs_past, cs_tail), name="fox_decode"), Tn)

    pk, pv = _cmp_partial_paged(c_ck, c_cv, page_table, prm["w1k"], prm["w1v"], _pick(n_pages, (32, 16, 8, 4, 2, 1)))
    kc, vc = _cmp_mlp(pk, pv, prm["b1k"], prm["b1v"], prm["w2k"], prm["w2v"])
    nb = -(-(past // SLC_BLOCK) // LANES) * LANES
    o_c, sel = _cmp_attn(big, C_NQ // 4, B, 1, DEC_PAD, kc, vc, past, nb)
    sel_st = jnp.tile(sel[:, :, :, None, :Tn], (1, 1, 1, GROUP, 1, 1)).reshape(B, NSA_KV, nb // LANES, STK, LANES)
    o_s = _unstack_rows(
        _decode_attn(_stack_rows(q3(C_NQ), Tn), c_sk, c_sv, page_table, big, C_SK, C_SV, P, Tn,
                     sel=sel_st, name="slc_decode"), Tn)
    n_buf = c_wk.shape[1]
    wspec = [pl.BlockSpec((n_buf, HEAD_DIM), lambda b, g, s, *_: (b, g))]
    wflat = lambda c: c.reshape(B * n_buf, NSA_KV * HEAD_DIM)
    o_w = _flash(big, C_NQ // 2, DEC_PAD, 1, B, _linear_tables(1), wspec, wspec, [wflat(c_wk), wflat(c_wv)], n_buf,
                 tail=(big, C_WK * 2, big, C_WV * 2), windowed=True,
                 q_base=past, k_base=past - n_buf, tail_base=past, name="win_decode")

    tm = _pick(R, (256, 128, 16))
    pp = jnp.pad(p_i, ((0, 0), (0, DEC_PAD - Tn), (0, 0))).reshape(R, -1)
    h = _mix_out(xp, o_f, o_c, o_s, o_w, big, prm["w_out"], tm)
    y = _ple(h, pp, prm["ple_norm"], prm["w_gate"], prm["w_ple"], tm)
    y = y.reshape(B, DEC_PAD, D)[:, :Tn]
    new = lambda c: _col(big, c).reshape(B, DEC_PAD, 2, HEAD_DIM)[:, :Tn]
    wk_new, wv_new = new(C_WK), new(C_WV)
    kw = jnp.concatenate([c_wk, wk_new], axis=1)[:, -n_buf:]
    vw = jnp.concatenate([c_wv, wv_new], axis=1)[:, -n_buf:]
    state = (new(C_FK), new(C_FV), logf.reshape(B, DEC_PAD, FOX_HEADS)[:, :Tn], new(C_CK), new(C_CV), new(C_SK),
             new(C_SV), kw, vw)
    return y, tuple(s[None] for s in state)


def kernel(x_prompt, x_sample, cache_fox_k, cache_fox_v, cache_fox_logf, cache_cmp_k, cache_cmp_v, cache_slc_k,
           cache_slc_v, cache_win_k, cache_win_v, page_table, p_prompt, p_sample, attn_norm, w_in, b_forget,
           fox_q_norm, fox_k_norm, nsa_q_norm, nsa_k_norm, cmp_k_w1, cmp_k_b1, cmp_k_w2, cmp_v_w1, cmp_v_b1,
           cmp_v_w2, w_out, ple_norm, w_ple, w_ple_gate):
    assert x_prompt.shape[0] == 1 and w_in.shape[0] == 1, "one prompt sequence, one layer"
    w_perm, gain, bf_pad = _prep_inproj_params(w_in[0], b_forget[0], fox_q_norm[0], fox_k_norm[0], nsa_q_norm[0],
                                               nsa_k_norm[0])
    prm = dict(attn_norm=attn_norm[0], w_perm=w_perm, gain=gain, bf_pad=bf_pad,
               w1k=cmp_k_w1[0].astype(BF16), w1v=cmp_v_w1[0].astype(BF16), b1k=cmp_k_b1[0], b1v=cmp_v_b1[0],
               w2k=cmp_k_w2[0], w2v=cmp_v_w2[0], w_out=w_out[0].astype(BF16), ple_norm=ple_norm[0],
               w_gate=w_ple_gate[0].astype(BF16), w_ple=w_ple[0].astype(BF16))
    y_p, st_p = _prompt_layer(x_prompt[0], p_prompt[0, 0], prm)
    caches = (cache_fox_k[0], cache_fox_v[0], cache_fox_logf[0], cache_cmp_k[0], cache_cmp_v[0], cache_slc_k[0],
              cache_slc_v[0], cache_win_k[0], cache_win_v[0])
    y_s, st_s = _sample_layer(x_sample, p_sample[0], caches, page_table, prm)
    return (y_p, y_s) + tuple(st_p) + tuple(st_s)
```
